```python
import math
import jax
import jax.numpy as jnp
from jax import lax
import numpy as np

D_MODEL = 1024
BATCH = 4
SEQ = 8192
DEPTH = 2
DEC_BATCH = 32
DEC_SEQ = 8
PAST_LEN = 16384
PAGE_SIZE = 128

HEAD_DIM = 64
NSA_HEADS = D_MODEL // (2 * HEAD_DIM)
NSA_KV_HEADS = NSA_HEADS // 4
NSA_GROUP = NSA_HEADS // NSA_KV_HEADS
CMP_BLOCK = 32
SEL_BLOCK = 64
SEL_TOPN = 16
WINDOW = 512
Q_BLOCK = 128
ROPE_THETA = 10000.0
MLSTM_HEADS = D_MODEL // (4 * HEAD_DIM)
MLSTM_CONV = 4
MLSTM_CHUNK = 64
CMLP_GROUPS = D_MODEL // (4 * HEAD_DIM)
CMLP_CHUNK = 128
D_FF = 2816
FFN_CONV = 3
NSA_W = NSA_HEADS * HEAD_DIM
KV_W = NSA_KV_HEADS * HEAD_DIM
MLSTM_W = MLSTM_HEADS * HEAD_DIM
CMLP_W = CMLP_GROUPS * HEAD_DIM
IN_SIZES = (NSA_W, 2 * KV_W, 2 * KV_W, 2 * KV_W, 3 * NSA_HEADS, MLSTM_W, MLSTM_W, MLSTM_W, MLSTM_HEADS, MLSTM_HEADS, CMLP_W, CMLP_W)
IN_W = sum(IN_SIZES)
IN_SPLITS = tuple(int(v) for v in np.cumsum(IN_SIZES)[:-1])
FGATE_OFF = sum(IN_SIZES[:9])
EPS = 1e-6
NEG = -1e30
FORCE = 1e4

kernel_name = 'nsa_mlstm_chunkmlp_hybrid_step'


def rmsnorm(x, g):
    xf = x.astype(jnp.float32)
    y = xf * lax.rsqrt(jnp.mean(xf * xf, axis=-1, keepdims=True) + EPS)
    return (y * g.astype(jnp.float32)).astype(x.dtype)


def rope(x, pos):
    half = HEAD_DIM // 2
    inv = jnp.power(ROPE_THETA, -jnp.arange(half, dtype=jnp.float32) / half)
    ang = pos.astype(jnp.float32)[:, None] * inv[None, :]
    cos = jnp.cos(ang)[None, :, None, :]
    sin = jnp.sin(ang)[None, :, None, :]
    xf = x.astype(jnp.float32)
    x1, x2 = xf[..., :half], xf[..., half:]
    return jnp.concatenate([x1 * cos - x2 * sin, x1 * sin + x2 * cos], axis=-1).astype(x.dtype)


def causal_dwconv(x, hist, w, b):
    width = w.shape[0]
    S = x.shape[1]
    xp = jnp.concatenate([hist.astype(x.dtype), x], axis=1)
    y = b
    for j in range(width):
        y = y + xp[:, j:j + S] * w[j]
    return y, xp[:, xp.shape[1] - (width - 1):]


def masked_softmax(s, mask):
    p = jax.nn.softmax(jnp.where(mask, s, NEG), axis=-1)
    return p * mask.astype(p.dtype)


def nsa_compress(kv, a, w):
    B = kv.shape[0]
    nb = kv.shape[1] // CMP_BLOCK
    blk = kv[:, :nb * CMP_BLOCK].reshape(B, nb, CMP_BLOCK, 2, NSA_KV_HEADS, HEAD_DIM)
    pooled = jnp.einsum('bnlsgd,slgd->bnsgd', blk, a)
    return jnp.einsum('bnsgd,sgde->bnsge', pooled, w)


def nsa_attention(q, gates, kv_c, kv_s, kv_w, pos0, cmp_a, cmp_w):
    B, Sq = q.shape[0], q.shape[1]
    G, R, dh = NSA_KV_HEADS, NSA_GROUP, HEAD_DIM
    f32 = jnp.float32
    Tk = kv_s.shape[1]
    cmp = nsa_compress(kv_c.astype(f32), cmp_a.astype(f32), cmp_w.astype(f32))
    k_cmp, v_cmp = cmp[:, :, 0], cmp[:, :, 1]
    nbc = k_cmp.shape[1]
    cmp_end = (jnp.arange(nbc) + 1) * CMP_BLOCK - 1
    ratio = SEL_BLOCK // CMP_BLOCK
    nbs = -(-Tk // SEL_BLOCK)
    ntop = min(SEL_TOPN, nbs)
    sel = jnp.pad(kv_s, ((0, 0), (0, nbs * SEL_BLOCK - Tk), (0, 0), (0, 0), (0, 0)))
    sel = sel.reshape(B, nbs, SEL_BLOCK, 2, G, dh).transpose(0, 4, 1, 2, 3, 5).reshape(B, G, nbs, SEL_BLOCK * 2 * dh)
    blk = min(Q_BLOCK, Sq)
    nq = -(-Sq // blk)
    padq = nq * blk - Sq
    qg = jnp.pad(q, ((0, 0), (0, padq), (0, 0), (0, 0))).reshape(B, nq * blk, G, R, dh)
    gg = jnp.pad(gates, ((0, 0), (0, padq), (0, 0), (0, 0))).reshape(B, nq * blk, G, R, 3)
    kvw = jnp.pad(kv_w, ((0, 0), (0, padq), (0, 0), (0, 0), (0, 0)))
    blk_ids = jnp.arange(nbs)

    def one_block(i):
        start = i * blk
        qpos = pos0 + start + jnp.arange(blk)
        qb = lax.dynamic_slice_in_dim(qg, start, blk, axis=1).astype(f32) * (HEAD_DIM ** -0.5)
        gb = lax.dynamic_slice_in_dim(gg, start, blk, axis=1).astype(f32)
        s_c = jnp.einsum('bqgrd,bngd->bgrqn', qb, k_cmp)
        p_c = masked_softmax(s_c, cmp_end[None, :] <= qpos[:, None])
        o_c = jnp.einsum('bgrqn,bngd->bqgrd', p_c, v_cmp)
        imp = p_c.sum(axis=2)
        imp = jnp.pad(imp, ((0, 0), (0, 0), (0, 0), (0, nbs * ratio - nbc))).reshape(B, G, blk, nbs, ratio).sum(-1)
        forced = (blk_ids[None, :] == (qpos // SEL_BLOCK)[:, None]) | (blk_ids[None, :] == 0)
        valid = blk_ids[None, :] * SEL_BLOCK <= qpos[:, None]
        imp = jnp.where(valid, imp + FORCE * forced.astype(f32), NEG)
        top_v, top_i = lax.top_k(imp, ntop)
        top_ok = top_v > 0.5 * NEG
        gat = jnp.take_along_axis(sel, top_i.reshape(B, G, blk * ntop, 1), axis=2)
        gat = gat.reshape(B, G, blk, ntop, SEL_BLOCK, 2, dh).astype(f32)
        s_s = jnp.einsum('bqgrd,bgqnld->bgrqnl', qb, gat[..., 0, :]).reshape(B, G, R, blk, ntop * SEL_BLOCK)
        kpos = top_i[..., None] * SEL_BLOCK + jnp.arange(SEL_BLOCK)
        m_s = (top_ok[..., None] & (kpos <= qpos[None, None, :, None, None])).reshape(B, G, 1, blk, ntop * SEL_BLOCK)
        p_s = masked_softmax(s_s, m_s)
        o_s = jnp.einsum('bgrqk,bgqkd->bqgrd', p_s, gat[..., 1, :].reshape(B, G, blk, ntop * SEL_BLOCK, dh))
        wrow = lax.dynamic_slice_in_dim(kvw, start, blk + WINDOW, axis=1).astype(f32)
        wpos = pos0 - WINDOW + start + jnp.arange(blk + WINDOW)
        m_w = (wpos[None, :] <= qpos[:, None]) & (wpos[None, :] > qpos[:, None] - WINDOW) & (wpos[None, :] >= 0)
        s_w = jnp.einsum('bqgrd,bkgd->bgrqk', qb, wrow[:, :, 0])
        o_w = jnp.einsum('bgrqk,bkgd->bqgrd', masked_softmax(s_w, m_w), wrow[:, :, 1])
        return gb[..., 0:1] * o_c + gb[..., 1:2] * o_s + gb[..., 2:3] * o_w

    out = lax.map(one_block, jnp.arange(nq))
    out = jnp.moveaxis(out, 0, 1).reshape(B, nq * blk, NSA_W)[:, :Sq]
    return out.astype(q.dtype)


def mlstm_chunkwise(q, k, v, log_i, log_f, C0, n0, m0):
    B, S, H, dh = q.shape
    f32 = jnp.float32
    L = min(MLSTM_CHUNK, S)
    nc = -(-S // L)
    pad = nc * L - S

    def chunks4(t):
        t = jnp.pad(t.astype(f32), ((0, 0), (0, pad), (0, 0), (0, 0)))
        return t.reshape(B, nc, L, H, dh).transpose(1, 0, 3, 2, 4)

    def chunks3(t, fill):
        t = jnp.pad(t.astype(f32), ((0, 0), (0, pad), (0, 0)), constant_values=fill)
        return t.reshape(B, nc, L, H).transpose(1, 0, 3, 2)

    causal = jnp.tril(jnp.ones((L, L), dtype=bool))

    def step(carry, inp):
        C, n, m = carry
        qc, kc, vc, li, lf = inp
        b = jnp.cumsum(lf, axis=-1)
        D = jnp.where(causal, b[..., :, None] - b[..., None, :] + li[..., None, :], NEG)
        inter = b + m[..., None]
        mt = jnp.maximum(inter, D.max(-1))
        A = jnp.einsum('bhtd,bhsd->bhts', qc, kc) * jnp.exp(D - mt[..., None])
        wi = jnp.exp(inter - mt)
        num = wi[..., None] * jnp.einsum('bhvd,bhtd->bhtv', C, qc) + jnp.einsum('bhts,bhsv->bhtv', A, vc)
        den = jnp.maximum(jnp.abs(wi * jnp.einsum('bhd,bhtd->bht', n, qc) + A.sum(-1)), jnp.exp(-mt))
        h = num / den[..., None]
        bl = b[..., -1]
        g = bl[..., None] - b + li
        m_new = jnp.maximum(bl + m, g.max(-1))
        w = jnp.exp(g - m_new[..., None])
        dec = jnp.exp(bl + m - m_new)
        C_new = dec[..., None, None] * C + jnp.einsum('bhs,bhsv,bhsd->bhvd', w, vc, kc)
        n_new = dec[..., None] * n + jnp.einsum('bhs,bhsd->bhd', w, kc)
        return (C_new, n_new, m_new), h

    (C, n, m), hs = lax.scan(step, (C0.astype(f32), n0.astype(f32), m0.astype(f32)),
                             (chunks4(q), chunks4(k), chunks4(v), chunks3(log_i, NEG), chunks3(log_f, 0.0)))
    h = hs.transpose(1, 0, 3, 2, 4).reshape(B, nc * L, H, dh)[:, :S]
    return h, C, n, m


def chunk_mlp(u, v, w_s, b_s):
    B, S = u.shape[0], u.shape[1]
    nch = -(-S // CMLP_CHUNK)
    pad = nch * CMLP_CHUNK - S
    vc = jnp.pad(v, ((0, 0), (0, pad), (0, 0))).reshape(B, nch, CMLP_CHUNK, CMLP_GROUPS, HEAD_DIM)
    s = jnp.einsum('gts,bnsgc->bntgc', jnp.tril(w_s), vc) + b_s.T[None, None, :, :, None]
    s = s.reshape(B, nch * CMLP_CHUNK, CMLP_W)[:, :S]
    return u * s


def layer_forward(x, c, pos0, kv_past, win_past, mC, mn, mm, mconv, fconv, P):
    B, S = x.shape[0], x.shape[1]
    G, dh = NSA_KV_HEADS, HEAD_DIM
    mod = jnp.einsum('bd,de->be', jax.nn.silu(c), P['ada_w']) + P['ada_b']
    sh1, sc1, g1, sh2, sc2, g2 = jnp.split(mod[:, None, :], 6, axis=-1)
    h = rmsnorm(x, P['norm1_g']) * (1.0 + sc1) + sh1
    z = jnp.einsum('bsd,de->bse', h, P['w_in']) + P['b_in']
    q, kvc, kvs, kvw, gt, mqk, mv, mo, mi, mf, cu, cv = jnp.split(z, IN_SPLITS, axis=-1)
    pos = pos0 + jnp.arange(S, dtype=jnp.int32)

    q = rope(q.reshape(B, S, NSA_HEADS, dh), pos)

    def kv_rows(t):
        t = t.reshape(B, S, 2, G, dh)
        return jnp.stack([rope(t[:, :, 0], pos), t[:, :, 1]], axis=2)

    kvc, kvs, kvw = kv_rows(kvc), kv_rows(kvs), kv_rows(kvw)
    new_rows = jnp.concatenate([kvc, kvs], axis=2)
    if kv_past is None:
        full_c, full_s = kvc, kvs
        win_hist = jnp.zeros((B, WINDOW, 2, G, dh), kvw.dtype)
        n_keep = min(WINDOW, S)
    else:
        full = jnp.concatenate([kv_past.astype(new_rows.dtype), new_rows], axis=1)
        full_c, full_s = full[:, :, 0:2], full[:, :, 2:4]
        n_keep = win_past.shape[1]
        win_hist = jnp.pad(win_past.astype(kvw.dtype), ((0, 0), (WINDOW - n_keep, 0), (0, 0), (0, 0), (0, 0)))
    win_all = jnp.concatenate([win_hist, kvw], axis=1)
    new_win = win_all[:, win_all.shape[1] - n_keep:]
    gates = jax.nn.sigmoid(gt).reshape(B, S, NSA_HEADS, 3)
    o_nsa = nsa_attention(q, gates, full_c, full_s, win_all, pos0, P['cmp_a'], P['cmp_w'])

    qk_c, new_mconv = causal_dwconv(mqk, mconv, P['m_conv_w'], P['m_conv_b'])
    qk_c = jax.nn.silu(qk_c).reshape(B, S, MLSTM_HEADS, dh)
    mq = jnp.einsum('bshd,hde->bshe', qk_c, P['m_wq'])
    mk = jnp.einsum('bshd,hde->bshe', qk_c, P['m_wk']) * (dh ** -0.5)
    hm, C1, n1, m1 = mlstm_chunkwise(mq, mk, mv.reshape(B, S, MLSTM_HEADS, dh), mi,
                                     jax.nn.log_sigmoid(mf.astype(jnp.float32)), mC, mn, mm)
    hm = rmsnorm(hm, P['m_norm_g']).reshape(B, S, MLSTM_W)
    hm = (jax.nn.sigmoid(mo.astype(jnp.float32)) * hm).astype(x.dtype)

    cu = jax.nn.gelu(cu)
    cv = rmsnorm(jax.nn.gelu(cv), P['c_norm_g'])
    o_cm = chunk_mlp(cu, cv, P['c_ws'], P['c_bs'])

    mix = jnp.einsum('bse,ed->bsd', jnp.concatenate([o_nsa, hm, o_cm], axis=-1), P['w_out'])
    x = x + g1 * mix

    h2 = rmsnorm(x, P['norm2_g']) * (1.0 + sc2) + sh2
    up = jnp.einsum('bsd,df->bsf', h2, P['w_up'])
    up, new_fconv = causal_dwconv(up, fconv, P['f_conv_w'], P['f_conv_b'])
    a, bv = jnp.split(up, 2, axis=-1)
    x = x + g2 * jnp.einsum('bsf,fd->bsd', jax.nn.silu(a) * bv, P['w_down'])
    return x, (new_rows, new_win, C1, n1, m1, new_mconv, cv, new_fconv)


def setup_inputs(seed: int = 0) -> dict:
    key = jax.random.key(seed)
    ks = iter(jax.random.split(key, 40))
    f32 = jnp.float32

    def nrm(shape, s):
        return jax.random.normal(next(ks), shape, f32) * s

    G, dh = NSA_KV_HEADS, HEAD_DIM
    n_pages = PAST_LEN // PAGE_SIZE
    n_used = DEC_BATCH * n_pages
    n_pool = n_used + max(1, n_used // 4)
    wb = min(WINDOW, PAST_LEN)
    x_prompt = nrm((BATCH, SEQ, D_MODEL), 1.0)
    x_sample = nrm((DEC_BATCH, DEC_SEQ, D_MODEL), 1.0)
    cache_nsa_kv = nrm((DEPTH, n_pool, PAGE_SIZE, 4, G, dh), 1.0)
    page_table = jax.random.permutation(next(ks), n_pool)[:n_used].reshape(DEC_BATCH, n_pages).astype(jnp.int32)
    cache_win_kv = nrm((DEPTH, DEC_BATCH, wb, 2, G, dh), 1.0)
    state_mlstm_C = nrm((DEPTH, DEC_BATCH, MLSTM_HEADS, dh, dh), 0.1)
    state_mlstm_n = nrm((DEPTH, DEC_BATCH, MLSTM_HEADS, dh), 0.1)
    state_mlstm_m = nrm((DEPTH, DEC_BATCH, MLSTM_HEADS), 1.0)
    state_mlstm_conv = nrm((DEPTH, DEC_BATCH, MLSTM_CONV - 1, MLSTM_W), 1.0)
    state_ffn_conv = nrm((DEPTH, DEC_BATCH, FFN_CONV - 1, 2 * D_FF), 1.0)
    c_prompt = nrm((BATCH, D_MODEL), 1.0)
    c_sample = nrm((DEC_BATCH, D_MODEL), 1.0)
    ada_w = nrm((DEPTH, D_MODEL, 6 * D_MODEL), 0.5 * D_MODEL ** -0.5)
    ada_b = nrm((DEPTH, 6 * D_MODEL), 0.02)
    norm1_g = 1.0 + nrm((DEPTH, D_MODEL), 0.02)
    norm2_g = 1.0 + nrm((DEPTH, D_MODEL), 0.02)
    w_in = nrm((DEPTH, D_MODEL, IN_W), D_MODEL ** -0.5)
    b_in = nrm((DEPTH, IN_W), 0.02).at[:, FGATE_OFF:FGATE_OFF + MLSTM_HEADS].add(3.0)
    cmp_a = nrm((DEPTH, 2, CMP_BLOCK, G, dh), CMP_BLOCK ** -0.5)
    cmp_w = nrm((DEPTH, 2, G, dh, dh), dh ** -0.5)
    m_conv_w = nrm((DEPTH, MLSTM_CONV, MLSTM_W), MLSTM_CONV ** -0.5)
    m_conv_b = nrm((DEPTH, MLSTM_W), 0.02)
    m_wq = nrm((DEPTH, MLSTM_HEADS, dh, dh), dh ** -0.5)
    m_wk = nrm((DEPTH, MLSTM_HEADS, dh, dh), dh ** -0.5)
    m_norm_g = 1.0 + nrm((DEPTH, MLSTM_HEADS, dh), 0.02)
    c_norm_g = 1.0 + nrm((DEPTH, CMLP_W), 0.02)
    c_ws = nrm((DEPTH, CMLP_GROUPS, CMLP_CHUNK, CMLP_CHUNK), CMLP_CHUNK ** -0.5)
    c_bs = 1.0 + nrm((DEPTH, CMLP_GROUPS, CMLP_CHUNK), 0.02)
    w_out = nrm((DEPTH, D_MODEL, D_MODEL), D_MODEL ** -0.5)
    w_up = nrm((DEPTH, D_MODEL, 2 * D_FF), D_MODEL ** -0.5)
    f_conv_w = nrm((DEPTH, FFN_CONV, 2 * D_FF), FFN_CONV ** -0.5)
    f_conv_b = nrm((DEPTH, 2 * D_FF), 0.02)
    w_down = nrm((DEPTH, D_FF, D_MODEL), D_FF ** -0.5)
    final_norm_g = 1.0 + nrm((D_MODEL,), 0.02)
    return {'x_prompt': x_prompt, 'x_sample': x_sample, 'cache_nsa_kv': cache_nsa_kv, 'page_table': page_table,
            'cache_win_kv': cache_win_kv, 'state_mlstm_C': state_mlstm_C, 'state_mlstm_n': state_mlstm_n,
            'state_mlstm_m': state_mlstm_m, 'state_mlstm_conv': state_mlstm_conv, 'state_ffn_conv': state_ffn_conv,
            'c_prompt': c_prompt, 'c_sample': c_sample, 'ada_w': ada_w, 'ada_b': ada_b, 'norm1_g': norm1_g,
            'norm2_g': norm2_g, 'w_in': w_in, 'b_in': b_in, 'cmp_a': cmp_a, 'cmp_w': cmp_w, 'm_conv_w': m_conv_w,
            'm_conv_b': m_conv_b, 'm_wq': m_wq, 'm_wk': m_wk, 'm_norm_g': m_norm_g, 'c_norm_g': c_norm_g,
            'c_ws': c_ws, 'c_bs': c_bs, 'w_out': w_out, 'w_up': w_up, 'f_conv_w': f_conv_w, 'f_conv_b': f_conv_b,
            'w_down': w_down, 'final_norm_g': final_norm_g}


def reference(x_prompt, x_sample, cache_nsa_kv, page_table, cache_win_kv, state_mlstm_C, state_mlstm_n,
              state_mlstm_m, state_mlstm_conv, state_ffn_conv, c_prompt, c_sample, ada_w, ada_b, norm1_g,
              norm2_g, w_in, b_in, cmp_a, cmp_w, m_conv_w, m_conv_b, m_wq, m_wk, m_norm_g, c_norm_g, c_ws,
              c_bs, w_out, w_up, f_conv_w, f_conv_b, w_down, final_norm_g):
    f32 = jnp.float32
    Bp = x_prompt.shape[0]
    Bs = x_sample.shape[0]
    n_pages = page_table.shape[1]
    past_len = n_pages * PAGE_SIZE
    xp, xs = x_prompt, x_sample
    st_p, st_s = [], []
    for l in range(DEPTH):
        P = {'ada_w': ada_w[l], 'ada_b': ada_b[l], 'norm1_g': norm1_g[l], 'norm2_g': norm2_g[l],
             'w_in': w_in[l], 'b_in': b_in[l], 'cmp_a': cmp_a[l], 'cmp_w': cmp_w[l], 'm_conv_w': m_conv_w[l],
             'm_conv_b': m_conv_b[l], 'm_wq': m_wq[l], 'm_wk': m_wk[l], 'm_norm_g': m_norm_g[l],
             'c_norm_g': c_norm_g[l], 'c_ws': c_ws[l], 'c_bs': c_bs[l], 'w_out': w_out[l], 'w_up': w_up[l],
             'f_conv_w': f_conv_w[l], 'f_conv_b': f_conv_b[l], 'w_down': w_down[l]}
        xp, sp = layer_forward(
            xp, c_prompt, 0, None, None,
            jnp.zeros((Bp, MLSTM_HEADS, HEAD_DIM, HEAD_DIM), f32), jnp.zeros((Bp, MLSTM_HEADS, HEAD_DIM), f32),
            jnp.zeros((Bp, MLSTM_HEADS), f32), jnp.zeros((Bp, MLSTM_CONV - 1, MLSTM_W), xp.dtype),
            jnp.zeros((Bp, FFN_CONV - 1, 2 * D_FF), xp.dtype), P)
        kv_past = cache_nsa_kv[l][page_table].reshape(Bs, past_len, 4, NSA_KV_HEADS, HEAD_DIM)
        xs, ss = layer_forward(
            xs, c_sample, past_len, kv_past, cache_win_kv[l], state_mlstm_C[l], state_mlstm_n[l],
            state_mlstm_m[l], state_mlstm_conv[l], state_ffn_conv[l], P)
        st_p.append(sp)
        st_s.append(ss)
    sp = [jnp.stack(t) for t in zip(*st_p)]
    ss = [jnp.stack(t) for t in zip(*st_s)]
    y_prompt = rmsnorm(xp, final_norm_g)
    y_sample = rmsnorm(xs, final_norm_g)
    kv_rows_prompt, win_prompt, mlstm_C_prompt, mlstm_n_prompt, mlstm_m_prompt, mlstm_conv_prompt, ffn_conv_prompt = sp[0], sp[1], sp[2], sp[3], sp[4], sp[5], sp[7]
    kv_rows_sample, win_sample, mlstm_C_sample, mlstm_n_sample, mlstm_m_sample, mlstm_conv_sample, cmlp_v_sample, ffn_conv_sample = ss[0], ss[1], ss[2], ss[3], ss[4], ss[5], ss[6], ss[7]
    return (y_prompt, y_sample, kv_rows_prompt, kv_rows_sample, win_prompt, win_sample,
            mlstm_C_prompt, mlstm_n_prompt, mlstm_m_prompt, mlstm_conv_prompt,
            mlstm_C_sample, mlstm_n_sample, mlstm_m_sample, mlstm_conv_sample,
            cmlp_v_sample, ffn_conv_prompt, ffn_conv_sample)
```

```python
import functools

import numpy as np
import jax
import jax.numpy as jnp
from jax import lax
from jax.experimental import pallas as pl
from jax.experimental.pallas import tpu as pltpu

F32 = jnp.float32
BF16 = jnp.bfloat16

D_MODEL = 1024
HEAD_DIM = 64
NSA_HEADS = 8
NSA_KV_HEADS = 2
NSA_GROUP = 4
CMP_BLOCK = 32
SEL_BLOCK = 64
SEL_TOPN = 16
WINDOW = 512
ROPE_THETA = 10000.0
MLSTM_HEADS = 4
MLSTM_CONV = 4
CMLP_GROUPS = 4
CMLP_CHUNK = 128
D_FF = 2816
FFN_CONV = 3
PAGE_SIZE = 128
NSA_W = 512
KV_W = 128
MLSTM_W = 256
CMLP_W = 256
IN_SIZES = (NSA_W, 2 * KV_W, 2 * KV_W, 2 * KV_W, 3 * NSA_HEADS, MLSTM_W, MLSTM_W, MLSTM_W, MLSTM_HEADS,
            MLSTM_HEADS, CMLP_W, CMLP_W)
IN_OFF = tuple(int(v) for v in np.cumsum((0,) + IN_SIZES))
IN_W = IN_OFF[-1]
EPS = 1e-6
NEG = -1e30
FORCE = 1e4
LANE = 128
NP_COLS = 2688
NSW_COLS = 896
VMEM_LIMIT = 56 * 1024 * 1024
PAGES_PER_STEP = 16
MLSTM_L = 128


def _dot(a, b):
    return jnp.dot(a, b, preferred_element_type=F32)


def _dot_nt(a, b):
    return lax.dot_general(a, b, (((1,), (1,)), ((), ())), preferred_element_type=F32)


def _dot01(m01, x):
    hi = x.astype(BF16)
    r1 = x - hi.astype(F32)
    mid = r1.astype(BF16)
    lo = (r1 - mid.astype(F32)).astype(BF16)
    return _dot(m01, hi) + _dot(m01, mid) + _dot(m01, lo)


def _dot01_r(x, m01):
    hi = x.astype(BF16)
    r1 = x - hi.astype(F32)
    mid = r1.astype(BF16)
    lo = (r1 - mid.astype(F32)).astype(BF16)
    return _dot(hi, m01) + _dot(mid, m01) + _dot(lo, m01)


def _sigmoid(x):
    return 1.0 / (1.0 + jnp.exp(-x))


def _silu(x):
    return x * _sigmoid(x)


def _gelu_tanh(x):
    return 0.5 * x * (1.0 + jnp.tanh(0.7978845608028654 * (x + 0.044715 * (x * x * x))))


def _log_sigmoid(x):
    return jnp.minimum(x, 0.0) - jnp.log(1.0 + jnp.exp(-jnp.abs(x)))


def _const_spec(shape):
    nd = len(shape)
    return pl.BlockSpec(shape, lambda *_: (0,) * nd)


def _params(sem):
    return pltpu.CompilerParams(dimension_semantics=sem, vmem_limit_bytes=VMEM_LIMIT)


def _ada_kernel(c_ref, w_ref, b_ref, o_ref):
    c = c_ref[...]
    o_ref[0] = _dot(_silu(c).astype(BF16), w_ref[0]) + b_ref[0]


def _ada(c_all, ada_w, ada_b):
    depth = ada_w.shape[0]
    n = c_all.shape[0]
    return pl.pallas_call(
        _ada_kernel,
        grid=(depth, 6),
        in_specs=[pl.BlockSpec((n, D_MODEL), lambda l, j: (0, 0)),
                  pl.BlockSpec((1, D_MODEL, D_MODEL), lambda l, j: (l, 0, j)),
                  pl.BlockSpec((1, 1, D_MODEL), lambda l, j: (l, 0, j))],
        out_specs=pl.BlockSpec((1, n, D_MODEL), lambda l, j: (l, 0, j)),
        out_shape=jax.ShapeDtypeStruct((depth, n, 6 * D_MODEL), F32),
        compiler_params=_params(("arbitrary", "arbitrary")),
        name="ada",
    )(c_all, ada_w.astype(BF16), ada_b.reshape(depth, 1, 6 * D_MODEL))


def _inproj_kernel(x_ref, mod_ref, g_ref, w_ref, b_ref, wsw_ref, bsw_ref, cos_ref, sin_ref, wc_ref, bc_ref, cg_ref,
                   rows_ref, win_ref, q_ref, kvb_ref, small_ref, mqk_ref, mv_ref, mo_ref, ocm_ref, cvn_ref, *, tc):
    x = x_ref[...]
    tm = x.shape[0]
    xn = x * lax.rsqrt(jnp.mean(x * x, axis=-1, keepdims=True) + EPS) * g_ref[...]
    mod = mod_ref[0]
    h = xn * (1.0 + mod[:, D_MODEL:2 * D_MODEL]) + mod[:, 0:D_MODEL]
    hb = h.astype(BF16)
    z = _dot(hb, w_ref[...]) + b_ref[...]
    zs = _dot(hb, wsw_ref[...]) + bsw_ref[...]
    cos = cos_ref[...]
    sin = sin_ref[...]

    def rope(c0, s0):
        return z[:, c0:c0 + LANE] * cos + zs[:, s0:s0 + LANE] * sin

    for j in range(4):
        q_ref[:, j * LANE:(j + 1) * LANE] = (rope(j * LANE, j * LANE) * (HEAD_DIM ** -0.5)).astype(q_ref.dtype)
    kc = rope(512, 512)
    vc = z[:, 640:768]
    ks = rope(768, 640)
    vs = z[:, 896:1024]
    kw = rope(1024, 768)
    vw = z[:, 1152:1280]
    rows_ref[:, 0:128] = kc
    rows_ref[:, 128:256] = vc
    rows_ref[:, 256:384] = ks
    rows_ref[:, 384:512] = vs
    win_ref[:, 0:128] = kw
    win_ref[:, 128:256] = vw
    kvb_ref[:, 0:128] = ks.astype(BF16)
    kvb_ref[:, 128:256] = vs.astype(BF16)
    kvb_ref[:, 256:384] = kw.astype(BF16)
    kvb_ref[:, 384:512] = vw.astype(BF16)
    sm = z[:, 2560:2688]
    lane = lax.broadcasted_iota(jnp.int32, sm.shape, 1)
    small_ref[...] = jnp.where(lane < 24, _sigmoid(sm), jnp.where(lane < 28, sm, _log_sigmoid(sm)))
    mqk_ref[...] = z[:, 1280:1536]
    mv_ref[...] = z[:, 1536:1792].astype(BF16)
    mo_ref[...] = _sigmoid(z[:, 1792:2048])
    cu = _gelu_tanh(z[:, 2048:2304])
    cv = _gelu_tanh(z[:, 2304:2560])
    cvn = cv * lax.rsqrt(jnp.mean(cv * cv, axis=-1, keepdims=True) + EPS) * cg_ref[...]
    cvn_ref[...] = cvn
    lane_c = lax.broadcasted_iota(jnp.int32, (tc, CMLP_W), 1) // HEAD_DIM
    for c in range(tm // tc):
        v = cvn[c * tc:(c + 1) * tc]
        vbd = jnp.concatenate([jnp.where(lane_c == g, v, 0.0) for g in range(CMLP_GROUPS)], axis=0).astype(BF16)
        s = _dot(wc_ref[...], vbd) + bc_ref[...]
        ocm_ref[c * tc:(c + 1) * tc, :] = (cu[c * tc:(c + 1) * tc] * s).astype(BF16)


def _inproj(x2d, mod, g1, wp, bp, wsw, bsw, cos_t, sin_t, wc, bc, cg, *, tm, tiles_per_batch, tc, q_dtype):
    t_total = x2d.shape[0]
    n_tiles = t_total // tm
    r = mod.shape[1]
    row = lambda w: pl.BlockSpec((tm, w), lambda i: (i, 0))
    out_shapes = (
        jax.ShapeDtypeStruct((t_total, 512), F32),
        jax.ShapeDtypeStruct((t_total, 256), F32),
        jax.ShapeDtypeStruct((t_total, 512), q_dtype),
        jax.ShapeDtypeStruct((t_total, 512), BF16),
        jax.ShapeDtypeStruct((t_total, 128), F32),
        jax.ShapeDtypeStruct((t_total, 256), F32),
        jax.ShapeDtypeStruct((t_total, 256), BF16),
        jax.ShapeDtypeStruct((t_total, 256), F32),
        jax.ShapeDtypeStruct((t_total, 256), BF16),
        jax.ShapeDtypeStruct((t_total, 256), F32),
    )
    return pl.pallas_call(
        functools.partial(_inproj_kernel, tc=tc),
        grid=(n_tiles,),
        in_specs=[row(D_MODEL),
                  pl.BlockSpec((1, r, 6 * D_MODEL), lambda i: (i // tiles_per_batch, 0, 0)),
                  _const_spec((1, D_MODEL)),
                  _const_spec((D_MODEL, NP_COLS)), _const_spec((1, NP_COLS)),
                  _const_spec((D_MODEL, NSW_COLS)), _const_spec((1, NSW_COLS)),
                  pl.BlockSpec((tm, LANE), lambda i: (i % tiles_per_batch, 0)),
                  pl.BlockSpec((tm, LANE), lambda i: (i % tiles_per_batch, 0)),
                  _const_spec(wc.shape), _const_spec(bc.shape), _const_spec((1, CMLP_W))],
        out_specs=[row(512), row(256), row(512), row(512), row(128), row(256), row(256), row(256), row(256), row(256)],
        out_shape=out_shapes,
        compiler_params=_params(("arbitrary",)),
        name="inproj",
    )(x2d, mod, g1, wp, bp, wsw, bsw, cos_t, sin_t, wc, bc, cg)


def _compress_kernel(rows_ref, a_ref, w_ref, o_ref):
    x = rows_ref[0]
    n = x.shape[0] // CMP_BLOCK
    pooled = jnp.sum(x.reshape(n, CMP_BLOCK, 2 * KV_W) * a_ref[...][None], axis=1)
    o_ref[0] = _dot(pooled.astype(BF16), w_ref[...])


def _compress(rows3, a_tile, w_bd, *, rows_per_step):
    b, s, _ = rows3.shape
    nb = s // CMP_BLOCK
    steps = s // rows_per_step
    return pl.pallas_call(
        _compress_kernel,
        grid=(b, steps),
        in_specs=[pl.BlockSpec((1, rows_per_step, 2 * KV_W), lambda i, j: (i, j, 0)),
                  _const_spec((CMP_BLOCK, 2 * KV_W)), _const_spec((2 * KV_W, 2 * KV_W))],
        out_specs=pl.BlockSpec((1, rows_per_step // CMP_BLOCK, 2 * KV_W), lambda i, j: (i, j, 0)),
        out_shape=jax.ShapeDtypeStruct((b, nb, 2 * KV_W), F32),
        compiler_params=_params(("arbitrary", "arbitrary")),
        name="compress",
    )(rows3, a_tile, w_bd)


def _group_queries(qf, g):
    lane = lax.broadcasted_iota(jnp.int32, (qf.shape[0], LANE), 1)
    keep = (lane >= HEAD_DIM * g) & (lane < HEAD_DIM * (g + 1))
    out = []
    for r in range(NSA_GROUP):
        h = NSA_GROUP * g + r
        slab = qf[:, (h // 2) * LANE:(h // 2 + 1) * LANE]
        if h % 2 != g:
            slab = pltpu.roll(slab, HEAD_DIM, axis=1)
        out.append(jnp.where(keep, slab, 0.0))
    return out


def _place_heads(pieces, g):
    lane = lax.broadcasted_iota(jnp.int32, pieces[0].shape, 1)
    slabs = []
    for j in range(2):
        halves = []
        for half in range(2):
            p = pieces[2 * j + half]
            if half != g:
                p = pltpu.roll(p, HEAD_DIM, axis=1)
            halves.append(p)
        slabs.append(jnp.where(lane < HEAD_DIM, halves[0], halves[1]))
    return slabs


def _softmax_rows(s, allowed):
    sm = jnp.where(allowed, s, NEG)
    m = jnp.max(sm, axis=-1, keepdims=True)
    e = jnp.where(allowed, jnp.exp(sm - m), 0.0)
    l = jnp.sum(e, axis=-1, keepdims=True)
    return e * (1.0 / jnp.maximum(l, 1e-30))


def _topk_select(v, jrow, n_rounds):
    big = float(v.shape[0] + 1)

    def body(_, carry):
        v, sel = carry
        m = jnp.max(v, axis=0, keepdims=True)
        first = jnp.min(jnp.where(v == m, jrow, big), axis=0, keepdims=True)
        pick = jrow == first
        sel = jnp.where(pick & (m > 0.5 * NEG), 1.0, sel)
        v = jnp.where(pick, -jnp.inf, v)
        return v, sel

    _, sel = lax.fori_loop(0, n_rounds, body, (v, jnp.zeros_like(v)))
    return sel


def _expand_tile(sel_b, t, kt):
    nblk = sel_b.shape[1]
    jrow = lax.broadcasted_iota(jnp.int32, (nblk, kt), 0) - t * (kt // SEL_BLOCK)
    cdiv = lax.broadcasted_iota(jnp.int32, (nblk, kt), 1) // SEL_BLOCK
    e = jnp.where(jrow == cdiv, 1.0, 0.0).astype(BF16)
    return _dot(sel_b, e)


def _nsa_prompt_kernel(q_ref, small_ref, kc_ref, vc_ref, kvb_ref, o_ref, m_scr, l_scr, acc_scr, *, kt, nbs):
    i = pl.program_id(1)
    tq = q_ref.shape[0]
    nbc = kc_ref.shape[1]
    s_len = kvb_ref.shape[1]
    qf = q_ref[...].astype(F32)
    gates = small_ref[...]
    q0 = i * tq
    qpos_c = q0 + lax.broadcasted_iota(jnp.int32, (tq, 1), 0)
    qpos_r = q0 + lax.broadcasted_iota(jnp.int32, (1, tq), 1)
    eye = jnp.where(lax.broadcasted_iota(jnp.int32, (tq, tq), 0) == lax.broadcasted_iota(jnp.int32, (tq, tq), 1),
                    1.0, 0.0).astype(BF16)
    qpos_r4 = q0 + lax.broadcasted_iota(jnp.int32, (1, NSA_GROUP * tq), 1) % tq
    kc = kc_ref[0].astype(BF16)
    vc = vc_ref[0].astype(BF16)
    half = nbc // 2
    rho_c = lax.broadcasted_iota(jnp.int32, (nbc, 1), 0)
    end_c = jnp.where(rho_c < half, 2 * rho_c, 2 * (rho_c - half) + 1) * CMP_BLOCK + (CMP_BLOCK - 1)
    rho_r = lax.broadcasted_iota(jnp.int32, (1, nbc), 1)
    end_r = jnp.where(rho_r < half, 2 * rho_r, 2 * (rho_r - half) + 1) * CMP_BLOCK + (CMP_BLOCK - 1)
    n_tiles = (q0 + tq - 1) // kt + 1
    w_start = pl.multiple_of(jnp.maximum(q0 - WINDOW, 0), LANE)
    w_len = WINDOW + tq
    out_slabs = []
    for g in range(NSA_KV_HEADS):
        qz4 = _group_queries(qf, g)
        qz = jnp.concatenate(qz4, axis=0).astype(BF16)
        s_c = _dot_nt(qz, kc).reshape(NSA_GROUP, tq, nbc)
        p_c = _softmax_rows(s_c, (end_r <= qpos_c)[None])
        o_c = _dot(p_c.reshape(NSA_GROUP * tq, nbc).astype(BF16), vc)
        s_t = _dot_nt(kc, qz)
        allow_t = end_c <= qpos_r4
        sm = jnp.where(allow_t, s_t, NEG)
        mt = jnp.max(sm, axis=0, keepdims=True)
        et = jnp.where(allow_t, jnp.exp(sm - mt), 0.0)
        pt = et * (1.0 / jnp.maximum(jnp.sum(et, axis=0, keepdims=True), 1e-30))
        ph = pt[:, 0:tq] + pt[:, tq:2 * tq] + pt[:, 2 * tq:3 * tq] + pt[:, 3 * tq:4 * tq]
        imp = ph[0:half] + ph[half:nbc]
        if nbs > half:
            imp = jnp.concatenate([imp, jnp.zeros((nbs - half, tq), F32)], axis=0)
        jrow_i = lax.broadcasted_iota(jnp.int32, (nbs, tq), 0)
        forced = (jrow_i == qpos_r // SEL_BLOCK) | (jrow_i == 0)
        valid = jrow_i * SEL_BLOCK <= qpos_r
        score = jnp.where(valid, imp + jnp.where(forced, FORCE, 0.0), NEG)
        sel_t = _topk_select(score, jrow_i.astype(F32), min(SEL_TOPN, nbs))
        sel_b = _dot_nt(eye, sel_t.astype(BF16)).astype(BF16)
        m_scr[...] = jnp.full(m_scr.shape, NEG, F32)
        l_scr[...] = jnp.zeros(l_scr.shape, F32)
        acc_scr[...] = jnp.zeros(acc_scr.shape, F32)

        def sel_body(t, _):
            k0 = pl.multiple_of(t * kt, kt)
            k_t = kvb_ref[0, pl.ds(k0, kt), 0:128]
            v_t = kvb_ref[0, pl.ds(k0, kt), 128:256]
            s = _dot_nt(qz, k_t).reshape(NSA_GROUP, tq, kt)
            me = _expand_tile(sel_b, t, kt)
            kpos = k0 + lax.broadcasted_iota(jnp.int32, (1, kt), 1)
            allowed = ((me > 0.5) & (kpos <= qpos_c))[None]
            sm = jnp.where(allowed, s, NEG)
            m_old = m_scr[...]
            m_new = jnp.maximum(m_old, jnp.max(sm, axis=-1, keepdims=True))
            p = jnp.where(allowed, jnp.exp(sm - m_new), 0.0)
            alpha = jnp.exp(m_old - m_new)
            l_scr[...] = alpha * l_scr[...] + jnp.sum(p, axis=-1, keepdims=True)
            pv = _dot(p.reshape(NSA_GROUP * tq, kt).astype(BF16), v_t).reshape(NSA_GROUP, tq, LANE)
            acc_scr[...] = alpha * acc_scr[...] + pv
            m_scr[...] = m_new
            return 0

        lax.fori_loop(0, n_tiles, sel_body, 0)
        o_s = (acc_scr[...] * (1.0 / jnp.maximum(l_scr[...], 1e-30))).reshape(NSA_GROUP * tq, LANE)
        k_w = kvb_ref[0, pl.ds(w_start, w_len), 256:384]
        v_w = kvb_ref[0, pl.ds(w_start, w_len), 384:512]
        wpos = w_start + lax.broadcasted_iota(jnp.int32, (1, w_len), 1)
        allow_w = ((wpos <= qpos_c) & (wpos > qpos_c - WINDOW))[None]
        p_w = _softmax_rows(_dot_nt(qz, k_w).reshape(NSA_GROUP, tq, w_len), allow_w)
        o_w = _dot(p_w.reshape(NSA_GROUP * tq, w_len).astype(BF16), v_w)
        pieces = []
        for r in range(NSA_GROUP):
            h = NSA_GROUP * g + r
            sl = slice(r * tq, (r + 1) * tq)
            pieces.append(gates[:, 3 * h:3 * h + 1] * o_c[sl] + gates[:, 3 * h + 1:3 * h + 2] * o_s[sl]
                          + gates[:, 3 * h + 2:3 * h + 3] * o_w[sl])
        out_slabs.extend(_place_heads(pieces, g))
    for j in range(4):
        o_ref[:, j * LANE:(j + 1) * LANE] = out_slabs[j].astype(o_ref.dtype)


def _nsa_prompt(qb, small, kcp, vcp, kvb3, *, b, s, kt):
    tq = 128
    nq = s // tq
    nbc = kcp.shape[1]
    nbs = s // SEL_BLOCK
    assert s >= WINDOW + tq and s % kt == 0 and nbc == 2 * nbs
    return pl.pallas_call(
        functools.partial(_nsa_prompt_kernel, kt=kt, nbs=nbs),
        grid=(b, nq),
        in_specs=[pl.BlockSpec((tq, 512), lambda bi, i: (bi * nq + i, 0)),
                  pl.BlockSpec((tq, LANE), lambda bi, i: (bi * nq + i, 0)),
                  pl.BlockSpec((1, nbc, LANE), lambda bi, i: (bi, 0, 0)),
                  pl.BlockSpec((1, nbc, LANE), lambda bi, i: (bi, 0, 0)),
                  pl.BlockSpec((1, s, 512), lambda bi, i: (bi, 0, 0))],
        out_specs=pl.BlockSpec((tq, 512), lambda bi, i: (bi * nq + i, 0)),
        out_shape=jax.ShapeDtypeStruct((b * s, 512), BF16),
        scratch_shapes=[pltpu.VMEM((NSA_GROUP, tq, 1), F32), pltpu.VMEM((NSA_GROUP, tq, 1), F32),
                        pltpu.VMEM((NSA_GROUP, tq, LANE), F32)],
        compiler_params=_params(("arbitrary", "arbitrary")),
        name="nsa_prompt",
    )(qb, small, kcp, vcp, kvb3)


def _nsa_sample_kernel(pt_ref, q_ref, small_ref, rows_ref, wnew_ref, wcache_ref, a_ref, w_ref, pair_ref, *rest,
                       n_pp, kt):
    page_refs = rest[:n_pp]
    o_ref = rest[n_pp]
    ksel, vsel, pooled, tmp = rest[n_pp + 1:]
    pg = pl.program_id(1)
    past = ksel.shape[0]
    nbc = pooled.shape[0]
    nblk = past // SEL_BLOCK
    a = a_ref[...]
    per_page = PAGE_SIZE // CMP_BLOCK
    for k in range(n_pp):
        page = page_refs[k][0]
        x = page[:, 0:2 * KV_W].reshape(per_page, CMP_BLOCK, 2 * KV_W) * a[None]
        tmp[per_page * k:per_page * (k + 1), :] = jnp.sum(x, axis=1)
        r0 = pl.multiple_of((pg * n_pp + k) * PAGE_SIZE, PAGE_SIZE)
        ksel[pl.ds(r0, PAGE_SIZE), :] = page[:, 256:384].astype(BF16)
        vsel[pl.ds(r0, PAGE_SIZE), :] = page[:, 384:512].astype(BF16)
    pooled[pl.ds(pl.multiple_of(pg * (per_page * n_pp), per_page * n_pp), per_page * n_pp), :] = tmp[...]

    @pl.when(pg == pl.num_programs(1) - 1)
    def _finish():
        nq = q_ref.shape[1]
        cmp = _dot(pooled[...].astype(BF16), w_ref[...])
        kc = cmp[:, 0:KV_W].astype(BF16)
        vc = cmp[:, KV_W:2 * KV_W].astype(BF16)
        qf = q_ref[0]
        gates = small_ref[0]
        qg = [_group_queries(qf, g) for g in range(NSA_KV_HEADS)]
        zpad = jnp.zeros((32 - NSA_KV_HEADS * nq, LANE), F32)
        blocks = []
        for r in range(NSA_GROUP):
            blocks += [qg[0][r], qg[1][r], zpad]
        qz = jnp.concatenate(blocks, axis=0).astype(BF16)
        rows_n = qz.shape[0]
        qidx_c = lax.broadcasted_iota(jnp.int32, (rows_n, 1), 0) % nq
        s_c = _dot_nt(qz, kc)
        e_c = jnp.exp(s_c - jnp.max(s_c, axis=-1, keepdims=True))
        p_c = e_c * (1.0 / jnp.sum(e_c, axis=-1, keepdims=True))
        o_c = _dot(p_c.astype(BF16), vc)
        p_sum = p_c[0:32] + p_c[32:64] + p_c[64:96] + p_c[96:128]
        imp = _dot01_r(p_sum, pair_ref[...])
        jcol = lax.broadcasted_iota(jnp.int32, imp.shape, 1)
        score = imp + jnp.where(jcol == 0, FORCE, 0.0)
        sel32 = _topk_select_lanes(score, jcol.astype(F32), min(SEL_TOPN - 1, nblk))
        sel_b = jnp.concatenate([sel32] * NSA_GROUP, axis=0).astype(BF16)

        def sel_body(t, carry):
            m_old, l_old, acc = carry
            k0 = pl.multiple_of(t * kt, kt)
            k_t = ksel[pl.ds(k0, kt), :]
            v_t = vsel[pl.ds(k0, kt), :]
            s = _dot_nt(qz, k_t)
            allowed = _expand_tile(sel_b, t, kt) > 0.5
            sm = jnp.where(allowed, s, NEG)
            m_new = jnp.maximum(m_old, jnp.max(sm, axis=-1, keepdims=True))
            p = jnp.where(allowed, jnp.exp(sm - m_new), 0.0)
            alpha = jnp.exp(m_old - m_new)
            l_new = alpha * l_old + jnp.sum(p, axis=-1, keepdims=True)
            acc = alpha * acc + _dot(p.astype(BF16), v_t)
            return m_new, l_new, acc

        init = (jnp.full((rows_n, 1), NEG, F32), jnp.zeros((rows_n, 1), F32), jnp.zeros((rows_n, LANE), F32))
        m_old, l_old, acc = lax.fori_loop(0, past // kt, sel_body, init)
        newr = rows_ref[0]
        zk = jnp.zeros((LANE - nq, LANE), F32)
        k_n = jnp.concatenate([newr[:, 256:384], zk], axis=0).astype(BF16)
        v_n = jnp.concatenate([newr[:, 384:512], zk], axis=0).astype(BF16)
        ccol = lax.broadcasted_iota(jnp.int32, (1, LANE), 1)
        allow_n = (ccol <= qidx_c) & (ccol < nq)
        sm = jnp.where(allow_n, _dot_nt(qz, k_n), NEG)
        m_new = jnp.maximum(m_old, jnp.max(sm, axis=-1, keepdims=True))
        p = jnp.where(allow_n, jnp.exp(sm - m_new), 0.0)
        alpha = jnp.exp(m_old - m_new)
        l_new = alpha * l_old + jnp.sum(p, axis=-1, keepdims=True)
        acc = alpha * acc + _dot(p.astype(BF16), v_n)
        o_s = acc * (1.0 / jnp.maximum(l_new, 1e-30))
        wc = wcache_ref[0]
        wn = wnew_ref[0]
        n_keep = wc.shape[0]
        k_w = jnp.concatenate([wc[:, 0:KV_W], wn[:, 0:KV_W], zk], axis=0).astype(BF16)
        v_w = jnp.concatenate([wc[:, KV_W:2 * KV_W], wn[:, KV_W:2 * KV_W], zk], axis=0).astype(BF16)
        wcol = lax.broadcasted_iota(jnp.int32, (1, n_keep + LANE), 1)
        allow_w = (((wcol < n_keep) & (wcol + (WINDOW - n_keep) > qidx_c))
                   | ((wcol >= n_keep) & (wcol - n_keep <= qidx_c) & (wcol - n_keep < nq)))
        p_w = _softmax_rows(_dot_nt(qz, k_w), allow_w)
        o_w = _dot(p_w.astype(BF16), v_w)
        slabs = []
        for g in range(NSA_KV_HEADS):
            pieces = []
            for r in range(NSA_GROUP):
                h = NSA_GROUP * g + r
                sl = slice(r * 32 + g * nq, r * 32 + (g + 1) * nq)
                pieces.append(gates[:, 3 * h:3 * h + 1] * o_c[sl] + gates[:, 3 * h + 1:3 * h + 2] * o_s[sl]
                              + gates[:, 3 * h + 2:3 * h + 3] * o_w[sl])
            slabs.extend(_place_heads(pieces, g))
        for j in range(4):
            o_ref[0, :, j * LANE:(j + 1) * LANE] = slabs[j]

    @pl.when(pg != pl.num_programs(1) - 1)
    def _hold():
        o_ref[...] = jnp.zeros(o_ref.shape, F32)


def _topk_select_lanes(v, jcol, n_rounds):
    big = float(v.shape[1] + 1)

    def body(_, carry):
        v, sel = carry
        m = jnp.max(v, axis=1, keepdims=True)
        first = jnp.min(jnp.where(v == m, jcol, big), axis=1, keepdims=True)
        pick = jcol == first
        sel = jnp.where(pick & (m > 0.5 * NEG), 1.0, sel)
        v = jnp.where(pick, -jnp.inf, v)
        return v, sel

    _, sel = lax.fori_loop(0, n_rounds, body, (v, jnp.zeros_like(v)))
    return sel


def _nsa_sample(page_table, q3, small3, rows3, wnew3, wcache3, a_tile, w_bd, pair, cache3, *, kt):
    b, n_pages = page_table.shape
    nq = q3.shape[1]
    past = n_pages * PAGE_SIZE
    n_pp = PAGES_PER_STEP
    assert n_pages % n_pp == 0 and past % kt == 0 and nq * NSA_KV_HEADS <= 32 and nq < CMP_BLOCK
    nbc = past // CMP_BLOCK
    per = lambda w: pl.BlockSpec((1, nq, w), lambda bi, pg, pt: (bi, 0, 0))
    page_specs = [pl.BlockSpec((1, PAGE_SIZE, 512), lambda bi, pg, pt, k=k: (pt[bi, pg * n_pp + k], 0, 0))
                  for k in range(n_pp)]
    const = lambda shape: pl.BlockSpec(shape, lambda bi, pg, pt: (0,) * len(shape))
    grid_spec = pltpu.PrefetchScalarGridSpec(
        num_scalar_prefetch=1,
        grid=(b, n_pages // n_pp),
        in_specs=[per(512), per(LANE), per(512), per(256),
                  pl.BlockSpec((1, wcache3.shape[1], 256), lambda bi, pg, pt: (bi, 0, 0)),
                  const(a_tile.shape), const(w_bd.shape), const(pair.shape)] + page_specs,
        out_specs=per(512),
        scratch_shapes=[pltpu.VMEM((past, LANE), BF16), pltpu.VMEM((past, LANE), BF16),
                        pltpu.VMEM((nbc, 2 * KV_W), F32),
                        pltpu.VMEM((n_pp * PAGE_SIZE // CMP_BLOCK, 2 * KV_W), F32)],
    )
    return pl.pallas_call(
        functools.partial(_nsa_sample_kernel, n_pp=n_pp, kt=kt),
        grid_spec=grid_spec,
        out_shape=jax.ShapeDtypeStruct((b, nq, 512), F32),
        compiler_params=_params(("arbitrary", "arbitrary")),
        name="nsa_sample",
    )(page_table, q3, small3, rows3, wnew3, wcache3, a_tile, w_bd, pair, *([cache3] * n_pp))


def _mlstm_kernel(mqk_ref, mv_ref, mo_ref, small_ref, smallt_ref, hist_ref, ext0_ref, m0_ref, cw_ref, cb_ref,
                  wq_ref, wk_ref, ng_ref, hm_ref, ext_ref, m_ref, ext_scr, m_scr, carry_scr, *, L):
    c = pl.program_id(1)

    @pl.when(c == 0)
    def _init():
        ext_scr[...] = ext0_ref[0]
        m_scr[...] = m0_ref[0]
        carry_scr[...] = hist_ref[0]

    x = mqk_ref[0]
    tl = x.shape[0]
    full = jnp.concatenate([carry_scr[...], x], axis=0)
    y = cb_ref[...]
    for j in range(MLSTM_CONV):
        y = y + cw_ref[j:j + 1, :] * full[8 - (MLSTM_CONV - 1) + j:8 - (MLSTM_CONV - 1) + j + tl]
    carry_scr[...] = full[tl:tl + 8]
    qk = _silu(y).astype(BF16)
    q_all = _dot(qk, wq_ref[...])
    k_all = _dot(qk, wk_ref[...]) * (HEAD_DIM ** -0.5)
    lane = lax.broadcasted_iota(jnp.int32, (L, LANE), 1)
    lane_row = lax.broadcasted_iota(jnp.int32, (1, LANE), 1)
    t_i = lax.broadcasted_iota(jnp.int32, (L, L), 0)
    s_i = lax.broadcasted_iota(jnp.int32, (L, L), 1)
    causal = s_i <= t_i
    tril_b = jnp.where(causal, 1.0, 0.0).astype(BF16)
    triu_b = jnp.where(t_i <= s_i, 1.0, 0.0).astype(BF16)
    eye_b = jnp.where(lax.broadcasted_iota(jnp.int32, (LANE, LANE), 0) == lax.broadcasted_iota(jnp.int32, (LANE, LANE), 1),
                      1.0, 0.0).astype(BF16)
    for u in range(tl // L):
        sl = slice(u * L, (u + 1) * L)
        sm = small_ref[0, sl, :]
        smt = smallt_ref[0, :, sl]
        bcols = _dot01(tril_b, sm)
        brows = _dot01_r(smt, triu_b)
        mrow = m_scr[...]
        for j in range(2):
            cs = slice(j * LANE, (j + 1) * LANE)
            q_slab = q_all[sl, cs]
            k_b = k_all[sl, cs].astype(BF16)
            v_slab = mv_ref[0, sl, cs].astype(F32)
            k_t = _dot_nt(eye_b, k_b).astype(BF16)
            outs = []
            for half in range(2):
                h = 2 * j + half
                keep = (lane >= HEAD_DIM * half) & (lane < HEAD_DIM * (half + 1))
                ones_lane = HEAD_DIM if half == 0 else 0
                qm = jnp.where(keep, q_slab, 0.0).astype(BF16)
                v_ext = jnp.where(keep, v_slab, jnp.where(lane == ones_lane, 1.0, 0.0))
                b_col = bcols[:, 28 + h:29 + h]
                li_col = sm[:, 24 + h:25 + h]
                b_row = brows[4 + h:5 + h, :]
                li_row = smt[h:h + 1, :]
                m_h = mrow[:, h:h + 1]
                d = jnp.where(causal, b_col - b_row + li_row, NEG)
                inter = b_col + m_h
                mt = jnp.maximum(inter, jnp.max(d, axis=-1, keepdims=True))
                a = _dot_nt(qm, k_b) * jnp.exp(d - mt)
                wi = jnp.exp(inter - mt)
                cts = ext_scr[h]
                num = wi * _dot(qm, cts.astype(BF16)) + _dot(a.astype(BF16), v_ext.astype(BF16))
                den = jnp.maximum(jnp.abs(num[:, ones_lane:ones_lane + 1]), jnp.exp(-mt))
                outs.append(num / den)
                bl = b_col[L - 1:L, :]
                g_col = bl - b_col + li_col
                m_new = jnp.maximum(bl + m_h, jnp.max(g_col, axis=0, keepdims=True))
                w_col = jnp.exp(g_col - m_new)
                dec = jnp.exp(bl + m_h - m_new)
                ext_scr[h] = dec * cts + _dot(k_t, (w_col * v_ext).astype(BF16))
                mrow = jnp.where(lane_row == h, m_new, mrow)
            hs = jnp.where(lane < HEAD_DIM, outs[0], outs[1])
            sq = hs * hs
            s0 = jnp.sum(jnp.where(lane < HEAD_DIM, sq, 0.0), axis=-1, keepdims=True)
            s1 = jnp.sum(jnp.where(lane < HEAD_DIM, 0.0, sq), axis=-1, keepdims=True)
            rs = jnp.where(lane < HEAD_DIM, lax.rsqrt(s0 * (1.0 / HEAD_DIM) + EPS), lax.rsqrt(s1 * (1.0 / HEAD_DIM) + EPS))
            hm_ref[0, sl, cs] = (mo_ref[0, sl, cs] * (hs * rs * ng_ref[:, cs])).astype(BF16)
        m_scr[...] = mrow

    @pl.when(c == pl.num_programs(1) - 1)
    def _out():
        ext_ref[0] = ext_scr[...]
        m_ref[0] = m_scr[...]


def _mlstm(mqk3, mv3, mo3, small3, smallt3, hist, ext0, m0, cw, cb, wq, wk, ng, *, tl):
    b, s, _ = mqk3.shape
    L = MLSTM_L
    assert s % tl == 0 and tl % L == 0
    blk = lambda w: pl.BlockSpec((1, tl, w), lambda bi, c: (bi, c, 0))
    st = lambda shape: pl.BlockSpec((1,) + shape, lambda bi, c: (bi,) + (0,) * len(shape))
    const = lambda shape: pl.BlockSpec(shape, lambda bi, c: (0,) * len(shape))
    return pl.pallas_call(
        functools.partial(_mlstm_kernel, L=L),
        grid=(b, s // tl),
        in_specs=[blk(256), blk(256), blk(256), blk(LANE),
                  pl.BlockSpec((1, 8, tl), lambda bi, c: (bi, 0, c)),
                  st((8, 256)), st((MLSTM_HEADS, LANE, LANE)), st((1, LANE)),
                  const((MLSTM_CONV, 256)), const((1, 256)), const((256, 256)), const((256, 256)), const((1, 256))],
        out_specs=[blk(256), st((MLSTM_HEADS, LANE, LANE)), st((1, LANE))],
        out_shape=(jax.ShapeDtypeStruct((b, s, 256), BF16),
                   jax.ShapeDtypeStruct((b, MLSTM_HEADS, LANE, LANE), F32),
                   jax.ShapeDtypeStruct((b, 1, LANE), F32)),
        scratch_shapes=[pltpu.VMEM((MLSTM_HEADS, LANE, LANE), F32), pltpu.VMEM((1, LANE), F32),
                        pltpu.VMEM((8, 256), F32)],
        compiler_params=_params(("arbitrary", "arbitrary")),
        name="mlstm",
    )(mqk3, mv3, mo3, small3, smallt3, hist, ext0, m0, cw, cb, wq, wk, ng)


FF_CHUNK = 1408


def _post_kernel(*refs, final, segmented, tiles_per_batch, seg_len):
    if segmented:
        (x_ref, onsa_ref, hm_ref, ocm_ref, mod_ref, wout_ref, g2_ref, wup_ref, cw_ref, cb_ref, wdn_ref, fg_ref,
         h1_ref, h2_ref, y_ref, fst_ref) = refs
        carry_scr = None
    else:
        (x_ref, onsa_ref, hm_ref, ocm_ref, mod_ref, wout_ref, g2_ref, wup_ref, cw_ref, cb_ref, wdn_ref, fg_ref,
         y_ref, fst_ref, carry_scr) = refs
    x = x_ref[...]
    tm = x.shape[0]
    mix = (_dot(onsa_ref[...].astype(BF16), wout_ref[0:512, :]) + _dot(hm_ref[...], wout_ref[512:768, :])
           + _dot(ocm_ref[...], wout_ref[768:1024, :]))
    mod = mod_ref[0]
    x1 = x + mod[:, 2 * D_MODEL:3 * D_MODEL] * mix
    h2 = x1 * lax.rsqrt(jnp.mean(x1 * x1, axis=-1, keepdims=True) + EPS) * g2_ref[...]
    hb = (h2 * (1.0 + mod[:, 4 * D_MODEL:5 * D_MODEL]) + mod[:, 3 * D_MODEL:4 * D_MODEL]).astype(BF16)
    if not segmented:
        @pl.when(pl.program_id(0) % tiles_per_batch == 0)
        def _reset():
            carry_scr[...] = jnp.zeros(carry_scr.shape, F32)
    else:
        tmod = lax.broadcasted_iota(jnp.int32, (tm, 1), 0) % seg_len
    acc = jnp.zeros((tm, D_MODEL), F32)
    for c0 in range(0, D_FF, FF_CHUNK):
        parts = []
        for part in range(2):
            cs = slice(part * D_FF + c0, part * D_FF + c0 + FF_CHUNK)
            up = _dot(hb, wup_ref[:, cs])
            prev = jnp.zeros((8, FF_CHUNK), F32) if segmented else carry_scr[:, cs]
            full = jnp.concatenate([prev, up], axis=0)
            s1 = full[7:7 + tm]
            s2 = full[6:6 + tm]
            if segmented:
                s1 = jnp.where(tmod >= 1, s1, h1_ref[:, cs])
                s2 = jnp.where(tmod >= 2, s2, h2_ref[:, cs])
                fst_ref[:, cs] = up
            else:
                carry_scr[:, cs] = up[tm - 8:tm]
                fst_ref[0, :, cs] = up[tm - 8:tm]
            parts.append(cb_ref[:, cs] + cw_ref[0:1, cs] * s2 + cw_ref[1:2, cs] * s1 + cw_ref[2:3, cs] * up)
        act = (_silu(parts[0]) * parts[1]).astype(BF16)
        acc = acc + _dot(act, wdn_ref[c0:c0 + FF_CHUNK, :])
    x2 = x1 + mod[:, 5 * D_MODEL:6 * D_MODEL] * acc
    if final:
        x2 = x2 * lax.rsqrt(jnp.mean(x2 * x2, axis=-1, keepdims=True) + EPS) * fg_ref[...]
    y_ref[...] = x2


def _post(x2d, onsa, hm, ocm, mod, wout, g2, wup, cw, cb, wdn, fg, h1=None, h2=None, *, tm, tiles_per_batch, final,
          seg_len=0):
    t_total = x2d.shape[0]
    n_tiles = t_total // tm
    segmented = h1 is not None
    r = mod.shape[1]
    row = lambda w: pl.BlockSpec((tm, w), lambda i: (i, 0))
    in_specs = [row(D_MODEL), row(512), row(256), row(256),
                pl.BlockSpec((1, r, 6 * D_MODEL), lambda i: (i // tiles_per_batch, 0, 0)),
                _const_spec((D_MODEL, D_MODEL)), _const_spec((1, D_MODEL)), _const_spec((D_MODEL, 2 * D_FF)),
                _const_spec((FFN_CONV, 2 * D_FF)), _const_spec((1, 2 * D_FF)), _const_spec((D_FF, D_MODEL)),
                _const_spec((1, D_MODEL))]
    args = [x2d, onsa, hm, ocm, mod, wout, g2, wup, cw, cb, wdn, fg]
    if segmented:
        in_specs += [row(2 * D_FF), row(2 * D_FF)]
        args += [h1, h2]
        out_specs = [row(D_MODEL), row(2 * D_FF)]
        out_shape = (jax.ShapeDtypeStruct((t_total, D_MODEL), F32), jax.ShapeDtypeStruct((t_total, 2 * D_FF), F32))
        scratch = []
    else:
        nb = n_tiles // tiles_per_batch
        out_specs = [row(D_MODEL), pl.BlockSpec((1, 8, 2 * D_FF), lambda i: (i // tiles_per_batch, 0, 0))]
        out_shape = (jax.ShapeDtypeStruct((t_total, D_MODEL), F32), jax.ShapeDtypeStruct((nb, 8, 2 * D_FF), F32))
        scratch = [pltpu.VMEM((8, 2 * D_FF), F32)]
    return pl.pallas_call(
        functools.partial(_post_kernel, final=final, segmented=segmented, tiles_per_batch=tiles_per_batch,
                          seg_len=seg_len),
        grid=(n_tiles,),
        in_specs=in_specs, out_specs=out_specs, out_shape=out_shape, scratch_shapes=scratch,
        compiler_params=_params(("arbitrary",)),
        name="post",
    )(*args)


def _col_perm():
    o = IN_OFF
    cols = list(range(o[0], o[4])) + list(range(o[5], o[8])) + list(range(o[10], o[12]))
    small = list(range(o[4], o[5])) + list(range(o[8], o[10]))
    small += [IN_W] * (LANE - len(small))
    return np.array(cols + small, dtype=np.int32)


def _swap_perm():
    cols = []
    for start, heads in ((0, NSA_HEADS), (512, NSA_KV_HEADS), (768, NSA_KV_HEADS), (1024, NSA_KV_HEADS)):
        for hh in range(heads):
            base = start + HEAD_DIM * hh
            cols += list(range(base + 32, base + 64)) + list(range(base, base + 32))
    return np.array(cols, dtype=np.int32)


def _block_diag(mats):
    n = len(mats)
    rows = []
    for i, m in enumerate(mats):
        rows.append(jnp.concatenate([m if j == i else jnp.zeros((m.shape[0], mats[j].shape[1]), m.dtype)
                                     for j in range(n)], axis=1))
    return jnp.concatenate(rows, axis=0)


def _rope_tables(pos):
    half = HEAD_DIM // 2
    inv = jnp.power(ROPE_THETA, -jnp.arange(half, dtype=F32) / half)
    ang = pos.astype(F32)[:, None] * inv[None, :]
    cos = jnp.cos(ang)
    sin = jnp.sin(ang)
    return jnp.tile(cos, (1, 4)), jnp.concatenate([-sin, sin, -sin, sin], axis=1)


def _layer_weights(l, w):
    win_z = jnp.concatenate([w['w_in'][l], jnp.zeros((D_MODEL, 1), F32)], axis=1)
    bin_z = jnp.concatenate([w['b_in'][l], jnp.zeros((1,), F32)])
    perm, swp = _col_perm(), _swap_perm()
    cmp_a = w['cmp_a'][l]
    cmp_w = w['cmp_w'][l]
    c_ws = jnp.tril(w['c_ws'][l])
    return dict(
        wp=win_z[:, perm].astype(BF16), bp=bin_z[perm][None, :],
        wsw=w['w_in'][l][:, swp].astype(BF16), bsw=w['b_in'][l][swp][None, :],
        g1=w['norm1_g'][l][None, :], g2=w['norm2_g'][l][None, :],
        cg=w['c_norm_g'][l][None, :], c_ws=c_ws, c_bs=w['c_bs'][l],
        a_tile=jnp.concatenate([cmp_a[0].reshape(CMP_BLOCK, KV_W), cmp_a[1].reshape(CMP_BLOCK, KV_W)], axis=1),
        w_bd=_block_diag([cmp_w[0, 0], cmp_w[0, 1], cmp_w[1, 0], cmp_w[1, 1]]).astype(BF16),
        m_cw=w['m_conv_w'][l], m_cb=w['m_conv_b'][l][None, :],
        wq=_block_diag([w['m_wq'][l][h] for h in range(MLSTM_HEADS)]).astype(BF16),
        wk=_block_diag([w['m_wk'][l][h] for h in range(MLSTM_HEADS)]).astype(BF16),
        ng=w['m_norm_g'][l].reshape(1, MLSTM_W),
        wout=w['w_out'][l].astype(BF16), wup=w['w_up'][l].astype(BF16), wdn=w['w_down'][l].astype(BF16),
        f_cw=w['f_conv_w'][l], f_cb=w['f_conv_b'][l][None, :],
    )


def _mlstm_state_in(c0, n0, m0):
    b = c0.shape[0]
    tiles = []
    for h in range(MLSTM_HEADS):
        r0 = HEAD_DIM * (h % 2)
        ol = HEAD_DIM if h % 2 == 0 else 0
        t = jnp.zeros((b, LANE, LANE), F32)
        t = t.at[:, r0:r0 + HEAD_DIM, r0:r0 + HEAD_DIM].set(jnp.swapaxes(c0[:, h], 1, 2))
        t = t.at[:, r0:r0 + HEAD_DIM, ol].set(n0[:, h])
        tiles.append(t)
    m = jnp.zeros((b, 1, LANE), F32).at[:, 0, 0:MLSTM_HEADS].set(m0)
    return jnp.stack(tiles, axis=1), m


def _mlstm_state_out(ext, m):
    cs, ns = [], []
    for h in range(MLSTM_HEADS):
        r0 = HEAD_DIM * (h % 2)
        ol = HEAD_DIM if h % 2 == 0 else 0
        cs.append(jnp.swapaxes(ext[:, h, r0:r0 + HEAD_DIM, r0:r0 + HEAD_DIM], 1, 2))
        ns.append(ext[:, h, r0:r0 + HEAD_DIM, ol])
    return jnp.stack(cs, axis=1), jnp.stack(ns, axis=1), m[:, 0, 0:MLSTM_HEADS]


def _layer_prompt(x2d, mod, lw, fg, *, b, s, final, tm=256, kt=256):
    tiles_per_batch = s // tm
    cos_t, sin_t = _rope_tables(jnp.arange(s, dtype=jnp.int32))
    wc = jnp.concatenate([lw['c_ws'][g] for g in range(CMLP_GROUPS)], axis=1).astype(BF16)
    bc = jnp.repeat(lw['c_bs'].T, HEAD_DIM, axis=1)
    rows, win, qb, kvb, small, mqk, mv, mo, ocm, _ = _inproj(
        x2d, mod, lw['g1'], lw['wp'], lw['bp'], lw['wsw'], lw['bsw'], cos_t, sin_t, wc, bc, lw['cg'],
        tm=tm, tiles_per_batch=tiles_per_batch, tc=CMLP_CHUNK, q_dtype=BF16)
    rows3 = rows.reshape(b, s, 512)
    nb = s // CMP_BLOCK
    comp = _compress(rows3, lw['a_tile'], lw['w_bd'], rows_per_step=min(s, 1024))
    comp = comp.reshape(b, nb // 2, 2, 2 * KV_W).transpose(0, 2, 1, 3).reshape(b, nb, 2 * KV_W)
    onsa = _nsa_prompt(qb, small, comp[..., 0:KV_W], comp[..., KV_W:], kvb.reshape(b, s, 512), b=b, s=s, kt=kt)
    zeros = lambda *sh: jnp.zeros(sh, F32)
    ext0, m0 = _mlstm_state_in(zeros(b, MLSTM_HEADS, HEAD_DIM, HEAD_DIM), zeros(b, MLSTM_HEADS, HEAD_DIM),
                               zeros(b, MLSTM_HEADS))
    small3 = small.reshape(b, s, LANE)
    smallt = jnp.swapaxes(small3[:, :, 24:32], 1, 2)
    hm, ext, mout = _mlstm(mqk.reshape(b, s, 256), mv.reshape(b, s, 256), mo.reshape(b, s, 256), small3, smallt,
                           zeros(b, 8, 256), ext0, m0, lw['m_cw'], lw['m_cb'], lw['wq'], lw['wk'], lw['ng'], tl=tm)
    c1, n1, m1 = _mlstm_state_out(ext, mout)
    y, fst = _post(x2d, onsa, hm.reshape(b * s, 256), ocm, mod, lw['wout'], lw['g2'], lw['wup'], lw['f_cw'],
                   lw['f_cb'], lw['wdn'], fg, tm=tm, tiles_per_batch=tiles_per_batch, final=final)
    n_keep = min(WINDOW, s)
    state = (rows3.reshape(b, s, 4, NSA_KV_HEADS, HEAD_DIM),
             win.reshape(b, s, 2, NSA_KV_HEADS, HEAD_DIM)[:, s - n_keep:],
             c1, n1, m1, mqk.reshape(b, s, 256)[:, s - (MLSTM_CONV - 1):], fst[:, 8 - (FFN_CONV - 1):])
    return y, state


def _layer_sample(x2d, mod, lw, fg, cache3, page_table, win_cache, mc, mn, mm, mconv, fconv, *, b, s, final,
                  kt=512):
    t = b * s
    past = page_table.shape[1] * PAGE_SIZE
    pos = past + jnp.arange(s, dtype=jnp.int32)
    cos1, sin1 = _rope_tables(pos)
    cos_t, sin_t = jnp.tile(cos1, (b, 1)), jnp.tile(sin1, (b, 1))
    eye_b = jnp.eye(b, dtype=F32)
    wc = jnp.concatenate([jnp.kron(eye_b, lw['c_ws'][g][:s, :s]) for g in range(CMLP_GROUPS)], axis=1).astype(BF16)
    bc = jnp.tile(jnp.repeat(lw['c_bs'].T[:s], HEAD_DIM, axis=1), (b, 1))
    rows, win, qf, _, small, mqk, mv, mo, ocm, cvn = _inproj(
        x2d, mod, lw['g1'], lw['wp'], lw['bp'], lw['wsw'], lw['bsw'], cos_t, sin_t, wc, bc, lw['cg'],
        tm=t, tiles_per_batch=1, tc=t, q_dtype=F32)
    nbc = past // CMP_BLOCK
    pair = jnp.repeat(jnp.eye(nbc // 2, dtype=BF16), 2, axis=0)
    n_keep = win_cache.shape[1]
    wcache3 = win_cache.reshape(b, n_keep, 256)
    wnew3 = win.reshape(b, s, 256)
    onsa = _nsa_sample(page_table, qf.reshape(b, s, 512), small.reshape(b, s, LANE), rows.reshape(b, s, 512), wnew3,
                       wcache3, lw['a_tile'], lw['w_bd'], pair, cache3, kt=kt).reshape(t, 512)
    L = MLSTM_L
    pad3 = lambda a, fill=0.0: jnp.pad(a.reshape(b, s, -1), ((0, 0), (0, L - s), (0, 0)), constant_values=fill)
    small3 = small.reshape(b, s, LANE)
    lane = jnp.arange(LANE)
    small_p = jnp.where((lane >= 24) & (lane < 28), pad3(small3, NEG), pad3(small3))
    smallt = jnp.swapaxes(small_p[:, :, 24:32], 1, 2)
    hist = jnp.pad(mconv, ((0, 0), (8 - (MLSTM_CONV - 1), 0), (0, 0)))
    ext0, m0 = _mlstm_state_in(mc, mn, mm)
    hm, ext, mout = _mlstm(pad3(mqk), pad3(mv), pad3(mo), small_p, smallt, hist, ext0, m0, lw['m_cw'], lw['m_cb'],
                           lw['wq'], lw['wk'], lw['ng'], tl=L)
    c1, n1, m1 = _mlstm_state_out(ext, mout)
    hm = hm[:, :s].reshape(t, 256)
    h1 = jnp.zeros((b, s, 2 * D_FF), F32).at[:, 0].set(fconv[:, 1]).reshape(t, 2 * D_FF)
    h2 = jnp.zeros((b, s, 2 * D_FF), F32).at[:, 0].set(fconv[:, 0]).at[:, 1].set(fconv[:, 1]).reshape(t, 2 * D_FF)
    y, up = _post(x2d, onsa, hm, ocm, mod, lw['wout'], lw['g2'], lw['wup'], lw['f_cw'], lw['f_cb'], lw['wdn'], fg,
                  h1, h2, tm=t, tiles_per_batch=1, final=final, seg_len=s)
    new_win = jnp.concatenate([wcache3, wnew3], axis=1)[:, s:].reshape(b, n_keep, 2, NSA_KV_HEADS, HEAD_DIM)
    state = (rows.reshape(b, s, 4, NSA_KV_HEADS, HEAD_DIM), new_win, c1, n1, m1,
             mqk.reshape(b, s, 256)[:, s - (MLSTM_CONV - 1):], cvn.reshape(b, s, CMLP_W),
             up.reshape(b, s, 2 * D_FF)[:, s - (FFN_CONV - 1):])
    return y, state


def kernel(x_prompt, x_sample, cache_nsa_kv, page_table, cache_win_kv, state_mlstm_C, state_mlstm_n, state_mlstm_m, state_mlstm_conv, state_ffn_conv, c_prompt, c_sample, ada_w, ada_b, norm1_g, norm2_g, w_in, b_in, cmp_a, cmp_w, m_conv_w, m_conv_b, m_wq, m_wk, m_norm_g, c_norm_g, c_ws, c_bs, w_out, w_up, f_conv_w, f_conv_b, w_down, final_norm_g):
    bp, sp, _ = x_prompt.shape
    bs, ss, _ = x_sample.shape
    depth = ada_w.shape[0]
    w = dict(w_in=w_in, b_in=b_in, norm1_g=norm1_g, norm2_g=norm2_g, cmp_a=cmp_a, cmp_w=cmp_w, m_conv_w=m_conv_w,
             m_conv_b=m_conv_b, m_wq=m_wq, m_wk=m_wk, m_norm_g=m_norm_g, c_norm_g=c_norm_g, c_ws=c_ws, c_bs=c_bs,
             w_out=w_out, w_up=w_up, f_conv_w=f_conv_w, f_conv_b=f_conv_b, w_down=w_down)
    n_c = bp + bs
    n_pad = -(-n_c // 8) * 8
    c_all = jnp.pad(jnp.concatenate([c_prompt, c_sample], axis=0), ((0, n_pad - n_c), (0, 0)))
    mod_all = _ada(c_all, ada_w, ada_b)
    fg = final_norm_g[None, :]
    xp = x_prompt.reshape(bp * sp, D_MODEL)
    xs = x_sample.reshape(bs * ss, D_MODEL)
    n_pool = cache_nsa_kv.shape[1]
    st_p, st_s = [], []
    for l in range(depth):
        lw = _layer_weights(l, w)
        final = l == depth - 1
        mod_p = mod_all[l, 0:bp][:, None, :]
        mod_s = jnp.repeat(mod_all[l, bp:bp + bs], ss, axis=0)[None]
        xp, sp_l = _layer_prompt(xp, mod_p, lw, fg, b=bp, s=sp, final=final)
        xs, ss_l = _layer_sample(xs, mod_s, lw, fg, cache_nsa_kv[l].reshape(n_pool, PAGE_SIZE, 512), page_table,
                                 cache_win_kv[l], state_mlstm_C[l], state_mlstm_n[l], state_mlstm_m[l],
                                 state_mlstm_conv[l], state_ffn_conv[l], b=bs, s=ss, final=final)
        st_p.append(sp_l)
        st_s.append(ss_l)
    sp_st = [jnp.stack(t) for t in zip(*st_p)]
    ss_st = [jnp.stack(t) for t in zip(*st_s)]
    y_prompt = xp.reshape(bp, sp, D_MODEL)
    y_sample = xs.reshape(bs, ss, D_MODEL)
    return (y_prompt, y_sample, sp_st[0], ss_st[0], sp_st[1], ss_st[1],
            sp_st[2], sp_st[3], sp_st[4], sp_st[5],
            ss_st[2], ss_st[3], ss_st[4], ss_st[5],
            ss_st[6], sp_st[6], ss_st[7])
```

```python
import functools

import numpy as np
import jax
import jax.numpy as jnp
from jax import lax
from jax.experimental import pallas as pl
from jax.experimental.pallas import tpu as pltpu

F32 = jnp.float32
BF16 = jnp.bfloat16

D_MODEL = 1024
HEAD_DIM = 64
NSA_HEADS = 8
NSA_KV_HEADS = 2
NSA_GROUP = 4
CMP_BLOCK = 32
SEL_BLOCK = 64
SEL_TOPN = 16
WINDOW = 512
ROPE_THETA = 10000.0
MLSTM_HEADS = 4
MLSTM_CONV = 4
CMLP_GROUPS = 4
CMLP_CHUNK = 128
D_FF = 2816
FFN_CONV = 3
PAGE_SIZE = 128
NSA_W = 512
KV_W = 128
MLSTM_W = 256
CMLP_W = 256
IN_SIZES = (NSA_W, 2 * KV_W, 2 * KV_W, 2 * KV_W, 3 * NSA_HEADS, MLSTM_W, MLSTM_W, MLSTM_W, MLSTM_HEADS,
            MLSTM_HEADS, CMLP_W, CMLP_W)
IN_OFF = tuple(int(v) for v in np.cumsum((0,) + IN_SIZES))
IN_W = IN_OFF[-1]
EPS = 1e-6
NEG = -1e30
FORCE = 1e4
LANE = 128
NP_COLS = 2688
NSW_COLS = 896
VMEM_LIMIT = 56 * 1024 * 1024
PAGES_PER_STEP = 32
SEL_KT = 256
Q_SCALE = HEAD_DIM ** -0.5 * 1.4426950408889634
MLSTM_L = 128


def _dot(a, b):
    return jnp.dot(a, b, preferred_element_type=F32)


def _dot_nt(a, b):
    return lax.dot_general(a, b, (((1,), (1,)), ((), ())), preferred_element_type=F32)


def _dot01(m01, x):
    hi = x.astype(BF16)
    r1 = x - hi.astype(F32)
    mid = r1.astype(BF16)
    lo = (r1 - mid.astype(F32)).astype(BF16)
    return _dot(m01, hi) + _dot(m01, mid) + _dot(m01, lo)


def _dot01_r(x, m01):
    hi = x.astype(BF16)
    r1 = x - hi.astype(F32)
    mid = r1.astype(BF16)
    lo = (r1 - mid.astype(F32)).astype(BF16)
    return _dot(hi, m01) + _dot(mid, m01) + _dot(lo, m01)


def _sigmoid(x):
    return 1.0 / (1.0 + jnp.exp(-x))


def _silu(x):
    return x * _sigmoid(x)


def _gelu_tanh(x):
    return 0.5 * x * (1.0 + jnp.tanh(0.7978845608028654 * (x + 0.044715 * (x * x * x))))


def _log_sigmoid(x):
    return jnp.minimum(x, 0.0) - jnp.log(1.0 + jnp.exp(-jnp.abs(x)))


def _const_spec(shape):
    nd = len(shape)
    return pl.BlockSpec(shape, lambda *_: (0,) * nd)


def _params(sem):
    return pltpu.CompilerParams(dimension_semantics=sem, vmem_limit_bytes=VMEM_LIMIT)


def _ada_kernel(c_ref, w_ref, b_ref, o_ref):
    c = c_ref[...]
    o_ref[0] = _dot(_silu(c).astype(BF16), w_ref[0]) + b_ref[0]


def _ada(c_all, ada_w, ada_b):
    depth = ada_w.shape[0]
    n = c_all.shape[0]
    return pl.pallas_call(
        _ada_kernel,
        grid=(depth, 6),
        in_specs=[pl.BlockSpec((n, D_MODEL), lambda l, j: (0, 0)),
                  pl.BlockSpec((1, D_MODEL, D_MODEL), lambda l, j: (l, 0, j)),
                  pl.BlockSpec((1, 1, D_MODEL), lambda l, j: (l, 0, j))],
        out_specs=pl.BlockSpec((1, n, D_MODEL), lambda l, j: (l, 0, j)),
        out_shape=jax.ShapeDtypeStruct((depth, n, 6 * D_MODEL), F32),
        compiler_params=_params(("arbitrary", "arbitrary")),
        name="ada",
    )(c_all, ada_w.astype(BF16), ada_b.reshape(depth, 1, 6 * D_MODEL))


def _inproj_kernel(x_ref, mod_ref, g_ref, w_ref, b_ref, wsw_ref, bsw_ref, cos_ref, sin_ref, wc_ref, bc_ref, cg_ref,
                   rows_ref, win_ref, q_ref, kvb_ref, small_ref, mqk_ref, mv_ref, mo_ref, ocm_ref, cvn_ref, *, tc,
                   tiles_per_batch):
    x = x_ref[...]
    tm = x.shape[0]
    xn = x * lax.rsqrt(jnp.mean(x * x, axis=-1, keepdims=True) + EPS) * g_ref[...]
    mod = mod_ref[0]
    h = xn * (1.0 + mod[:, D_MODEL:2 * D_MODEL]) + mod[:, 0:D_MODEL]
    hb = h.astype(BF16)
    z = _dot(hb, w_ref[...]) + b_ref[...]
    zs = _dot(hb, wsw_ref[...]) + bsw_ref[...]
    cos = cos_ref[...]
    sin = sin_ref[...]

    def rope(c0, s0):
        return z[:, c0:c0 + LANE] * cos + zs[:, s0:s0 + LANE] * sin

    for j in range(4):
        q_ref[:, j * LANE:(j + 1) * LANE] = (rope(j * LANE, j * LANE) * Q_SCALE).astype(q_ref.dtype)
    kc = rope(512, 512)
    vc = z[:, 640:768]
    ks = rope(768, 640)
    vs = z[:, 896:1024]
    kw = rope(1024, 768)
    vw = z[:, 1152:1280]
    rows_ref[:, 0:128] = kc
    rows_ref[:, 128:256] = vc
    rows_ref[:, 256:384] = ks
    rows_ref[:, 384:512] = vs
    win_ref[:, 0:128] = kw
    win_ref[:, 128:256] = vw
    lane_k = lax.broadcasted_iota(jnp.int32, (tm, LANE), 1)
    row0 = (pl.program_id(0) % tiles_per_batch) * tm
    blk = ((row0 + lax.broadcasted_iota(jnp.int32, (tm, LANE), 0)) // SEL_BLOCK) % HEAD_DIM
    low = lane_k < HEAD_DIM
    kvb_ref[:, 0:128] = jnp.where(low, ks, jnp.where(lane_k == HEAD_DIM + blk, 1.0, 0.0)).astype(BF16)
    kvb_ref[:, 128:256] = jnp.where(low, jnp.where(lane_k == blk, 1.0, 0.0), ks).astype(BF16)
    kvb_ref[:, 256:384] = jnp.where(low, vs, 1.0).astype(BF16)
    kvb_ref[:, 384:512] = jnp.where(low, 1.0, vs).astype(BF16)
    kvb_ref[:, 512:640] = kw.astype(BF16)
    kvb_ref[:, 640:768] = vw.astype(BF16)
    sm = z[:, 2560:2688]
    lane = lax.broadcasted_iota(jnp.int32, sm.shape, 1)
    small_ref[...] = jnp.where(lane < 24, _sigmoid(sm), jnp.where(lane < 28, sm, _log_sigmoid(sm)))
    mqk_ref[...] = z[:, 1280:1536]
    mv_ref[...] = z[:, 1536:1792].astype(BF16)
    mo_ref[...] = _sigmoid(z[:, 1792:2048])
    cu = _gelu_tanh(z[:, 2048:2304])
    cv = _gelu_tanh(z[:, 2304:2560])
    cvn = cv * lax.rsqrt(jnp.mean(cv * cv, axis=-1, keepdims=True) + EPS) * cg_ref[...]
    cvn_ref[...] = cvn
    lane_c = lax.broadcasted_iota(jnp.int32, (tc, CMLP_W), 1) // HEAD_DIM
    for c in range(tm // tc):
        v = cvn[c * tc:(c + 1) * tc]
        vbd = jnp.concatenate([jnp.where(lane_c == g, v, 0.0) for g in range(CMLP_GROUPS)], axis=0).astype(BF16)
        s = _dot(wc_ref[...], vbd) + bc_ref[...]
        ocm_ref[c * tc:(c + 1) * tc, :] = (cu[c * tc:(c + 1) * tc] * s).astype(BF16)


def _inproj(x2d, mod, g1, wp, bp, wsw, bsw, cos_t, sin_t, wc, bc, cg, *, tm, tiles_per_batch, tc, q_dtype):
    t_total = x2d.shape[0]
    n_tiles = t_total // tm
    r = mod.shape[1]
    row = lambda w: pl.BlockSpec((tm, w), lambda i: (i, 0))
    out_shapes = (
        jax.ShapeDtypeStruct((t_total, 512), F32),
        jax.ShapeDtypeStruct((t_total, 256), F32),
        jax.ShapeDtypeStruct((t_total, 512), q_dtype),
        jax.ShapeDtypeStruct((t_total, 768), BF16),
        jax.ShapeDtypeStruct((t_total, 128), F32),
        jax.ShapeDtypeStruct((t_total, 256), F32),
        jax.ShapeDtypeStruct((t_total, 256), BF16),
        jax.ShapeDtypeStruct((t_total, 256), F32),
        jax.ShapeDtypeStruct((t_total, 256), BF16),
        jax.ShapeDtypeStruct((t_total, 256), F32),
    )
    return pl.pallas_call(
        functools.partial(_inproj_kernel, tc=tc, tiles_per_batch=tiles_per_batch),
        grid=(n_tiles,),
        in_specs=[row(D_MODEL),
                  pl.BlockSpec((1, r, 6 * D_MODEL), lambda i: (i // tiles_per_batch, 0, 0)),
                  _const_spec((1, D_MODEL)),
                  _const_spec((D_MODEL, NP_COLS)), _const_spec((1, NP_COLS)),
                  _const_spec((D_MODEL, NSW_COLS)), _const_spec((1, NSW_COLS)),
                  pl.BlockSpec((tm, LANE), lambda i: (i % tiles_per_batch, 0)),
                  pl.BlockSpec((tm, LANE), lambda i: (i % tiles_per_batch, 0)),
                  _const_spec(wc.shape), _const_spec(bc.shape), _const_spec((1, CMLP_W))],
        out_specs=[row(512), row(256), row(512), row(768), row(128), row(256), row(256), row(256), row(256), row(256)],
        out_shape=out_shapes,
        compiler_params=_params(("arbitrary",)),
        name="inproj",
    )(x2d, mod, g1, wp, bp, wsw, bsw, cos_t, sin_t, wc, bc, cg)


def _compress_kernel(rows_ref, a_ref, w_ref, o_ref):
    x = rows_ref[0]
    n = x.shape[0] // CMP_BLOCK
    pooled = jnp.sum(x.reshape(n, CMP_BLOCK, 2 * KV_W) * a_ref[...][None], axis=1)
    o_ref[0] = _dot(pooled.astype(BF16), w_ref[...])


def _compress(rows3, a_tile, w_bd, *, rows_per_step):
    b, s, _ = rows3.shape
    nb = s // CMP_BLOCK
    steps = s // rows_per_step
    return pl.pallas_call(
        _compress_kernel,
        grid=(b, steps),
        in_specs=[pl.BlockSpec((1, rows_per_step, 2 * KV_W), lambda i, j: (i, j, 0)),
                  _const_spec((CMP_BLOCK, 2 * KV_W)), _const_spec((2 * KV_W, 2 * KV_W))],
        out_specs=pl.BlockSpec((1, rows_per_step // CMP_BLOCK, 2 * KV_W), lambda i, j: (i, j, 0)),
        out_shape=jax.ShapeDtypeStruct((b, nb, 2 * KV_W), F32),
        compiler_params=_params(("arbitrary", "arbitrary")),
        name="compress",
    )(rows3, a_tile, w_bd)


def _group_queries(qf, g):
    lane = lax.broadcasted_iota(jnp.int32, (qf.shape[0], LANE), 1)
    keep = (lane >= HEAD_DIM * g) & (lane < HEAD_DIM * (g + 1))
    out = []
    for r in range(NSA_GROUP):
        h = NSA_GROUP * g + r
        slab = qf[:, (h // 2) * LANE:(h // 2 + 1) * LANE]
        if h % 2 != g:
            slab = pltpu.roll(slab, HEAD_DIM, axis=1)
        out.append(jnp.where(keep, slab, 0.0))
    return out


def _place_heads(pieces, g):
    lane = lax.broadcasted_iota(jnp.int32, pieces[0].shape, 1)
    slabs = []
    for j in range(2):
        halves = []
        for half in range(2):
            p = pieces[2 * j + half]
            if half != g:
                p = pltpu.roll(p, HEAD_DIM, axis=1)
            halves.append(p)
        slabs.append(jnp.where(lane < HEAD_DIM, halves[0], halves[1]))
    return slabs


def _softmax_rows(s, allowed):
    sm = jnp.where(allowed, s, NEG)
    m = jnp.max(sm, axis=-1, keepdims=True)
    e = jnp.where(allowed, jnp.exp2(sm - m), 0.0)
    l = jnp.sum(e, axis=-1, keepdims=True)
    return e * (1.0 / jnp.maximum(l, 1e-30))


def _topk_select(v, jrow, n_rounds):
    big = float(v.shape[0] + 1)

    def body(_, carry):
        v, sel = carry
        m = jnp.max(v, axis=0, keepdims=True)
        first = jnp.min(jnp.where(v == m, jrow, big), axis=0, keepdims=True)
        pick = jrow == first
        sel = jnp.where(pick & (m > 0.5 * NEG), 1.0, sel)
        v = jnp.where(pick, -jnp.inf, v)
        return v, sel

    _, sel = lax.fori_loop(0, n_rounds, body, (v, jnp.zeros_like(v)))
    return sel


def _expand_tile(sel_b, t, kt):
    nblk = sel_b.shape[1]
    jrow = lax.broadcasted_iota(jnp.int32, (nblk, kt), 0) - t * (kt // SEL_BLOCK)
    cdiv = lax.broadcasted_iota(jnp.int32, (nblk, kt), 1) // SEL_BLOCK
    e = jnp.where(jrow == cdiv, 1.0, 0.0).astype(BF16)
    return _dot(sel_b, e)


def _nsa_prompt_kernel(q_ref, small_ref, kc_ref, vc_ref, kvb_ref, o_ref, m_scr, acc_scr, lhs_scr, s_scr, *, kt, nbs):
    i = pl.program_id(1)
    tq = q_ref.shape[0]
    nbc = kc_ref.shape[1]
    s_len = kvb_ref.shape[1]
    qf = q_ref[...].astype(F32)
    gates = small_ref[...]
    q0 = i * tq
    qpos_c = q0 + lax.broadcasted_iota(jnp.int32, (tq, 1), 0)
    qpos_r = q0 + lax.broadcasted_iota(jnp.int32, (1, tq), 1)
    eye = jnp.where(lax.broadcasted_iota(jnp.int32, (tq, tq), 0) == lax.broadcasted_iota(jnp.int32, (tq, tq), 1),
                    1.0, 0.0).astype(BF16)
    qpos_r4 = q0 + lax.broadcasted_iota(jnp.int32, (1, NSA_GROUP * tq), 1) % tq
    kc = kc_ref[0].astype(BF16)
    vc = vc_ref[0].astype(BF16)
    half = nbc // 2
    rho_c = lax.broadcasted_iota(jnp.int32, (nbc, 1), 0)
    end_c = jnp.where(rho_c < half, 2 * rho_c, 2 * (rho_c - half) + 1) * CMP_BLOCK + (CMP_BLOCK - 1)
    rho_r = lax.broadcasted_iota(jnp.int32, (1, nbc), 1)
    end_r = jnp.where(rho_r < half, 2 * rho_r, 2 * (rho_r - half) + 1) * CMP_BLOCK + (CMP_BLOCK - 1)
    w_start = pl.multiple_of(jnp.maximum(q0 - WINDOW, 0), LANE)
    w_len = WINDOW + tq
    wpos = w_start + lax.broadcasted_iota(jnp.int32, (1, w_len), 1)
    bias_w = jnp.where((wpos <= qpos_c) & (wpos > qpos_c - WINDOW), 0.0, NEG)
    bias_c = jnp.where(end_r <= qpos_c, 0.0, NEG)
    any_c = qpos_c >= CMP_BLOCK - 1
    k_w = kvb_ref[0, pl.ds(w_start, w_len), 512:640]
    v_w = kvb_ref[0, pl.ds(w_start, w_len), 640:768]
    out_slabs = []

    def attend(q_b, k, v_ones, bias):
        s = _dot_nt(q_b, k) + bias
        p = jnp.exp2(s - jnp.max(s, axis=-1, keepdims=True))
        o = _dot(p.astype(BF16), v_ones)
        return o * (1.0 / pltpu.roll(o, HEAD_DIM, axis=1))

    for g in range(NSA_KV_HEADS):
        qz4 = _group_queries(qf, g)
        qz_b = [x.astype(BF16) for x in qz4]
        qz = jnp.concatenate(qz_b, axis=0)
        lane_g = lax.broadcasted_iota(jnp.int32, (1, LANE), 1)
        keep_g = (lane_g >= HEAD_DIM * g) & (lane_g < HEAD_DIM * (g + 1))
        one_b = jnp.ones((), BF16)
        vc_x = jnp.where(keep_g, vc, one_b)
        o_c = [jnp.where(any_c, attend(qz_b[r], kc, vc_x, bias_c), 0.0) for r in range(NSA_GROUP)]
        s_t = _dot_nt(kc, qz)
        sm = s_t + jnp.where(end_c <= qpos_r4, 0.0, NEG)
        et = jnp.exp2(sm - jnp.max(sm, axis=0, keepdims=True))
        pt = et * (1.0 / jnp.sum(et, axis=0, keepdims=True))
        ph = pt[:, 0:tq] + pt[:, tq:2 * tq] + pt[:, 2 * tq:3 * tq] + pt[:, 3 * tq:4 * tq]
        imp = jnp.where(qpos_r >= CMP_BLOCK - 1, ph[0:half] + ph[half:nbc], 0.0)
        if nbs > half:
            imp = jnp.concatenate([imp, jnp.zeros((nbs - half, tq), F32)], axis=0)
        jrow_i = lax.broadcasted_iota(jnp.int32, (nbs, tq), 0)
        forced = (jrow_i == qpos_r // SEL_BLOCK) | (jrow_i == 0)
        valid = jrow_i * SEL_BLOCK <= qpos_r
        score = jnp.where(valid, imp + jnp.where(forced, FORCE, 0.0), NEG)
        sel_t = _topk_select(score, jrow_i.astype(F32), min(SEL_TOPN, nbs))
        m_scr[...] = jnp.full(m_scr.shape, NEG, F32)
        acc_scr[...] = jnp.zeros(acc_scr.shape, F32)
        sel_q = _dot_nt(eye, sel_t.astype(BF16))
        if nbs < LANE:
            sel_q = jnp.concatenate([sel_q, jnp.zeros((tq, LANE - nbs), F32)], axis=1)
        negsel = (sel_q - 1.0) * 1e30
        base = HEAD_DIM * (1 - g)
        lane_q = lax.broadcasted_iota(jnp.int32, (tq, LANE), 1)
        feat_mask = (lane_q >= base) & (lane_q < base + HEAD_DIM)
        for hf in range(max(1, nbs // HEAD_DIM)):
            feat = (negsel if HEAD_DIM * hf == base else pltpu.roll(negsel, HEAD_DIM, axis=1)).astype(BF16)
            for r in range(NSA_GROUP):
                lhs_scr[hf, r] = jnp.where(feat_mask, feat, qz_b[r])
        tiles_per_half = HEAD_DIM * SEL_BLOCK // kt

        def scores(t, slot):
            k_t = kvb_ref[0, pl.ds(pl.multiple_of(t * kt, kt), kt), g * LANE:(g + 1) * LANE]
            hf = t // tiles_per_half
            for r in range(NSA_GROUP):
                s_scr[slot, r] = _dot_nt(lhs_scr[hf, r], k_t)

        def consume(t, slot, causal):
            k0 = pl.multiple_of(t * kt, kt)
            v_t = kvb_ref[0, pl.ds(k0, kt), (2 + g) * LANE:(3 + g) * LANE]
            if causal:
                kpos = k0 + lax.broadcasted_iota(jnp.int32, (1, kt), 1)
                bias = jnp.where(kpos <= qpos_c, 0.0, NEG)
            for r in range(NSA_GROUP):
                s = s_scr[slot, r]
                if causal:
                    s = s + bias
                m_old = m_scr[r]
                m_new = jnp.maximum(m_old, jnp.max(s, axis=-1, keepdims=True))
                p = jnp.exp2(s - jnp.concatenate([m_new] * (kt // LANE), axis=1))
                acc_scr[r] = jnp.exp2(m_old - m_new) * acc_scr[r] + _dot(p.astype(BF16), v_t)
                m_scr[r] = m_new

        n_full = q0 // kt
        scores(0, 0)

        def pair_body(tp, _):
            scores(2 * tp + 1, 1)
            consume(2 * tp, 0, False)
            scores(2 * tp + 2, 0)
            consume(2 * tp + 1, 1, False)
            return 0

        lax.fori_loop(0, n_full // 2, pair_body, 0)

        @pl.when(n_full % 2 == 1)
        def _odd():
            scores(n_full, 1)
            consume(n_full - 1, 0, False)
            consume(n_full, 1, True)

        @pl.when(n_full % 2 == 0)
        def _even():
            consume(n_full, 0, True)
        acc = acc_scr[...].reshape(NSA_GROUP * tq, LANE)
        o_s = acc * (1.0 / jnp.maximum(pltpu.roll(acc, HEAD_DIM, axis=1), 1e-30))
        v_wx = jnp.where(keep_g, v_w, one_b)
        pieces = []
        for r in range(NSA_GROUP):
            h = NSA_GROUP * g + r
            sl = slice(r * tq, (r + 1) * tq)
            o_w = attend(qz_b[r], k_w, v_wx, bias_w)
            pieces.append(gates[:, 3 * h:3 * h + 1] * o_c[r] + gates[:, 3 * h + 1:3 * h + 2] * o_s[sl]
                          + gates[:, 3 * h + 2:3 * h + 3] * o_w)
        out_slabs.extend(_place_heads(pieces, g))
    for j in range(4):
        o_ref[:, j * LANE:(j + 1) * LANE] = out_slabs[j].astype(o_ref.dtype)


def _nsa_prompt(qb, small, kcp, vcp, kvb3, *, b, s, kt):
    tq = 128
    nq = s // tq
    nbc = kcp.shape[1]
    nbs = s // SEL_BLOCK
    assert s >= WINDOW + tq and s % kt == 0 and nbc == 2 * nbs and nbs <= LANE and kt == SEL_KT
    return pl.pallas_call(
        functools.partial(_nsa_prompt_kernel, kt=kt, nbs=nbs),
        grid=(b, nq),
        in_specs=[pl.BlockSpec((tq, 512), lambda bi, i: (bi * nq + i, 0)),
                  pl.BlockSpec((tq, LANE), lambda bi, i: (bi * nq + i, 0)),
                  pl.BlockSpec((1, nbc, LANE), lambda bi, i: (bi, 0, 0)),
                  pl.BlockSpec((1, nbc, LANE), lambda bi, i: (bi, 0, 0)),
                  pl.BlockSpec((1, s, 768), lambda bi, i: (bi, 0, 0))],
        out_specs=pl.BlockSpec((tq, 512), lambda bi, i: (bi * nq + i, 0)),
        out_shape=jax.ShapeDtypeStruct((b * s, 512), BF16),
        scratch_shapes=[pltpu.VMEM((NSA_GROUP, tq, LANE), F32), pltpu.VMEM((NSA_GROUP, tq, LANE), F32),
                        pltpu.VMEM((max(1, nbs // HEAD_DIM), NSA_GROUP, tq, LANE), BF16),
                        pltpu.VMEM((2, NSA_GROUP, tq, kt), F32)],
        compiler_params=_params(("arbitrary", "arbitrary")),
        name="nsa_prompt",
    )(qb, small, kcp, vcp, kvb3)


def _nsa_sample_kernel(pt_ref, q_ref, small_ref, rows_ref, wnew_ref, wcache_ref, a_ref, w_ref, pair_ref, *rest,
                       n_pp, kt):
    page_refs = rest[:n_pp]
    o_ref = rest[n_pp]
    ksel, vsel, pooled = rest[n_pp + 1:]
    pg = pl.program_id(1)
    npg = pooled.shape[0]
    past = ksel.shape[1]
    nblk = past // SEL_BLOCK
    a_t = a_ref[...]
    per_page = PAGE_SIZE // CMP_BLOCK
    t_i = lax.broadcasted_iota(jnp.int32, (PAGE_SIZE, per_page * n_pp), 0) // CMP_BLOCK
    n_i = lax.broadcasted_iota(jnp.int32, (PAGE_SIZE, per_page * n_pp), 1)
    acc_p = jnp.zeros((2 * KV_W, per_page * n_pp), F32)
    for k in range(n_pp):
        page = page_refs[k][0, 0]
        xa = page[0:2 * KV_W, :] * a_t
        seg = jnp.where(n_i == t_i + per_page * k, 1.0, 0.0).astype(BF16)
        hi = xa.astype(BF16)
        lo = (xa - hi.astype(F32)).astype(BF16)
        acc_p = acc_p + _dot(hi, seg) + _dot(lo, seg)
        c0 = pl.multiple_of((pg * n_pp + k) * PAGE_SIZE, PAGE_SIZE)
        ksel[:, pl.ds(c0, PAGE_SIZE)] = page[256:384, :].astype(BF16)
        vsel[:, pl.ds(c0, PAGE_SIZE)] = page[384:512, :].astype(BF16)
    pooled[pg] = acc_p

    @pl.when(pg == pl.num_programs(1) - 1)
    def _finish():
        nq = q_ref.shape[1]
        cmp_t = [_dot(w_ref[...], pooled[j].astype(BF16)) for j in range(npg)]
        kc = jnp.concatenate([c[0:KV_W] for c in cmp_t], axis=1).astype(BF16)
        vc = jnp.concatenate([c[KV_W:2 * KV_W] for c in cmp_t], axis=1).astype(BF16)
        qf = q_ref[0]
        gates = small_ref[0]
        qg = [_group_queries(qf, g) for g in range(NSA_KV_HEADS)]
        zpad = jnp.zeros((32 - NSA_KV_HEADS * nq, LANE), F32)
        blocks = []
        for r in range(NSA_GROUP):
            blocks += [qg[0][r], qg[1][r], zpad]
        qz = jnp.concatenate(blocks, axis=0).astype(BF16)
        rows_n = qz.shape[0]
        qidx_c = lax.broadcasted_iota(jnp.int32, (rows_n, 1), 0) % nq
        s_c = _dot(qz, kc)
        e_c = jnp.exp2(s_c - jnp.max(s_c, axis=-1, keepdims=True))
        p_c = e_c * (1.0 / jnp.sum(e_c, axis=-1, keepdims=True))
        o_c = _dot_nt(p_c.astype(BF16), vc)
        p_sum = p_c[0:32] + p_c[32:64] + p_c[64:96] + p_c[96:128]
        imp = _dot01_r(p_sum, pair_ref[...])
        jcol = lax.broadcasted_iota(jnp.int32, imp.shape, 1)
        score = imp + jnp.where(jcol == 0, FORCE, 0.0)
        sel32 = _topk_select_lanes(score, jcol.astype(F32), min(SEL_TOPN - 1, nblk))
        sel_b = jnp.concatenate([sel32] * NSA_GROUP, axis=0).astype(BF16)

        def sel_body(t, carry):
            m_old, l_old, acc = carry
            k0 = pl.multiple_of(t * kt, kt)
            k_t = ksel[:, pl.ds(k0, kt)]
            v_t = vsel[:, pl.ds(k0, kt)]
            s = _dot(qz, k_t)
            allowed = _expand_tile(sel_b, t, kt) > 0.5
            sm = jnp.where(allowed, s, NEG)
            m_new = jnp.maximum(m_old, jnp.max(sm, axis=-1, keepdims=True))
            p = jnp.where(allowed, jnp.exp2(sm - m_new), 0.0)
            alpha = jnp.exp2(m_old - m_new)
            l_new = alpha * l_old + jnp.sum(p, axis=-1, keepdims=True)
            acc = alpha * acc + _dot_nt(p.astype(BF16), v_t)
            return m_new, l_new, acc

        init = (jnp.full((rows_n, 1), NEG, F32), jnp.zeros((rows_n, 1), F32), jnp.zeros((rows_n, LANE), F32))
        m_old, l_old, acc = lax.fori_loop(0, past // kt, sel_body, init)
        newr = rows_ref[0]
        zk = jnp.zeros((LANE - nq, LANE), F32)
        k_n = jnp.concatenate([newr[:, 256:384], zk], axis=0).astype(BF16)
        v_n = jnp.concatenate([newr[:, 384:512], zk], axis=0).astype(BF16)
        ccol = lax.broadcasted_iota(jnp.int32, (1, LANE), 1)
        allow_n = (ccol <= qidx_c) & (ccol < nq)
        sm = jnp.where(allow_n, _dot_nt(qz, k_n), NEG)
        m_new = jnp.maximum(m_old, jnp.max(sm, axis=-1, keepdims=True))
        p = jnp.where(allow_n, jnp.exp2(sm - m_new), 0.0)
        alpha = jnp.exp2(m_old - m_new)
        l_new = alpha * l_old + jnp.sum(p, axis=-1, keepdims=True)
        acc = alpha * acc + _dot(p.astype(BF16), v_n)
        o_s = acc * (1.0 / jnp.maximum(l_new, 1e-30))
        wc = wcache_ref[0, 0]
        wn = wnew_ref[0]
        n_keep = wc.shape[1]
        k_wn = jnp.concatenate([wn[:, 0:KV_W], zk], axis=0).astype(BF16)
        v_wn = jnp.concatenate([wn[:, KV_W:2 * KV_W], zk], axis=0).astype(BF16)
        s_w = jnp.concatenate([_dot(qz, wc[0:KV_W].astype(BF16)), _dot_nt(qz, k_wn)], axis=1)
        wcol = lax.broadcasted_iota(jnp.int32, (1, n_keep + LANE), 1)
        allow_w = (((wcol < n_keep) & (wcol + (WINDOW - n_keep) > qidx_c))
                   | ((wcol >= n_keep) & (wcol - n_keep <= qidx_c) & (wcol - n_keep < nq)))
        p_w = _softmax_rows(s_w, allow_w).astype(BF16)
        o_w = _dot_nt(p_w[:, 0:n_keep], wc[KV_W:2 * KV_W].astype(BF16)) + _dot(p_w[:, n_keep:], v_wn)
        slabs = []
        for g in range(NSA_KV_HEADS):
            pieces = []
            for r in range(NSA_GROUP):
                h = NSA_GROUP * g + r
                sl = slice(r * 32 + g * nq, r * 32 + (g + 1) * nq)
                pieces.append(gates[:, 3 * h:3 * h + 1] * o_c[sl] + gates[:, 3 * h + 1:3 * h + 2] * o_s[sl]
                              + gates[:, 3 * h + 2:3 * h + 3] * o_w[sl])
            slabs.extend(_place_heads(pieces, g))
        for j in range(4):
            o_ref[0, :, j * LANE:(j + 1) * LANE] = slabs[j]


def _topk_select_lanes(v, jcol, n_rounds):
    big = float(v.shape[1] + 1)

    def body(_, carry):
        v, sel = carry
        m = jnp.max(v, axis=1, keepdims=True)
        first = jnp.min(jnp.where(v == m, jcol, big), axis=1, keepdims=True)
        pick = jcol == first
        sel = jnp.where(pick & (m > 0.5 * NEG), 1.0, sel)
        v = jnp.where(pick, -jnp.inf, v)
        return v, sel

    _, sel = lax.fori_loop(0, n_rounds, body, (v, jnp.zeros_like(v)))
    return sel


def _nsa_sample(page_table, q3, small3, rows3, wnew3, wcache_t, a_t, w_bd_t, pair, cache_t, *, layer, kt):
    b, n_pages = page_table.shape
    nq = q3.shape[1]
    past = n_pages * PAGE_SIZE
    n_pp = PAGES_PER_STEP
    assert n_pages % n_pp == 0 and past % kt == 0 and nq * NSA_KV_HEADS <= 32 and nq < CMP_BLOCK
    assert n_pp * PAGE_SIZE // CMP_BLOCK == LANE
    npg = n_pages // n_pp
    per = lambda w: pl.BlockSpec((1, nq, w), lambda bi, pg, pt: (bi, 0, 0))
    page_specs = [pl.BlockSpec((1, 1, 512, PAGE_SIZE), lambda bi, pg, pt, k=k: (layer, pt[bi, pg * n_pp + k], 0, 0))
                  for k in range(n_pp)]
    const = lambda shape: pl.BlockSpec(shape, lambda bi, pg, pt: (0,) * len(shape))
    grid_spec = pltpu.PrefetchScalarGridSpec(
        num_scalar_prefetch=1,
        grid=(b, npg),
        in_specs=[per(512), per(LANE), per(512), per(256),
                  pl.BlockSpec((1, 1, 256, wcache_t.shape[3]), lambda bi, pg, pt: (layer, bi, 0, 0)),
                  const(a_t.shape), const(w_bd_t.shape), const(pair.shape)] + page_specs,
        out_specs=per(512),
        scratch_shapes=[pltpu.VMEM((LANE, past), BF16), pltpu.VMEM((LANE, past), BF16),
                        pltpu.VMEM((npg, 2 * KV_W, LANE), F32)],
    )
    return pl.pallas_call(
        functools.partial(_nsa_sample_kernel, n_pp=n_pp, kt=kt),
        grid_spec=grid_spec,
        out_shape=jax.ShapeDtypeStruct((b, nq, 512), F32),
        compiler_params=_params(("arbitrary", "arbitrary")),
        name="nsa_sample",
    )(page_table, q3, small3, rows3, wnew3, wcache_t, a_t, w_bd_t, pair, *([cache_t] * n_pp))


def _mlstm_kernel(mqk_ref, mv_ref, mo_ref, small_ref, smallt_ref, hist_ref, ext0_ref, m0_ref, cw_ref, cb_ref,
                  wq_ref, wk_ref, ng_ref, hm_ref, ext_ref, m_ref, ext_scr, m_scr, carry_scr, *, L):
    c = pl.program_id(1)

    @pl.when(c == 0)
    def _init():
        ext_scr[...] = ext0_ref[0]
        m_scr[...] = m0_ref[0]
        carry_scr[...] = hist_ref[0]

    x = mqk_ref[0]
    tl = x.shape[0]
    full = jnp.concatenate([carry_scr[...], x], axis=0)
    y = cb_ref[...]
    for j in range(MLSTM_CONV):
        y = y + cw_ref[j:j + 1, :] * full[8 - (MLSTM_CONV - 1) + j:8 - (MLSTM_CONV - 1) + j + tl]
    carry_scr[...] = full[tl:tl + 8]
    qk = _silu(y).astype(BF16)
    q_all = _dot(qk, wq_ref[...])
    k_all = _dot(qk, wk_ref[...]) * (HEAD_DIM ** -0.5)
    lane = lax.broadcasted_iota(jnp.int32, (L, LANE), 1)
    lane_row = lax.broadcasted_iota(jnp.int32, (1, LANE), 1)
    t_i = lax.broadcasted_iota(jnp.int32, (L, L), 0)
    s_i = lax.broadcasted_iota(jnp.int32, (L, L), 1)
    causal = s_i <= t_i
    tril_b = jnp.where(causal, 1.0, 0.0).astype(BF16)
    triu_b = jnp.where(t_i <= s_i, 1.0, 0.0).astype(BF16)
    eye_b = jnp.where(lax.broadcasted_iota(jnp.int32, (LANE, LANE), 0) == lax.broadcasted_iota(jnp.int32, (LANE, LANE), 1),
                      1.0, 0.0).astype(BF16)
    for u in range(tl // L):
        sl = slice(u * L, (u + 1) * L)
        sm = small_ref[0, sl, :]
        smt = smallt_ref[0, :, sl]
        bcols = _dot01(tril_b, sm)
        brows = _dot01_r(smt, triu_b)
        mrow = m_scr[...]
        for j in range(2):
            cs = slice(j * LANE, (j + 1) * LANE)
            q_slab = q_all[sl, cs]
            k_b = k_all[sl, cs].astype(BF16)
            v_slab = mv_ref[0, sl, cs].astype(F32)
            k_t = _dot_nt(eye_b, k_b).astype(BF16)
            outs = []
            for half in range(2):
                h = 2 * j + half
                keep = (lane >= HEAD_DIM * half) & (lane < HEAD_DIM * (half + 1))
                ones_lane = HEAD_DIM if half == 0 else 0
                qm = jnp.where(keep, q_slab, 0.0).astype(BF16)
                v_ext = jnp.where(keep, v_slab, jnp.where(lane == ones_lane, 1.0, 0.0))
                b_col = bcols[:, 28 + h:29 + h]
                li_col = sm[:, 24 + h:25 + h]
                b_row = brows[4 + h:5 + h, :]
                li_row = smt[h:h + 1, :]
                m_h = mrow[:, h:h + 1]
                d = jnp.where(causal, b_col - b_row + li_row, NEG)
                inter = b_col + m_h
                mt = jnp.maximum(inter, jnp.max(d, axis=-1, keepdims=True))
                a = _dot_nt(qm, k_b) * jnp.exp(d - mt)
                wi = jnp.exp(inter - mt)
                cts = ext_scr[h]
                num = wi * _dot(qm, cts.astype(BF16)) + _dot(a.astype(BF16), v_ext.astype(BF16))
                den = jnp.maximum(jnp.abs(num[:, ones_lane:ones_lane + 1]), jnp.exp(-mt))
                outs.append(num / den)
                bl = b_col[L - 1:L, :]
                g_col = bl - b_col + li_col
                m_new = jnp.maximum(bl + m_h, jnp.max(g_col, axis=0, keepdims=True))
                w_col = jnp.exp(g_col - m_new)
                dec = jnp.exp(bl + m_h - m_new)
                ext_scr[h] = dec * cts + _dot(k_t, (w_col * v_ext).astype(BF16))
                mrow = jnp.where(lane_row == h, m_new, mrow)
            hs = jnp.where(lane < HEAD_DIM, outs[0], outs[1])
            sq = hs * hs
            s0 = jnp.sum(jnp.where(lane < HEAD_DIM, sq, 0.0), axis=-1, keepdims=True)
            s1 = jnp.sum(jnp.where(lane < HEAD_DIM, 0.0, sq), axis=-1, keepdims=True)
            rs = jnp.where(lane < HEAD_DIM, lax.rsqrt(s0 * (1.0 / HEAD_DIM) + EPS), lax.rsqrt(s1 * (1.0 / HEAD_DIM) + EPS))
            hm_ref[0, sl, cs] = (mo_ref[0, sl, cs] * (hs * rs * ng_ref[:, cs])).astype(BF16)
        m_scr[...] = mrow

    @pl.when(c == pl.num_programs(1) - 1)
    def _out():
        ext_ref[0] = ext_scr[...]
        m_ref[0] = m_scr[...]


def _mlstm(mqk3, mv3, mo3, small3, smallt3, hist, ext0, m0, cw, cb, wq, wk, ng, *, tl):
    b, s, _ = mqk3.shape
    L = MLSTM_L
    assert s % tl == 0 and tl % L == 0
    blk = lambda w: pl.BlockSpec((1, tl, w), lambda bi, c: (bi, c, 0))
    st = lambda shape: pl.BlockSpec((1,) + shape, lambda bi, c: (bi,) + (0,) * len(shape))
    const = lambda shape: pl.BlockSpec(shape, lambda bi, c: (0,) * len(shape))
    return pl.pallas_call(
        functools.partial(_mlstm_kernel, L=L),
        grid=(b, s // tl),
        in_specs=[blk(256), blk(256), blk(256), blk(LANE),
                  pl.BlockSpec((1, 8, tl), lambda bi, c: (bi, 0, c)),
                  st((8, 256)), st((MLSTM_HEADS, LANE, LANE)), st((1, LANE)),
                  const((MLSTM_CONV, 256)), const((1, 256)), const((256, 256)), const((256, 256)), const((1, 256))],
        out_specs=[blk(256), st((MLSTM_HEADS, LANE, LANE)), st((1, LANE))],
        out_shape=(jax.ShapeDtypeStruct((b, s, 256), BF16),
                   jax.ShapeDtypeStruct((b, MLSTM_HEADS, LANE, LANE), F32),
                   jax.ShapeDtypeStruct((b, 1, LANE), F32)),
        scratch_shapes=[pltpu.VMEM((MLSTM_HEADS, LANE, LANE), F32), pltpu.VMEM((1, LANE), F32),
                        pltpu.VMEM((8, 256), F32)],
        compiler_params=_params(("arbitrary", "arbitrary")),
        name="mlstm",
    )(mqk3, mv3, mo3, small3, smallt3, hist, ext0, m0, cw, cb, wq, wk, ng)


FF_CHUNK = 1408


def _post_kernel(*refs, final, segmented, tiles_per_batch, seg_len):
    if segmented:
        (x_ref, onsa_ref, hm_ref, ocm_ref, mod_ref, wout_ref, g2_ref, wup_ref, cw_ref, cb_ref, wdn_ref, fg_ref,
         h1_ref, h2_ref, y_ref, fst_ref) = refs
        carry_scr = None
    else:
        (x_ref, onsa_ref, hm_ref, ocm_ref, mod_ref, wout_ref, g2_ref, wup_ref, cw_ref, cb_ref, wdn_ref, fg_ref,
         y_ref, fst_ref, carry_scr) = refs
    x = x_ref[...]
    tm = x.shape[0]
    mix = (_dot(onsa_ref[...].astype(BF16), wout_ref[0:512, :]) + _dot(hm_ref[...], wout_ref[512:768, :])
           + _dot(ocm_ref[...], wout_ref[768:1024, :]))
    mod = mod_ref[0]
    x1 = x + mod[:, 2 * D_MODEL:3 * D_MODEL] * mix
    h2 = x1 * lax.rsqrt(jnp.mean(x1 * x1, axis=-1, keepdims=True) + EPS) * g2_ref[...]
    hb = (h2 * (1.0 + mod[:, 4 * D_MODEL:5 * D_MODEL]) + mod[:, 3 * D_MODEL:4 * D_MODEL]).astype(BF16)
    if not segmented:
        @pl.when(pl.program_id(0) % tiles_per_batch == 0)
        def _reset():
            carry_scr[...] = jnp.zeros(carry_scr.shape, F32)
    else:
        tmod = lax.broadcasted_iota(jnp.int32, (tm, 1), 0) % seg_len
    acc = jnp.zeros((tm, D_MODEL), F32)
    for c0 in range(0, D_FF, FF_CHUNK):
        parts = []
        for part in range(2):
            cs = slice(part * D_FF + c0, part * D_FF + c0 + FF_CHUNK)
            up = _dot(hb, wup_ref[:, cs])
            prev = jnp.zeros((8, FF_CHUNK), F32) if segmented else carry_scr[:, cs]
            full = jnp.concatenate([prev, up], axis=0)
            s1 = full[7:7 + tm]
            s2 = full[6:6 + tm]
            if segmented:
                s1 = jnp.where(tmod >= 1, s1, h1_ref[:, cs])
                s2 = jnp.where(tmod >= 2, s2, h2_ref[:, cs])
                fst_ref[:, cs] = up
            else:
                carry_scr[:, cs] = up[tm - 8:tm]
                fst_ref[0, :, cs] = up[tm - 8:tm]
            parts.append(cb_ref[:, cs] + cw_ref[0:1, cs] * s2 + cw_ref[1:2, cs] * s1 + cw_ref[2:3, cs] * up)
        act = (_silu(parts[0]) * parts[1]).astype(BF16)
        acc = acc + _dot(act, wdn_ref[c0:c0 + FF_CHUNK, :])
    x2 = x1 + mod[:, 5 * D_MODEL:6 * D_MODEL] * acc
    if final:
        x2 = x2 * lax.rsqrt(jnp.mean(x2 * x2, axis=-1, keepdims=True) + EPS) * fg_ref[...]
    y_ref[...] = x2


def _post(x2d, onsa, hm, ocm, mod, wout, g2, wup, cw, cb, wdn, fg, h1=None, h2=None, *, tm, tiles_per_batch, final,
          seg_len=0):
    t_total = x2d.shape[0]
    n_tiles = t_total // tm
    segmented = h1 is not None
    r = mod.shape[1]
    row = lambda w: pl.BlockSpec((tm, w), lambda i: (i, 0))
    in_specs = [row(D_MODEL), row(512), row(256), row(256),
                pl.BlockSpec((1, r, 6 * D_MODEL), lambda i: (i // tiles_per_batch, 0, 0)),
                _const_spec((D_MODEL, D_MODEL)), _const_spec((1, D_MODEL)), _const_spec((D_MODEL, 2 * D_FF)),
                _const_spec((FFN_CONV, 2 * D_FF)), _const_spec((1, 2 * D_FF)), _const_spec((D_FF, D_MODEL)),
                _const_spec((1, D_MODEL))]
    args = [x2d, onsa, hm, ocm, mod, wout, g2, wup, cw, cb, wdn, fg]
    if segmented:
        in_specs += [row(2 * D_FF), row(2 * D_FF)]
        args += [h1, h2]
        out_specs = [row(D_MODEL), row(2 * D_FF)]
        out_shape = (jax.ShapeDtypeStruct((t_total, D_MODEL), F32), jax.ShapeDtypeStruct((t_total, 2 * D_FF), F32))
        scratch = []
    else:
        nb = n_tiles // tiles_per_batch
        out_specs = [row(D_MODEL), pl.BlockSpec((1, 8, 2 * D_FF), lambda i: (i // tiles_per_batch, 0, 0))]
        out_shape = (jax.ShapeDtypeStruct((t_total, D_MODEL), F32), jax.ShapeDtypeStruct((nb, 8, 2 * D_FF), F32))
        scratch = [pltpu.VMEM((8, 2 * D_FF), F32)]
    return pl.pallas_call(
        functools.partial(_post_kernel, final=final, segmented=segmented, tiles_per_batch=tiles_per_batch,
                          seg_len=seg_len),
        grid=(n_tiles,),
        in_specs=in_specs, out_specs=out_specs, out_shape=out_shape, scratch_shapes=scratch,
        compiler_params=_params(("arbitrary",)),
        name="post",
    )(*args)


def _col_perm():
    o = IN_OFF
    cols = list(range(o[0], o[4])) + list(range(o[5], o[8])) + list(range(o[10], o[12]))
    small = list(range(o[4], o[5])) + list(range(o[8], o[10]))
    small += [IN_W] * (LANE - len(small))
    return np.array(cols + small, dtype=np.int32)


def _swap_perm():
    cols = []
    for start, heads in ((0, NSA_HEADS), (512, NSA_KV_HEADS), (768, NSA_KV_HEADS), (1024, NSA_KV_HEADS)):
        for hh in range(heads):
            base = start + HEAD_DIM * hh
            cols += list(range(base + 32, base + 64)) + list(range(base, base + 32))
    return np.array(cols, dtype=np.int32)


def _block_diag(mats):
    n = len(mats)
    rows = []
    for i, m in enumerate(mats):
        rows.append(jnp.concatenate([m if j == i else jnp.zeros((m.shape[0], mats[j].shape[1]), m.dtype)
                                     for j in range(n)], axis=1))
    return jnp.concatenate(rows, axis=0)


def _rope_tables(pos):
    half = HEAD_DIM // 2
    inv = jnp.power(ROPE_THETA, -jnp.arange(half, dtype=F32) / half)
    ang = pos.astype(F32)[:, None] * inv[None, :]
    cos = jnp.cos(ang)
    sin = jnp.sin(ang)
    return jnp.tile(cos, (1, 4)), jnp.concatenate([-sin, sin, -sin, sin], axis=1)


def _layer_weights(l, w):
    win_z = jnp.concatenate([w['w_in'][l], jnp.zeros((D_MODEL, 1), F32)], axis=1)
    bin_z = jnp.concatenate([w['b_in'][l], jnp.zeros((1,), F32)])
    perm, swp = _col_perm(), _swap_perm()
    cmp_a = w['cmp_a'][l]
    cmp_w = w['cmp_w'][l]
    c_ws = jnp.tril(w['c_ws'][l])
    return dict(
        wp=win_z[:, perm].astype(BF16), bp=bin_z[perm][None, :],
        wsw=w['w_in'][l][:, swp].astype(BF16), bsw=w['b_in'][l][swp][None, :],
        g1=w['norm1_g'][l][None, :], g2=w['norm2_g'][l][None, :],
        cg=w['c_norm_g'][l][None, :], c_ws=c_ws, c_bs=w['c_bs'][l],
        a_tile=jnp.concatenate([cmp_a[0].reshape(CMP_BLOCK, KV_W), cmp_a[1].reshape(CMP_BLOCK, KV_W)], axis=1),
        w_bd=_block_diag([cmp_w[0, 0], cmp_w[0, 1], cmp_w[1, 0], cmp_w[1, 1]]).astype(BF16),
        m_cw=w['m_conv_w'][l], m_cb=w['m_conv_b'][l][None, :],
        wq=_block_diag([w['m_wq'][l][h] for h in range(MLSTM_HEADS)]).astype(BF16),
        wk=_block_diag([w['m_wk'][l][h] for h in range(MLSTM_HEADS)]).astype(BF16),
        ng=w['m_norm_g'][l].reshape(1, MLSTM_W),
        wout=w['w_out'][l].astype(BF16), wup=w['w_up'][l].astype(BF16), wdn=w['w_down'][l].astype(BF16),
        f_cw=w['f_conv_w'][l], f_cb=w['f_conv_b'][l][None, :],
    )


def _mlstm_state_in(c0, n0, m0):
    b = c0.shape[0]
    tiles = []
    for h in range(MLSTM_HEADS):
        r0 = HEAD_DIM * (h % 2)
        ol = HEAD_DIM if h % 2 == 0 else 0
        t = jnp.zeros((b, LANE, LANE), F32)
        t = t.at[:, r0:r0 + HEAD_DIM, r0:r0 + HEAD_DIM].set(jnp.swapaxes(c0[:, h], 1, 2))
        t = t.at[:, r0:r0 + HEAD_DIM, ol].set(n0[:, h])
        tiles.append(t)
    m = jnp.zeros((b, 1, LANE), F32).at[:, 0, 0:MLSTM_HEADS].set(m0)
    return jnp.stack(tiles, axis=1), m


def _mlstm_state_out(ext, m):
    cs, ns = [], []
    for h in range(MLSTM_HEADS):
        r0 = HEAD_DIM * (h % 2)
        ol = HEAD_DIM if h % 2 == 0 else 0
        cs.append(jnp.swapaxes(ext[:, h, r0:r0 + HEAD_DIM, r0:r0 + HEAD_DIM], 1, 2))
        ns.append(ext[:, h, r0:r0 + HEAD_DIM, ol])
    return jnp.stack(cs, axis=1), jnp.stack(ns, axis=1), m[:, 0, 0:MLSTM_HEADS]


def _layer_prompt(x2d, mod, lw, fg, *, b, s, final, tm=256):
    assert tm % SEL_KT == 0
    tiles_per_batch = s // tm
    cos_t, sin_t = _rope_tables(jnp.arange(s, dtype=jnp.int32))
    wc = jnp.concatenate([lw['c_ws'][g] for g in range(CMLP_GROUPS)], axis=1).astype(BF16)
    bc = jnp.repeat(lw['c_bs'].T, HEAD_DIM, axis=1)
    rows, win, qb, kvb, small, mqk, mv, mo, ocm, _ = _inproj(
        x2d, mod, lw['g1'], lw['wp'], lw['bp'], lw['wsw'], lw['bsw'], cos_t, sin_t, wc, bc, lw['cg'],
        tm=tm, tiles_per_batch=tiles_per_batch, tc=CMLP_CHUNK, q_dtype=BF16)
    rows3 = rows.reshape(b, s, 512)
    nb = s // CMP_BLOCK
    comp = _compress(rows3, lw['a_tile'], lw['w_bd'], rows_per_step=min(s, 1024))
    comp = comp.reshape(b, nb // 2, 2, 2 * KV_W).transpose(0, 2, 1, 3).reshape(b, nb, 2 * KV_W)
    onsa = _nsa_prompt(qb, small, comp[..., 0:KV_W], comp[..., KV_W:], kvb.reshape(b, s, 768), b=b, s=s, kt=SEL_KT)
    zeros = lambda *sh: jnp.zeros(sh, F32)
    ext0, m0 = _mlstm_state_in(zeros(b, MLSTM_HEADS, HEAD_DIM, HEAD_DIM), zeros(b, MLSTM_HEADS, HEAD_DIM),
                               zeros(b, MLSTM_HEADS))
    small3 = small.reshape(b, s, LANE)
    smallt = jnp.swapaxes(small3[:, :, 24:32], 1, 2)
    hm, ext, mout = _mlstm(mqk.reshape(b, s, 256), mv.reshape(b, s, 256), mo.reshape(b, s, 256), small3, smallt,
                           zeros(b, 8, 256), ext0, m0, lw['m_cw'], lw['m_cb'], lw['wq'], lw['wk'], lw['ng'], tl=tm)
    c1, n1, m1 = _mlstm_state_out(ext, mout)
    y, fst = _post(x2d, onsa, hm.reshape(b * s, 256), ocm, mod, lw['wout'], lw['g2'], lw['wup'], lw['f_cw'],
                   lw['f_cb'], lw['wdn'], fg, tm=tm, tiles_per_batch=tiles_per_batch, final=final)
    n_keep = min(WINDOW, s)
    state = (rows3.reshape(b, s, 4, NSA_KV_HEADS, HEAD_DIM),
             win.reshape(b, s, 2, NSA_KV_HEADS, HEAD_DIM)[:, s - n_keep:],
             c1, n1, m1, mqk.reshape(b, s, 256)[:, s - (MLSTM_CONV - 1):], fst[:, 8 - (FFN_CONV - 1):])
    return y, state


def _layer_sample(x2d, mod, lw, fg, cache_t, page_table, wcache_t, mc, mn, mm, mconv, fconv, *, layer, b, s, final,
                  kt=512):
    t = b * s
    past = page_table.shape[1] * PAGE_SIZE
    pos = past + jnp.arange(s, dtype=jnp.int32)
    cos1, sin1 = _rope_tables(pos)
    cos_t, sin_t = jnp.tile(cos1, (b, 1)), jnp.tile(sin1, (b, 1))
    eye_b = jnp.eye(b, dtype=F32)
    wc = jnp.concatenate([jnp.kron(eye_b, lw['c_ws'][g][:s, :s]) for g in range(CMLP_GROUPS)], axis=1).astype(BF16)
    bc = jnp.tile(jnp.repeat(lw['c_bs'].T[:s], HEAD_DIM, axis=1), (b, 1))
    rows, win, qf, _, small, mqk, mv, mo, ocm, cvn = _inproj(
        x2d, mod, lw['g1'], lw['wp'], lw['bp'], lw['wsw'], lw['bsw'], cos_t, sin_t, wc, bc, lw['cg'],
        tm=t, tiles_per_batch=1, tc=t, q_dtype=F32)
    nbc = past // CMP_BLOCK
    pair = jnp.repeat(jnp.eye(nbc // 2, dtype=BF16), 2, axis=0)
    n_keep = wcache_t.shape[3]
    wnew3 = win.reshape(b, s, 256)
    a_t = jnp.tile(lw['a_tile'].T, (1, PAGE_SIZE // CMP_BLOCK))
    onsa = _nsa_sample(page_table, qf.reshape(b, s, 512), small.reshape(b, s, LANE), rows.reshape(b, s, 512), wnew3,
                       wcache_t, a_t, lw['w_bd'].T, pair, cache_t, layer=layer, kt=kt).reshape(t, 512)
    L = MLSTM_L
    pad3 = lambda a, fill=0.0: jnp.pad(a.reshape(b, s, -1), ((0, 0), (0, L - s), (0, 0)), constant_values=fill)
    small3 = small.reshape(b, s, LANE)
    lane = jnp.arange(LANE)
    small_p = jnp.where((lane >= 24) & (lane < 28), pad3(small3, NEG), pad3(small3))
    smallt = jnp.swapaxes(small_p[:, :, 24:32], 1, 2)
    hist = jnp.pad(mconv, ((0, 0), (8 - (MLSTM_CONV - 1), 0), (0, 0)))
    ext0, m0 = _mlstm_state_in(mc, mn, mm)
    hm, ext, mout = _mlstm(pad3(mqk), pad3(mv), pad3(mo), small_p, smallt, hist, ext0, m0, lw['m_cw'], lw['m_cb'],
                           lw['wq'], lw['wk'], lw['ng'], tl=L)
    c1, n1, m1 = _mlstm_state_out(ext, mout)
    hm = hm[:, :s].reshape(t, 256)
    h1 = jnp.zeros((b, s, 2 * D_FF), F32).at[:, 0].set(fconv[:, 1]).reshape(t, 2 * D_FF)
    h2 = jnp.zeros((b, s, 2 * D_FF), F32).at[:, 0].set(fconv[:, 0]).at[:, 1].set(fconv[:, 1]).reshape(t, 2 * D_FF)
    y, up = _post(x2d, onsa, hm, ocm, mod, lw['wout'], lw['g2'], lw['wup'], lw['f_cw'], lw['f_cb'], lw['wdn'], fg,
                  h1, h2, tm=t, tiles_per_batch=1, final=final, seg_len=s)
    new_win_t = jnp.concatenate([wcache_t[layer][:, :, s:], jnp.swapaxes(wnew3, 1, 2)], axis=2)
    new_win = jnp.transpose(new_win_t.reshape(b, 2, NSA_KV_HEADS, HEAD_DIM, n_keep), (0, 4, 1, 2, 3))
    state = (rows.reshape(b, s, 4, NSA_KV_HEADS, HEAD_DIM), new_win, c1, n1, m1,
             mqk.reshape(b, s, 256)[:, s - (MLSTM_CONV - 1):], cvn.reshape(b, s, CMLP_W),
             up.reshape(b, s, 2 * D_FF)[:, s - (FFN_CONV - 1):])
    return y, state


def kernel(x_prompt, x_sample, cache_nsa_kv, page_table, cache_win_kv, state_mlstm_C, state_mlstm_n, state_mlstm_m, state_mlstm_conv, state_ffn_conv, c_prompt, c_sample, ada_w, ada_b, norm1_g, norm2_g, w_in, b_in, cmp_a, cmp_w, m_conv_w, m_conv_b, m_wq, m_wk, m_norm_g, c_norm_g, c_ws, c_bs, w_out, w_up, f_conv_w, f_conv_b, w_down, final_norm_g):
    bp, sp, _ = x_prompt.shape
    bs, ss, _ = x_sample.shape
    depth = ada_w.shape[0]
    w = dict(w_in=w_in, b_in=b_in, norm1_g=norm1_g, norm2_g=norm2_g, cmp_a=cmp_a, cmp_w=cmp_w, m_conv_w=m_conv_w,
             m_conv_b=m_conv_b, m_wq=m_wq, m_wk=m_wk, m_norm_g=m_norm_g, c_norm_g=c_norm_g, c_ws=c_ws, c_bs=c_bs,
             w_out=w_out, w_up=w_up, f_conv_w=f_conv_w, f_conv_b=f_conv_b, w_down=w_down)
    n_c = bp + bs
    n_pad = -(-n_c // 8) * 8
    c_all = jnp.pad(jnp.concatenate([c_prompt, c_sample], axis=0), ((0, n_pad - n_c), (0, 0)))
    mod_all = _ada(c_all, ada_w, ada_b)
    fg = final_norm_g[None, :]
    xp = x_prompt.reshape(bp * sp, D_MODEL)
    xs = x_sample.reshape(bs * ss, D_MODEL)
    n_pool = cache_nsa_kv.shape[1]
    n_keep = cache_win_kv.shape[2]
    cache_t = jnp.transpose(cache_nsa_kv, (0, 1, 3, 4, 5, 2)).reshape(depth, n_pool, 512, PAGE_SIZE)
    wcache_t = jnp.transpose(cache_win_kv, (0, 1, 3, 4, 5, 2)).reshape(depth, bs, 256, n_keep)
    st_p, st_s = [], []
    for l in range(depth):
        lw = _layer_weights(l, w)
        final = l == depth - 1
        mod_p = mod_all[l, 0:bp][:, None, :]
        mod_s = jnp.repeat(mod_all[l, bp:bp + bs], ss, axis=0)[None]
        xp, sp_l = _layer_prompt(xp, mod_p, lw, fg, b=bp, s=sp, final=final)
        xs, ss_l = _layer_sample(xs, mod_s, lw, fg, cache_t, page_table, wcache_t, state_mlstm_C[l],
                                 state_mlstm_n[l], state_mlstm_m[l], state_mlstm_conv[l], state_ffn_conv[l],
                                 layer=l, b=bs, s=ss, final=final)
        st_p.append(sp_l)
        st_s.append(ss_l)
    sp_st = [jnp.stack(t) for t in zip(*st_p)]
    ss_st = [jnp.stack(t) for t in zip(*st_s)]
    y_prompt = xp.reshape(bp, sp, D_MODEL)
    y_sample = xs.reshape(bs, ss, D_MODEL)
    return (y_prompt, y_sample, sp_st[0], ss_st[0], sp_st[1], ss_st[1],
            sp_st[2], sp_st[3], sp_st[4], sp_st[5],
            ss_st[2], ss_st[3], ss_st[4], ss_st[5],
            ss_st[6], sp_st[6], ss_st[7])
```

```python
import functools

import numpy as np
import jax
import jax.numpy as jnp
from jax import lax
from jax.experimental import pallas as pl
from jax.experimental.pallas import tpu as pltpu

F32 = jnp.float32
BF16 = jnp.bfloat16

D_MODEL = 1024
HEAD_DIM = 64
NSA_HEADS = 8
NSA_KV_HEADS = 2
NSA_GROUP = 4
CMP_BLOCK = 32
SEL_BLOCK = 64
SEL_TOPN = 16
WINDOW = 512
ROPE_THETA = 10000.0
MLSTM_HEADS = 4
MLSTM_CONV = 4
CMLP_GROUPS = 4
CMLP_CHUNK = 128
D_FF = 2816
FFN_CONV = 3
PAGE_SIZE = 128
NSA_W = 512
KV_W = 128
MLSTM_W = 256
CMLP_W = 256
IN_SIZES = (NSA_W, 2 * KV_W, 2 * KV_W, 2 * KV_W, 3 * NSA_HEADS, MLSTM_W, MLSTM_W, MLSTM_W, MLSTM_HEADS,
            MLSTM_HEADS, CMLP_W, CMLP_W)
IN_OFF = tuple(int(v) for v in np.cumsum((0,) + IN_SIZES))
IN_W = IN_OFF[-1]
EPS = 1e-6
NEG = -1e30
FORCE = 1e4
LANE = 128
NP_COLS = 2688
NSW_COLS = 896
VMEM_LIMIT = 56 * 1024 * 1024
PAGES_PER_STEP = 32
NSA_TQ = 256
SEL_KT = 256
Q_SCALE = HEAD_DIM ** -0.5 * 1.4426950408889634
MLSTM_L = 128


def _dot(a, b):
    return jnp.dot(a, b, preferred_element_type=F32)


def _dot_nt(a, b):
    return lax.dot_general(a, b, (((1,), (1,)), ((), ())), preferred_element_type=F32)


def _dot01(m01, x):
    hi = x.astype(BF16)
    r1 = x - hi.astype(F32)
    mid = r1.astype(BF16)
    lo = (r1 - mid.astype(F32)).astype(BF16)
    return _dot(m01, hi) + _dot(m01, mid) + _dot(m01, lo)


def _dot01_r(x, m01):
    hi = x.astype(BF16)
    r1 = x - hi.astype(F32)
    mid = r1.astype(BF16)
    lo = (r1 - mid.astype(F32)).astype(BF16)
    return _dot(hi, m01) + _dot(mid, m01) + _dot(lo, m01)


def _sigmoid(x):
    return 1.0 / (1.0 + jnp.exp(-x))


def _silu(x):
    return x * _sigmoid(x)


def _gelu_tanh(x):
    return 0.5 * x * (1.0 + jnp.tanh(0.7978845608028654 * (x + 0.044715 * (x * x * x))))


def _log_sigmoid(x):
    return jnp.minimum(x, 0.0) - jnp.log(1.0 + jnp.exp(-jnp.abs(x)))


def _const_spec(shape):
    nd = len(shape)
    return pl.BlockSpec(shape, lambda *_: (0,) * nd)


def _params(sem):
    return pltpu.CompilerParams(dimension_semantics=sem, vmem_limit_bytes=VMEM_LIMIT)


def _ada_kernel(c_ref, w_ref, b_ref, o_ref):
    c = c_ref[...]
    o_ref[0] = _dot(_silu(c).astype(BF16), w_ref[0]) + b_ref[0]


def _ada(c_all, ada_w, ada_b):
    depth = ada_w.shape[0]
    n = c_all.shape[0]
    return pl.pallas_call(
        _ada_kernel,
        grid=(depth, 6),
        in_specs=[pl.BlockSpec((n, D_MODEL), lambda l, j: (0, 0)),
                  pl.BlockSpec((1, D_MODEL, D_MODEL), lambda l, j: (l, 0, j)),
                  pl.BlockSpec((1, 1, D_MODEL), lambda l, j: (l, 0, j))],
        out_specs=pl.BlockSpec((1, n, D_MODEL), lambda l, j: (l, 0, j)),
        out_shape=jax.ShapeDtypeStruct((depth, n, 6 * D_MODEL), F32),
        compiler_params=_params(("arbitrary", "arbitrary")),
        name="ada",
    )(c_all, ada_w.astype(BF16), ada_b.reshape(depth, 1, 6 * D_MODEL))


def _inproj_kernel(x_ref, mod_ref, g_ref, w_ref, b_ref, wsw_ref, bsw_ref, cos_ref, sin_ref, wc_ref, bc_ref, cg_ref,
                   rows_ref, win_ref, q_ref, kvb_ref, small_ref, mqk_ref, mv_ref, mo_ref, ocm_ref, cvn_ref,
                   *cmp_rows_ref, tc, tiles_per_batch):
    x = x_ref[...]
    tm = x.shape[0]
    xn = x * lax.rsqrt(jnp.mean(x * x, axis=-1, keepdims=True) + EPS) * g_ref[...]
    mod = mod_ref[0]
    h = xn * (1.0 + mod[:, D_MODEL:2 * D_MODEL]) + mod[:, 0:D_MODEL]
    hb = h.astype(BF16)
    z = _dot(hb, w_ref[...]) + b_ref[...]
    zs = _dot(hb, wsw_ref[...]) + bsw_ref[...]
    cos = cos_ref[...]
    sin = sin_ref[...]

    def rope(c0, s0):
        return z[:, c0:c0 + LANE] * cos + zs[:, s0:s0 + LANE] * sin

    for j in range(4):
        q_ref[:, j * LANE:(j + 1) * LANE] = (rope(j * LANE, j * LANE) * Q_SCALE).astype(q_ref.dtype)
    kc = rope(512, 512)
    vc = z[:, 640:768]
    ks = rope(768, 640)
    vs = z[:, 896:1024]
    kw = rope(1024, 768)
    vw = z[:, 1152:1280]
    if cmp_rows_ref:
        rows_ref[0] = jnp.transpose(jnp.concatenate([kc, vc, ks, vs], axis=1))
        cmp_rows_ref[0][:, 0:128] = kc
        cmp_rows_ref[0][:, 128:256] = vc
    else:
        rows_ref[:, 0:128] = kc
        rows_ref[:, 128:256] = vc
        rows_ref[:, 256:384] = ks
        rows_ref[:, 384:512] = vs
    win_ref[:, 0:128] = kw
    win_ref[:, 128:256] = vw
    lane_k = lax.broadcasted_iota(jnp.int32, (tm, LANE), 1)
    row0 = (pl.program_id(0) % tiles_per_batch) * tm
    blk = ((row0 + lax.broadcasted_iota(jnp.int32, (tm, LANE), 0)) // SEL_BLOCK) % HEAD_DIM
    low = lane_k < HEAD_DIM
    kvb_ref[:, 0:128] = jnp.where(low, ks, jnp.where(lane_k == HEAD_DIM + blk, 1.0, 0.0)).astype(BF16)
    kvb_ref[:, 128:256] = jnp.where(low, jnp.where(lane_k == blk, 1.0, 0.0), ks).astype(BF16)
    kvb_ref[:, 256:384] = jnp.where(low, vs, 1.0).astype(BF16)
    kvb_ref[:, 384:512] = jnp.where(low, 1.0, vs).astype(BF16)
    kvb_ref[:, 512:640] = kw.astype(BF16)
    kvb_ref[:, 640:768] = vw.astype(BF16)
    sm = z[:, 2560:2688]
    lane = lax.broadcasted_iota(jnp.int32, sm.shape, 1)
    small_ref[...] = jnp.where(lane < 24, _sigmoid(sm), jnp.where(lane < 28, sm, _log_sigmoid(sm)))
    mqk_ref[...] = z[:, 1280:1536]
    mv_ref[...] = z[:, 1536:1792].astype(BF16)
    mo_ref[...] = _sigmoid(z[:, 1792:2048])
    cu = _gelu_tanh(z[:, 2048:2304])
    cv = _gelu_tanh(z[:, 2304:2560])
    cvn = cv * lax.rsqrt(jnp.mean(cv * cv, axis=-1, keepdims=True) + EPS) * cg_ref[...]
    cvn_ref[...] = cvn
    lane_c = lax.broadcasted_iota(jnp.int32, (tc, CMLP_W), 1) // HEAD_DIM
    for c in range(tm // tc):
        v = cvn[c * tc:(c + 1) * tc]
        vbd = jnp.concatenate([jnp.where(lane_c == g, v, 0.0) for g in range(CMLP_GROUPS)], axis=0).astype(BF16)
        s = _dot(wc_ref[...], vbd) + bc_ref[...]
        ocm_ref[c * tc:(c + 1) * tc, :] = (cu[c * tc:(c + 1) * tc] * s).astype(BF16)


def _inproj(x2d, mod, g1, wp, bp, wsw, bsw, cos_t, sin_t, wc, bc, cg, *, tm, tiles_per_batch, tc, q_dtype,
            rows_feature_major):
    t_total = x2d.shape[0]
    n_tiles = t_total // tm
    r = mod.shape[1]
    row = lambda w: pl.BlockSpec((tm, w), lambda i: (i, 0))
    if rows_feature_major:
        rows_shape = jax.ShapeDtypeStruct((n_tiles // tiles_per_batch, 512, tiles_per_batch * tm), F32)
        rows_spec = pl.BlockSpec((1, 512, tm), lambda i: (i // tiles_per_batch, 0, i % tiles_per_batch))
    else:
        rows_shape = jax.ShapeDtypeStruct((t_total, 512), F32)
        rows_spec = row(512)
    out_shapes = (
        rows_shape,
        jax.ShapeDtypeStruct((t_total, 256), F32),
        jax.ShapeDtypeStruct((t_total, 512), q_dtype),
        jax.ShapeDtypeStruct((t_total, 768), BF16),
        jax.ShapeDtypeStruct((t_total, 128), F32),
        jax.ShapeDtypeStruct((t_total, 256), F32),
        jax.ShapeDtypeStruct((t_total, 256), BF16),
        jax.ShapeDtypeStruct((t_total, 256), F32),
        jax.ShapeDtypeStruct((t_total, 256), BF16),
        jax.ShapeDtypeStruct((t_total, 256), F32),
    )
    out_specs = [rows_spec, row(256), row(512), row(768), row(128), row(256), row(256), row(256), row(256), row(256)]
    if rows_feature_major:
        out_shapes += (jax.ShapeDtypeStruct((t_total, 256), F32),)
        out_specs.append(row(256))
    return pl.pallas_call(
        functools.partial(_inproj_kernel, tc=tc, tiles_per_batch=tiles_per_batch),
        grid=(n_tiles,),
        in_specs=[row(D_MODEL),
                  pl.BlockSpec((1, r, 6 * D_MODEL), lambda i: (i // tiles_per_batch, 0, 0)),
                  _const_spec((1, D_MODEL)),
                  _const_spec((D_MODEL, NP_COLS)), _const_spec((1, NP_COLS)),
                  _const_spec((D_MODEL, NSW_COLS)), _const_spec((1, NSW_COLS)),
                  pl.BlockSpec((tm, LANE), lambda i: (i % tiles_per_batch, 0)),
                  pl.BlockSpec((tm, LANE), lambda i: (i % tiles_per_batch, 0)),
                  _const_spec(wc.shape), _const_spec(bc.shape), _const_spec((1, CMLP_W))],
        out_specs=out_specs,
        out_shape=out_shapes,
        compiler_params=_params(("arbitrary",)),
        name="inproj",
    )(x2d, mod, g1, wp, bp, wsw, bsw, cos_t, sin_t, wc, bc, cg)


def _compress_kernel(rows_ref, a_ref, w_ref, o_ref):
    x = rows_ref[0]
    n = x.shape[0] // CMP_BLOCK
    pooled = jnp.sum(x.reshape(n, CMP_BLOCK, 2 * KV_W) * a_ref[...][None], axis=1)
    o_ref[0] = _dot(pooled.astype(BF16), w_ref[...])


def _compress(rows3, a_tile, w_bd, *, rows_per_step):
    b, s, _ = rows3.shape
    nb = s // CMP_BLOCK
    steps = s // rows_per_step
    return pl.pallas_call(
        _compress_kernel,
        grid=(b, steps),
        in_specs=[pl.BlockSpec((1, rows_per_step, 2 * KV_W), lambda i, j: (i, j, 0)),
                  _const_spec((CMP_BLOCK, 2 * KV_W)), _const_spec((2 * KV_W, 2 * KV_W))],
        out_specs=pl.BlockSpec((1, rows_per_step // CMP_BLOCK, 2 * KV_W), lambda i, j: (i, j, 0)),
        out_shape=jax.ShapeDtypeStruct((b, nb, 2 * KV_W), F32),
        compiler_params=_params(("arbitrary", "arbitrary")),
        name="compress",
    )(rows3, a_tile, w_bd)


def _group_queries(qf, g):
    lane = lax.broadcasted_iota(jnp.int32, (qf.shape[0], LANE), 1)
    keep = (lane >= HEAD_DIM * g) & (lane < HEAD_DIM * (g + 1))
    out = []
    for r in range(NSA_GROUP):
        h = NSA_GROUP * g + r
        slab = qf[:, (h // 2) * LANE:(h // 2 + 1) * LANE]
        if h % 2 != g:
            slab = pltpu.roll(slab, HEAD_DIM, axis=1)
        out.append(jnp.where(keep, slab, 0.0))
    return out


def _place_heads(pieces, g):
    lane = lax.broadcasted_iota(jnp.int32, pieces[0].shape, 1)
    slabs = []
    for j in range(2):
        halves = []
        for half in range(2):
            p = pieces[2 * j + half]
            if half != g:
                p = pltpu.roll(p, HEAD_DIM, axis=1)
            halves.append(p)
        slabs.append(jnp.where(lane < HEAD_DIM, halves[0], halves[1]))
    return slabs


def _softmax_rows(s, allowed):
    sm = jnp.where(allowed, s, NEG)
    m = jnp.max(sm, axis=-1, keepdims=True)
    e = jnp.where(allowed, jnp.exp2(sm - m), 0.0)
    l = jnp.sum(e, axis=-1, keepdims=True)
    return e * (1.0 / jnp.maximum(l, 1e-30))


def _topk_select(vs, jrow, n_rounds):
    big = float(vs[0].shape[0] + 1)

    def body(_, carry):
        out = []
        for v, sel in carry:
            m = jnp.max(v, axis=0, keepdims=True)
            first = jnp.min(jnp.where(v == m, jrow, big), axis=0, keepdims=True)
            pick = jrow == first
            out.append((jnp.where(pick, -jnp.inf, v), jnp.where(pick & (m > 0.5 * NEG), 1.0, sel)))
        return tuple(out)

    res = lax.fori_loop(0, n_rounds, body, tuple((v, jnp.zeros_like(v)) for v in vs))
    return [sel for _, sel in res]


def _expand_tile(sel_b, t, kt):
    nblk = sel_b.shape[1]
    jrow = lax.broadcasted_iota(jnp.int32, (nblk, kt), 0) - t * (kt // SEL_BLOCK)
    cdiv = lax.broadcasted_iota(jnp.int32, (nblk, kt), 1) // SEL_BLOCK
    e = jnp.where(jrow == cdiv, 1.0, 0.0).astype(BF16)
    return _dot(sel_b, e)


def _nsa_prompt_kernel(q_ref, small_ref, kc_ref, vc_ref, kvb_ref, o_ref, m_scr, acc_scr, lhs_scr, s_scr, *, kt, nbs):
    i = pl.program_id(1)
    tq = q_ref.shape[0]
    nbc = kc_ref.shape[1]
    s_len = kvb_ref.shape[1]
    qf = q_ref[...].astype(F32)
    gates = small_ref[...]
    q0 = i * tq
    qpos_c = q0 + lax.broadcasted_iota(jnp.int32, (tq, 1), 0)
    qpos_r = q0 + lax.broadcasted_iota(jnp.int32, (1, tq), 1)
    eye = jnp.where(lax.broadcasted_iota(jnp.int32, (tq, tq), 0) == lax.broadcasted_iota(jnp.int32, (tq, tq), 1),
                    1.0, 0.0).astype(BF16)
    qpos_r4 = q0 + lax.broadcasted_iota(jnp.int32, (1, NSA_GROUP * tq), 1) % tq
    kc = kc_ref[0].astype(BF16)
    vc = vc_ref[0].astype(BF16)
    half = nbc // 2
    rho_c = lax.broadcasted_iota(jnp.int32, (nbc, 1), 0)
    end_c = jnp.where(rho_c < half, 2 * rho_c, 2 * (rho_c - half) + 1) * CMP_BLOCK + (CMP_BLOCK - 1)
    rho_r = lax.broadcasted_iota(jnp.int32, (1, nbc), 1)
    end_r = jnp.where(rho_r < half, 2 * rho_r, 2 * (rho_r - half) + 1) * CMP_BLOCK + (CMP_BLOCK - 1)
    w_start = pl.multiple_of(jnp.maximum(q0 - WINDOW, 0), LANE)
    w_len = WINDOW + tq
    wpos = w_start + lax.broadcasted_iota(jnp.int32, (1, w_len), 1)
    bias_w = jnp.where((wpos <= qpos_c) & (wpos > qpos_c - WINDOW), 0.0, NEG)
    bias_c = jnp.where(end_r <= qpos_c, 0.0, NEG)
    any_c = qpos_c >= CMP_BLOCK - 1
    k_w = kvb_ref[0, pl.ds(w_start, w_len), 512:640]
    v_w = kvb_ref[0, pl.ds(w_start, w_len), 640:768]
    out_slabs = []

    def attend(q_b, k, v_ones, bias):
        s = _dot_nt(q_b, k) + bias
        p = jnp.exp2(s - jnp.max(s, axis=-1, keepdims=True))
        o = _dot(p.astype(BF16), v_ones)
        return o * (1.0 / pltpu.roll(o, HEAD_DIM, axis=1))

    lane_g = lax.broadcasted_iota(jnp.int32, (1, LANE), 1)
    one_b = jnp.ones((), BF16)
    jrow_i = lax.broadcasted_iota(jnp.int32, (nbs, tq), 0)
    qz_bs, o_cs, scores_g = [], [], []
    for g in range(NSA_KV_HEADS):
        qz4 = _group_queries(qf, g)
        qz_b = [x.astype(BF16) for x in qz4]
        qz = jnp.concatenate(qz_b, axis=0)
        keep_g = (lane_g >= HEAD_DIM * g) & (lane_g < HEAD_DIM * (g + 1))
        vc_x = jnp.where(keep_g, vc, one_b)
        o_c = [jnp.where(any_c, attend(qz_b[r], kc, vc_x, bias_c), 0.0) for r in range(NSA_GROUP)]
        s_t = _dot_nt(kc, qz)
        sm = s_t + jnp.where(end_c <= qpos_r4, 0.0, NEG)
        et = jnp.exp2(sm - jnp.max(sm, axis=0, keepdims=True))
        pt = et * (1.0 / jnp.sum(et, axis=0, keepdims=True))
        ph = pt[:, 0:tq] + pt[:, tq:2 * tq] + pt[:, 2 * tq:3 * tq] + pt[:, 3 * tq:4 * tq]
        imp = jnp.where(qpos_r >= CMP_BLOCK - 1, ph[0:half] + ph[half:nbc], 0.0)
        if nbs > half:
            imp = jnp.concatenate([imp, jnp.zeros((nbs - half, tq), F32)], axis=0)
        forced = (jrow_i == qpos_r // SEL_BLOCK) | (jrow_i == 0)
        valid = jrow_i * SEL_BLOCK <= qpos_r
        scores_g.append(jnp.where(valid, imp + jnp.where(forced, FORCE, 0.0), NEG))
        qz_bs.append(qz_b)
        o_cs.append(o_c)
    sel_ts = [_topk_select([sc], jrow_i.astype(F32), min(SEL_TOPN, nbs))[0] for sc in scores_g]
    for g in range(NSA_KV_HEADS):
        qz_b, o_c, sel_t = qz_bs[g], o_cs[g], sel_ts[g]
        keep_g = (lane_g >= HEAD_DIM * g) & (lane_g < HEAD_DIM * (g + 1))
        m_scr[...] = jnp.full(m_scr.shape, NEG, F32)
        acc_scr[...] = jnp.zeros(acc_scr.shape, F32)
        sel_q = _dot_nt(eye, sel_t.astype(BF16))
        if nbs < LANE:
            sel_q = jnp.concatenate([sel_q, jnp.zeros((tq, LANE - nbs), F32)], axis=1)
        negsel = (sel_q - 1.0) * 1e30
        base = HEAD_DIM * (1 - g)
        lane_q = lax.broadcasted_iota(jnp.int32, (tq, LANE), 1)
        feat_mask = (lane_q >= base) & (lane_q < base + HEAD_DIM)
        for hf in range(max(1, nbs // HEAD_DIM)):
            feat = (negsel if HEAD_DIM * hf == base else pltpu.roll(negsel, HEAD_DIM, axis=1)).astype(BF16)
            for r in range(NSA_GROUP):
                lhs_scr[hf, r] = jnp.where(feat_mask, feat, qz_b[r])
        tiles_per_half = HEAD_DIM * SEL_BLOCK // kt

        def scores(t, slot):
            k_t = kvb_ref[0, pl.ds(pl.multiple_of(t * kt, kt), kt), g * LANE:(g + 1) * LANE]
            hf = t // tiles_per_half
            for r in range(NSA_GROUP):
                s_scr[slot, r] = _dot_nt(lhs_scr[hf, r], k_t)

        def consume(t, slot, causal):
            k0 = pl.multiple_of(t * kt, kt)
            v_t = kvb_ref[0, pl.ds(k0, kt), (2 + g) * LANE:(3 + g) * LANE]
            if causal:
                kpos = k0 + lax.broadcasted_iota(jnp.int32, (1, kt), 1)
                bias = jnp.where(kpos <= qpos_c, 0.0, NEG)
            for r in range(NSA_GROUP):
                s = s_scr[slot, r]
                if causal:
                    s = s + bias
                m_old = m_scr[r]
                m_new = jnp.maximum(m_old, jnp.max(s, axis=-1, keepdims=True))
                p = jnp.exp2(s - jnp.concatenate([m_new] * (kt // LANE), axis=1))
                acc_scr[r] = jnp.exp2(m_old - m_new) * acc_scr[r] + _dot(p.astype(BF16), v_t)
                m_scr[r] = m_new

        n_full = q0 // kt
        scores(0, 0)

        def pair_body(tp, _):
            scores(2 * tp + 1, 1)
            consume(2 * tp, 0, False)
            scores(2 * tp + 2, 0)
            consume(2 * tp + 1, 1, False)
            return 0

        lax.fori_loop(0, n_full // 2, pair_body, 0)

        @pl.when(n_full % 2 == 1)
        def _odd():
            scores(n_full, 1)
            consume(n_full - 1, 0, False)
            consume(n_full, 1, True)

        @pl.when(n_full % 2 == 0)
        def _even():
            consume(n_full, 0, True)
        acc = acc_scr[...].reshape(NSA_GROUP * tq, LANE)
        o_s = acc * (1.0 / jnp.maximum(pltpu.roll(acc, HEAD_DIM, axis=1), 1e-30))
        v_wx = jnp.where(keep_g, v_w, one_b)
        pieces = []
        for r in range(NSA_GROUP):
            h = NSA_GROUP * g + r
            sl = slice(r * tq, (r + 1) * tq)
            o_w = attend(qz_b[r], k_w, v_wx, bias_w)
            pieces.append(gates[:, 3 * h:3 * h + 1] * o_c[r] + gates[:, 3 * h + 1:3 * h + 2] * o_s[sl]
                          + gates[:, 3 * h + 2:3 * h + 3] * o_w)
        out_slabs.extend(_place_heads(pieces, g))
    for j in range(4):
        o_ref[:, j * LANE:(j + 1) * LANE] = out_slabs[j].astype(o_ref.dtype)


def _nsa_prompt(qb, small, kcp, vcp, kvb3, *, b, s, kt):
    tq = NSA_TQ
    nq = s // tq
    nbc = kcp.shape[1]
    nbs = s // SEL_BLOCK
    assert s >= WINDOW + tq and s % kt == 0 and nbc == 2 * nbs and nbs <= LANE and kt == SEL_KT
    return pl.pallas_call(
        functools.partial(_nsa_prompt_kernel, kt=kt, nbs=nbs),
        grid=(b, nq),
        in_specs=[pl.BlockSpec((tq, 512), lambda bi, i: (bi * nq + i, 0)),
                  pl.BlockSpec((tq, LANE), lambda bi, i: (bi * nq + i, 0)),
                  pl.BlockSpec((1, nbc, LANE), lambda bi, i: (bi, 0, 0)),
                  pl.BlockSpec((1, nbc, LANE), lambda bi, i: (bi, 0, 0)),
                  pl.BlockSpec((1, s, 768), lambda bi, i: (bi, 0, 0))],
        out_specs=pl.BlockSpec((tq, 512), lambda bi, i: (bi * nq + i, 0)),
        out_shape=jax.ShapeDtypeStruct((b * s, 512), BF16),
        scratch_shapes=[pltpu.VMEM((NSA_GROUP, tq, LANE), F32), pltpu.VMEM((NSA_GROUP, tq, LANE), F32),
                        pltpu.VMEM((max(1, nbs // HEAD_DIM), NSA_GROUP, tq, LANE), BF16),
                        pltpu.VMEM((2, NSA_GROUP, tq, kt), F32)],
        compiler_params=_params(("arbitrary", "arbitrary")),
        name="nsa_prompt",
    )(qb, small, kcp, vcp, kvb3)


def _nsa_sample_kernel(pt_ref, q_ref, small_ref, rows_ref, wnew_ref, wcache_ref, a_ref, w_ref, pair_ref, *rest,
                       n_pp, kt):
    page_refs = rest[:n_pp]
    o_ref = rest[n_pp]
    ksel, vsel, pooled, s_scr = rest[n_pp + 1:]
    pg = pl.program_id(1)
    npg = pooled.shape[0]
    past = ksel.shape[1]
    nblk = past // SEL_BLOCK
    a_t = a_ref[...]
    per_page = PAGE_SIZE // CMP_BLOCK
    t_i = lax.broadcasted_iota(jnp.int32, (PAGE_SIZE, per_page * n_pp), 0) // CMP_BLOCK
    n_i = lax.broadcasted_iota(jnp.int32, (PAGE_SIZE, per_page * n_pp), 1)
    acc_p = jnp.zeros((2 * KV_W, per_page * n_pp), F32)
    for k in range(n_pp):
        page = page_refs[k][0, 0]
        xa = page[0:2 * KV_W, :] * a_t
        seg = jnp.where(n_i == t_i + per_page * k, 1.0, 0.0).astype(BF16)
        hi = xa.astype(BF16)
        lo = (xa - hi.astype(F32)).astype(BF16)
        acc_p = acc_p + _dot(hi, seg) + _dot(lo, seg)
        c0 = pl.multiple_of((pg * n_pp + k) * PAGE_SIZE, PAGE_SIZE)
        ksel[:, pl.ds(c0, PAGE_SIZE)] = page[256:384, :].astype(BF16)
        vsel[:, pl.ds(c0, PAGE_SIZE)] = page[384:512, :].astype(BF16)
    pooled[pg] = acc_p

    @pl.when(pg == pl.num_programs(1) - 1)
    def _finish():
        nq = q_ref.shape[1]
        cmp_t = [_dot(w_ref[...], pooled[j].astype(BF16)) for j in range(npg)]
        kc = jnp.concatenate([c[0:KV_W] for c in cmp_t], axis=1).astype(BF16)
        vc = jnp.concatenate([c[KV_W:2 * KV_W] for c in cmp_t], axis=1).astype(BF16)
        qf = q_ref[0]
        gates = small_ref[0]
        qg = [_group_queries(qf, g) for g in range(NSA_KV_HEADS)]
        zpad = jnp.zeros((32 - NSA_KV_HEADS * nq, LANE), F32)
        blocks = []
        for r in range(NSA_GROUP):
            blocks += [qg[0][r], qg[1][r], zpad]
        qz = jnp.concatenate(blocks, axis=0).astype(BF16)
        rows_n = qz.shape[0]
        qidx_c = lax.broadcasted_iota(jnp.int32, (rows_n, 1), 0) % nq
        s_c = _dot(qz, kc)
        e_c = jnp.exp2(s_c - jnp.max(s_c, axis=-1, keepdims=True))
        p_c = e_c * (1.0 / jnp.sum(e_c, axis=-1, keepdims=True))
        o_c = _dot_nt(p_c.astype(BF16), vc)
        p_sum = p_c[0:32] + p_c[32:64] + p_c[64:96] + p_c[96:128]
        imp = _dot01_r(p_sum, pair_ref[...])
        jcol = lax.broadcasted_iota(jnp.int32, imp.shape, 1)
        score = imp + jnp.where(jcol == 0, FORCE, 0.0)
        sel32 = _topk_select_lanes(score, jcol.astype(F32), min(SEL_TOPN - 1, nblk))
        sel_b = jnp.concatenate([sel32] * NSA_GROUP, axis=0).astype(BF16)

        def scores(t, slot):
            k_t = ksel[:, pl.ds(pl.multiple_of(t * kt, kt), kt)]
            s_scr[slot] = _dot(qz, k_t) + jnp.where(_expand_tile(sel_b, t, kt) > 0.5, 0.0, NEG)

        def consume(t, slot, carry):
            m_old, l_old, acc = carry
            v_t = vsel[:, pl.ds(pl.multiple_of(t * kt, kt), kt)]
            s = s_scr[slot]
            m_new = jnp.maximum(m_old, jnp.max(s, axis=-1, keepdims=True))
            p = jnp.exp2(s - m_new)
            alpha = jnp.exp2(m_old - m_new)
            l_new = alpha * l_old + jnp.sum(p, axis=-1, keepdims=True)
            return m_new, l_new, alpha * acc + _dot_nt(p.astype(BF16), v_t)

        def pair_body(tp, carry):
            scores(2 * tp + 1, 1)
            carry = consume(2 * tp, 0, carry)
            scores(2 * tp + 2, 0)
            return consume(2 * tp + 1, 1, carry)

        n_t = past // kt
        scores(0, 0)
        init = (jnp.full((rows_n, 1), NEG, F32), jnp.zeros((rows_n, 1), F32), jnp.zeros((rows_n, LANE), F32))
        carry = lax.fori_loop(0, n_t // 2 - 1, pair_body, init)
        scores(n_t - 1, 1)
        carry = consume(n_t - 2, 0, carry)
        m_old, l_old, acc = consume(n_t - 1, 1, carry)
        newr = rows_ref[0]
        zk = jnp.zeros((LANE - nq, LANE), F32)
        k_n = jnp.concatenate([newr[:, 256:384], zk], axis=0).astype(BF16)
        v_n = jnp.concatenate([newr[:, 384:512], zk], axis=0).astype(BF16)
        ccol = lax.broadcasted_iota(jnp.int32, (1, LANE), 1)
        allow_n = (ccol <= qidx_c) & (ccol < nq)
        sm = jnp.where(allow_n, _dot_nt(qz, k_n), NEG)
        m_new = jnp.maximum(m_old, jnp.max(sm, axis=-1, keepdims=True))
        p = jnp.where(allow_n, jnp.exp2(sm - m_new), 0.0)
        alpha = jnp.exp2(m_old - m_new)
        l_new = alpha * l_old + jnp.sum(p, axis=-1, keepdims=True)
        acc = alpha * acc + _dot(p.astype(BF16), v_n)
        o_s = acc * (1.0 / jnp.maximum(l_new, 1e-30))
        wc = wcache_ref[0, 0]
        wn = wnew_ref[0]
        n_keep = wc.shape[1]
        k_wn = jnp.concatenate([wn[:, 0:KV_W], zk], axis=0).astype(BF16)
        v_wn = jnp.concatenate([wn[:, KV_W:2 * KV_W], zk], axis=0).astype(BF16)
        s_w = jnp.concatenate([_dot(qz, wc[0:KV_W].astype(BF16)), _dot_nt(qz, k_wn)], axis=1)
        wcol = lax.broadcasted_iota(jnp.int32, (1, n_keep + LANE), 1)
        allow_w = (((wcol < n_keep) & (wcol + (WINDOW - n_keep) > qidx_c))
                   | ((wcol >= n_keep) & (wcol - n_keep <= qidx_c) & (wcol - n_keep < nq)))
        p_w = _softmax_rows(s_w, allow_w).astype(BF16)
        o_w = _dot_nt(p_w[:, 0:n_keep], wc[KV_W:2 * KV_W].astype(BF16)) + _dot(p_w[:, n_keep:], v_wn)
        slabs = []
        for g in range(NSA_KV_HEADS):
            pieces = []
            for r in range(NSA_GROUP):
                h = NSA_GROUP * g + r
                sl = slice(r * 32 + g * nq, r * 32 + (g + 1) * nq)
                pieces.append(gates[:, 3 * h:3 * h + 1] * o_c[sl] + gates[:, 3 * h + 1:3 * h + 2] * o_s[sl]
                              + gates[:, 3 * h + 2:3 * h + 3] * o_w[sl])
            slabs.extend(_place_heads(pieces, g))
        for j in range(4):
            o_ref[0, :, j * LANE:(j + 1) * LANE] = slabs[j]


def _topk_select_lanes(v, jcol, n_rounds):
    big = float(v.shape[1] + 1)

    def body(_, carry):
        v, sel = carry
        m = jnp.max(v, axis=1, keepdims=True)
        first = jnp.min(jnp.where(v == m, jcol, big), axis=1, keepdims=True)
        pick = jcol == first
        sel = jnp.where(pick & (m > 0.5 * NEG), 1.0, sel)
        v = jnp.where(pick, -jnp.inf, v)
        return v, sel

    _, sel = lax.fori_loop(0, n_rounds, body, (v, jnp.zeros_like(v)))
    return sel


def _nsa_sample(page_table, q3, small3, rows3, wnew3, wcache_t, a_t, w_bd_t, pair, cache_t, *, layer, kt):
    b, n_pages = page_table.shape
    nq = q3.shape[1]
    past = n_pages * PAGE_SIZE
    n_pp = PAGES_PER_STEP
    assert n_pages % n_pp == 0 and past % kt == 0 and nq * NSA_KV_HEADS <= 32 and nq < CMP_BLOCK
    assert n_pp * PAGE_SIZE // CMP_BLOCK == LANE and (past // kt) % 2 == 0
    npg = n_pages // n_pp
    per = lambda w: pl.BlockSpec((1, nq, w), lambda bi, pg, pt: (bi, 0, 0))
    page_specs = [pl.BlockSpec((1, 1, 512, PAGE_SIZE), lambda bi, pg, pt, k=k: (layer, pt[bi, pg * n_pp + k], 0, 0))
                  for k in range(n_pp)]
    const = lambda shape: pl.BlockSpec(shape, lambda bi, pg, pt: (0,) * len(shape))
    grid_spec = pltpu.PrefetchScalarGridSpec(
        num_scalar_prefetch=1,
        grid=(b, npg),
        in_specs=[per(512), per(LANE), per(512), per(256),
                  pl.BlockSpec((1, 1, 256, wcache_t.shape[3]), lambda bi, pg, pt: (layer, bi, 0, 0)),
                  const(a_t.shape), const(w_bd_t.shape), const(pair.shape)] + page_specs,
        out_specs=per(512),
        scratch_shapes=[pltpu.VMEM((LANE, past), BF16), pltpu.VMEM((LANE, past), BF16),
                        pltpu.VMEM((npg, 2 * KV_W, LANE), F32), pltpu.VMEM((2, LANE, kt), F32)],
    )
    return pl.pallas_call(
        functools.partial(_nsa_sample_kernel, n_pp=n_pp, kt=kt),
        grid_spec=grid_spec,
        out_shape=jax.ShapeDtypeStruct((b, nq, 512), F32),
        compiler_params=_params(("arbitrary", "arbitrary")),
        name="nsa_sample",
    )(page_table, q3, small3, rows3, wnew3, wcache_t, a_t, w_bd_t, pair, *([cache_t] * n_pp))


def _mlstm_kernel(mqk_ref, mv_ref, mo_ref, small_ref, smallt_ref, hist_ref, ext0_ref, m0_ref, cw_ref, cb_ref,
                  wq_ref, wk_ref, ng_ref, hm_ref, ext_ref, m_ref, ext_scr, m_scr, carry_scr, *, L):
    c = pl.program_id(1)

    @pl.when(c == 0)
    def _init():
        ext_scr[...] = ext0_ref[0]
        m_scr[...] = m0_ref[0]
        carry_scr[...] = hist_ref[0]

    x = mqk_ref[0]
    tl = x.shape[0]
    full = jnp.concatenate([carry_scr[...], x], axis=0)
    y = cb_ref[...]
    for j in range(MLSTM_CONV):
        y = y + cw_ref[j:j + 1, :] * full[8 - (MLSTM_CONV - 1) + j:8 - (MLSTM_CONV - 1) + j + tl]
    carry_scr[...] = full[tl:tl + 8]
    qk = _silu(y).astype(BF16)
    q_all = _dot(qk, wq_ref[...])
    k_all = _dot(qk, wk_ref[...]) * (HEAD_DIM ** -0.5)
    lane = lax.broadcasted_iota(jnp.int32, (L, LANE), 1)
    lane_row = lax.broadcasted_iota(jnp.int32, (1, LANE), 1)
    t_i = lax.broadcasted_iota(jnp.int32, (L, L), 0)
    s_i = lax.broadcasted_iota(jnp.int32, (L, L), 1)
    causal = s_i <= t_i
    tril_b = jnp.where(causal, 1.0, 0.0).astype(BF16)
    triu_b = jnp.where(t_i <= s_i, 1.0, 0.0).astype(BF16)
    eye_b = jnp.where(lax.broadcasted_iota(jnp.int32, (LANE, LANE), 0) == lax.broadcasted_iota(jnp.int32, (LANE, LANE), 1),
                      1.0, 0.0).astype(BF16)
    for u in range(tl // L):
        sl = slice(u * L, (u + 1) * L)
        sm = small_ref[0, sl, :]
        smt = smallt_ref[0, :, sl]
        bcols = _dot01(tril_b, sm)
        brows = _dot01_r(smt, triu_b)
        mrow = m_scr[...]
        for j in range(2):
            cs = slice(j * LANE, (j + 1) * LANE)
            q_slab = q_all[sl, cs]
            k_b = k_all[sl, cs].astype(BF16)
            v_slab = mv_ref[0, sl, cs].astype(F32)
            k_t = _dot_nt(eye_b, k_b).astype(BF16)
            outs = []
            for half in range(2):
                h = 2 * j + half
                keep = (lane >= HEAD_DIM * half) & (lane < HEAD_DIM * (half + 1))
                ones_lane = HEAD_DIM if half == 0 else 0
                qm = jnp.where(keep, q_slab, 0.0).astype(BF16)
                v_ext = jnp.where(keep, v_slab, jnp.where(lane == ones_lane, 1.0, 0.0))
                b_col = bcols[:, 28 + h:29 + h]
                li_col = sm[:, 24 + h:25 + h]
                b_row = brows[4 + h:5 + h, :]
                li_row = smt[h:h + 1, :]
                m_h = mrow[:, h:h + 1]
                d = jnp.where(causal, b_col - b_row + li_row, NEG)
                inter = b_col + m_h
                mt = jnp.maximum(inter, jnp.max(d, axis=-1, keepdims=True))
                a = _dot_nt(qm, k_b) * jnp.exp(d - mt)
                wi = jnp.exp(inter - mt)
                cts = ext_scr[h]
                num = wi * _dot(qm, cts.astype(BF16)) + _dot(a.astype(BF16), v_ext.astype(BF16))
                den = jnp.maximum(jnp.abs(num[:, ones_lane:ones_lane + 1]), jnp.exp(-mt))
                outs.append(num / den)
                bl = b_col[L - 1:L, :]
                g_col = bl - b_col + li_col
                m_new = jnp.maximum(bl + m_h, jnp.max(g_col, axis=0, keepdims=True))
                w_col = jnp.exp(g_col - m_new)
                dec = jnp.exp(bl + m_h - m_new)
                ext_scr[h] = dec * cts + _dot(k_t, (w_col * v_ext).astype(BF16))
                mrow = jnp.where(lane_row == h, m_new, mrow)
            hs = jnp.where(lane < HEAD_DIM, outs[0], outs[1])
            sq = hs * hs
            s0 = jnp.sum(jnp.where(lane < HEAD_DIM, sq, 0.0), axis=-1, keepdims=True)
            s1 = jnp.sum(jnp.where(lane < HEAD_DIM, 0.0, sq), axis=-1, keepdims=True)
            rs = jnp.where(lane < HEAD_DIM, lax.rsqrt(s0 * (1.0 / HEAD_DIM) + EPS), lax.rsqrt(s1 * (1.0 / HEAD_DIM) + EPS))
            hm_ref[0, sl, cs] = (mo_ref[0, sl, cs] * (hs * rs * ng_ref[:, cs])).astype(BF16)
        m_scr[...] = mrow

    @pl.when(c == pl.num_programs(1) - 1)
    def _out():
        ext_ref[0] = ext_scr[...]
        m_ref[0] = m_scr[...]


def _mlstm(mqk3, mv3, mo3, small3, smallt3, hist, ext0, m0, cw, cb, wq, wk, ng, *, tl):
    b, s, _ = mqk3.shape
    L = MLSTM_L
    assert s % tl == 0 and tl % L == 0
    blk = lambda w: pl.BlockSpec((1, tl, w), lambda bi, c: (bi, c, 0))
    st = lambda shape: pl.BlockSpec((1,) + shape, lambda bi, c: (bi,) + (0,) * len(shape))
    const = lambda shape: pl.BlockSpec(shape, lambda bi, c: (0,) * len(shape))
    return pl.pallas_call(
        functools.partial(_mlstm_kernel, L=L),
        grid=(b, s // tl),
        in_specs=[blk(256), blk(256), blk(256), blk(LANE),
                  pl.BlockSpec((1, 8, tl), lambda bi, c: (bi, 0, c)),
                  st((8, 256)), st((MLSTM_HEADS, LANE, LANE)), st((1, LANE)),
                  const((MLSTM_CONV, 256)), const((1, 256)), const((256, 256)), const((256, 256)), const((1, 256))],
        out_specs=[blk(256), st((MLSTM_HEADS, LANE, LANE)), st((1, LANE))],
        out_shape=(jax.ShapeDtypeStruct((b, s, 256), BF16),
                   jax.ShapeDtypeStruct((b, MLSTM_HEADS, LANE, LANE), F32),
                   jax.ShapeDtypeStruct((b, 1, LANE), F32)),
        scratch_shapes=[pltpu.VMEM((MLSTM_HEADS, LANE, LANE), F32), pltpu.VMEM((1, LANE), F32),
                        pltpu.VMEM((8, 256), F32)],
        compiler_params=_params(("arbitrary", "arbitrary")),
        name="mlstm",
    )(mqk3, mv3, mo3, small3, smallt3, hist, ext0, m0, cw, cb, wq, wk, ng)


FF_CHUNK = 1408


def _post_kernel(*refs, final, segmented, tiles_per_batch, seg_len):
    if segmented:
        (x_ref, onsa_ref, hm_ref, ocm_ref, mod_ref, wout_ref, g2_ref, wup_ref, cw_ref, cb_ref, wdn_ref, fg_ref,
         h1_ref, h2_ref, y_ref, fst_ref) = refs
        carry_scr = None
    else:
        (x_ref, onsa_ref, hm_ref, ocm_ref, mod_ref, wout_ref, g2_ref, wup_ref, cw_ref, cb_ref, wdn_ref, fg_ref,
         y_ref, fst_ref, carry_scr) = refs
    x = x_ref[...]
    tm = x.shape[0]
    mix = (_dot(onsa_ref[...].astype(BF16), wout_ref[0:512, :]) + _dot(hm_ref[...], wout_ref[512:768, :])
           + _dot(ocm_ref[...], wout_ref[768:1024, :]))
    mod = mod_ref[0]
    x1 = x + mod[:, 2 * D_MODEL:3 * D_MODEL] * mix
    h2 = x1 * lax.rsqrt(jnp.mean(x1 * x1, axis=-1, keepdims=True) + EPS) * g2_ref[...]
    hb = (h2 * (1.0 + mod[:, 4 * D_MODEL:5 * D_MODEL]) + mod[:, 3 * D_MODEL:4 * D_MODEL]).astype(BF16)
    if not segmented:
        @pl.when(pl.program_id(0) % tiles_per_batch == 0)
        def _reset():
            carry_scr[...] = jnp.zeros(carry_scr.shape, F32)
    else:
        tmod = lax.broadcasted_iota(jnp.int32, (tm, 1), 0) % seg_len
    acc = jnp.zeros((tm, D_MODEL), F32)
    for c0 in range(0, D_FF, FF_CHUNK):
        parts = []
        for part in range(2):
            cs = slice(part * D_FF + c0, part * D_FF + c0 + FF_CHUNK)
            up = _dot(hb, wup_ref[:, cs])
            prev = jnp.zeros((8, FF_CHUNK), F32) if segmented else carry_scr[:, cs]
            full = jnp.concatenate([prev, up], axis=0)
            s1 = full[7:7 + tm]
            s2 = full[6:6 + tm]
            if segmented:
                s1 = jnp.where(tmod >= 1, s1, h1_ref[:, cs])
                s2 = jnp.where(tmod >= 2, s2, h2_ref[:, cs])
                fst_ref[:, cs] = up
            else:
                carry_scr[:, cs] = up[tm - 8:tm]
                fst_ref[0, :, cs] = up[tm - 8:tm]
            parts.append(cb_ref[:, cs] + cw_ref[0:1, cs] * s2 + cw_ref[1:2, cs] * s1 + cw_ref[2:3, cs] * up)
        act = (_silu(parts[0]) * parts[1]).astype(BF16)
        acc = acc + _dot(act, wdn_ref[c0:c0 + FF_CHUNK, :])
    x2 = x1 + mod[:, 5 * D_MODEL:6 * D_MODEL] * acc
    if final:
        x2 = x2 * lax.rsqrt(jnp.mean(x2 * x2, axis=-1, keepdims=True) + EPS) * fg_ref[...]
    y_ref[...] = x2


def _post(x2d, onsa, hm, ocm, mod, wout, g2, wup, cw, cb, wdn, fg, h1=None, h2=None, *, tm, tiles_per_batch, final,
          seg_len=0):
    t_total = x2d.shape[0]
    n_tiles = t_total // tm
    segmented = h1 is not None
    r = mod.shape[1]
    row = lambda w: pl.BlockSpec((tm, w), lambda i: (i, 0))
    in_specs = [row(D_MODEL), row(512), row(256), row(256),
                pl.BlockSpec((1, r, 6 * D_MODEL), lambda i: (i // tiles_per_batch, 0, 0)),
                _const_spec((D_MODEL, D_MODEL)), _const_spec((1, D_MODEL)), _const_spec((D_MODEL, 2 * D_FF)),
                _const_spec((FFN_CONV, 2 * D_FF)), _const_spec((1, 2 * D_FF)), _const_spec((D_FF, D_MODEL)),
                _const_spec((1, D_MODEL))]
    args = [x2d, onsa, hm, ocm, mod, wout, g2, wup, cw, cb, wdn, fg]
    if segmented:
        in_specs += [row(2 * D_FF), row(2 * D_FF)]
        args += [h1, h2]
        out_specs = [row(D_MODEL), row(2 * D_FF)]
        out_shape = (jax.ShapeDtypeStruct((t_total, D_MODEL), F32), jax.ShapeDtypeStruct((t_total, 2 * D_FF), F32))
        scratch = []
    else:
        nb = n_tiles // tiles_per_batch
        out_specs = [row(D_MODEL), pl.BlockSpec((1, 8, 2 * D_FF), lambda i: (i // tiles_per_batch, 0, 0))]
        out_shape = (jax.ShapeDtypeStruct((t_total, D_MODEL), F32), jax.ShapeDtypeStruct((nb, 8, 2 * D_FF), F32))
        scratch = [pltpu.VMEM((8, 2 * D_FF), F32)]
    return pl.pallas_call(
        functools.partial(_post_kernel, final=final, segmented=segmented, tiles_per_batch=tiles_per_batch,
                          seg_len=seg_len),
        grid=(n_tiles,),
        in_specs=in_specs, out_specs=out_specs, out_shape=out_shape, scratch_shapes=scratch,
        compiler_params=_params(("arbitrary",)),
        name="post",
    )(*args)


def _col_perm():
    o = IN_OFF
    cols = list(range(o[0], o[4])) + list(range(o[5], o[8])) + list(range(o[10], o[12]))
    small = list(range(o[4], o[5])) + list(range(o[8], o[10]))
    small += [IN_W] * (LANE - len(small))
    return np.array(cols + small, dtype=np.int32)


def _swap_perm():
    cols = []
    for start, heads in ((0, NSA_HEADS), (512, NSA_KV_HEADS), (768, NSA_KV_HEADS), (1024, NSA_KV_HEADS)):
        for hh in range(heads):
            base = start + HEAD_DIM * hh
            cols += list(range(base + 32, base + 64)) + list(range(base, base + 32))
    return np.array(cols, dtype=np.int32)


def _block_diag(mats):
    n = len(mats)
    rows = []
    for i, m in enumerate(mats):
        rows.append(jnp.concatenate([m if j == i else jnp.zeros((m.shape[0], mats[j].shape[1]), m.dtype)
                                     for j in range(n)], axis=1))
    return jnp.concatenate(rows, axis=0)


def _rope_tables(pos):
    half = HEAD_DIM // 2
    inv = jnp.power(ROPE_THETA, -jnp.arange(half, dtype=F32) / half)
    ang = pos.astype(F32)[:, None] * inv[None, :]
    cos = jnp.cos(ang)
    sin = jnp.sin(ang)
    return jnp.tile(cos, (1, 4)), jnp.concatenate([-sin, sin, -sin, sin], axis=1)


def _layer_weights(l, w):
    win_z = jnp.concatenate([w['w_in'][l], jnp.zeros((D_MODEL, 1), F32)], axis=1)
    bin_z = jnp.concatenate([w['b_in'][l], jnp.zeros((1,), F32)])
    perm, swp = _col_perm(), _swap_perm()
    cmp_a = w['cmp_a'][l]
    cmp_w = w['cmp_w'][l]
    c_ws = jnp.tril(w['c_ws'][l])
    return dict(
        wp=win_z[:, perm].astype(BF16), bp=bin_z[perm][None, :],
        wsw=w['w_in'][l][:, swp].astype(BF16), bsw=w['b_in'][l][swp][None, :],
        g1=w['norm1_g'][l][None, :], g2=w['norm2_g'][l][None, :],
        cg=w['c_norm_g'][l][None, :], c_ws=c_ws, c_bs=w['c_bs'][l],
        a_tile=jnp.concatenate([cmp_a[0].reshape(CMP_BLOCK, KV_W), cmp_a[1].reshape(CMP_BLOCK, KV_W)], axis=1),
        w_bd=_block_diag([cmp_w[0, 0], cmp_w[0, 1], cmp_w[1, 0], cmp_w[1, 1]]).astype(BF16),
        m_cw=w['m_conv_w'][l], m_cb=w['m_conv_b'][l][None, :],
        wq=_block_diag([w['m_wq'][l][h] for h in range(MLSTM_HEADS)]).astype(BF16),
        wk=_block_diag([w['m_wk'][l][h] for h in range(MLSTM_HEADS)]).astype(BF16),
        ng=w['m_norm_g'][l].reshape(1, MLSTM_W),
        wout=w['w_out'][l].astype(BF16), wup=w['w_up'][l].astype(BF16), wdn=w['w_down'][l].astype(BF16),
        f_cw=w['f_conv_w'][l], f_cb=w['f_conv_b'][l][None, :],
    )


def _mlstm_state_in(c0, n0, m0):
    b = c0.shape[0]
    tiles = []
    for h in range(MLSTM_HEADS):
        r0 = HEAD_DIM * (h % 2)
        ol = HEAD_DIM if h % 2 == 0 else 0
        t = jnp.zeros((b, LANE, LANE), F32)
        t = t.at[:, r0:r0 + HEAD_DIM, r0:r0 + HEAD_DIM].set(jnp.swapaxes(c0[:, h], 1, 2))
        t = t.at[:, r0:r0 + HEAD_DIM, ol].set(n0[:, h])
        tiles.append(t)
    m = jnp.zeros((b, 1, LANE), F32).at[:, 0, 0:MLSTM_HEADS].set(m0)
    return jnp.stack(tiles, axis=1), m


def _mlstm_state_out(ext, m):
    cs, ns = [], []
    for h in range(MLSTM_HEADS):
        r0 = HEAD_DIM * (h % 2)
        ol = HEAD_DIM if h % 2 == 0 else 0
        cs.append(jnp.swapaxes(ext[:, h, r0:r0 + HEAD_DIM, r0:r0 + HEAD_DIM], 1, 2))
        ns.append(ext[:, h, r0:r0 + HEAD_DIM, ol])
    return jnp.stack(cs, axis=1), jnp.stack(ns, axis=1), m[:, 0, 0:MLSTM_HEADS]


def _layer_prompt(x2d, mod, lw, fg, *, b, s, final, tm=256):
    assert tm % SEL_KT == 0
    tiles_per_batch = s // tm
    cos_t, sin_t = _rope_tables(jnp.arange(s, dtype=jnp.int32))
    wc = jnp.concatenate([lw['c_ws'][g] for g in range(CMLP_GROUPS)], axis=1).astype(BF16)
    bc = jnp.repeat(lw['c_bs'].T, HEAD_DIM, axis=1)
    rows_t, win, qb, kvb, small, mqk, mv, mo, ocm, _, cmp_rows = _inproj(
        x2d, mod, lw['g1'], lw['wp'], lw['bp'], lw['wsw'], lw['bsw'], cos_t, sin_t, wc, bc, lw['cg'],
        tm=tm, tiles_per_batch=tiles_per_batch, tc=CMLP_CHUNK, q_dtype=BF16, rows_feature_major=True)
    nb = s // CMP_BLOCK
    comp = _compress(cmp_rows.reshape(b, s, 2 * KV_W), lw['a_tile'], lw['w_bd'], rows_per_step=min(s, 1024))
    comp = comp.reshape(b, nb // 2, 2, 2 * KV_W).transpose(0, 2, 1, 3).reshape(b, nb, 2 * KV_W)
    onsa = _nsa_prompt(qb, small, comp[..., 0:KV_W], comp[..., KV_W:], kvb.reshape(b, s, 768), b=b, s=s, kt=SEL_KT)
    zeros = lambda *sh: jnp.zeros(sh, F32)
    ext0, m0 = _mlstm_state_in(zeros(b, MLSTM_HEADS, HEAD_DIM, HEAD_DIM), zeros(b, MLSTM_HEADS, HEAD_DIM),
                               zeros(b, MLSTM_HEADS))
    small3 = small.reshape(b, s, LANE)
    smallt = jnp.swapaxes(small3[:, :, 24:32], 1, 2)
    hm, ext, mout = _mlstm(mqk.reshape(b, s, 256), mv.reshape(b, s, 256), mo.reshape(b, s, 256), small3, smallt,
                           zeros(b, 8, 256), ext0, m0, lw['m_cw'], lw['m_cb'], lw['wq'], lw['wk'], lw['ng'], tl=tm)
    c1, n1, m1 = _mlstm_state_out(ext, mout)
    y, fst = _post(x2d, onsa, hm.reshape(b * s, 256), ocm, mod, lw['wout'], lw['g2'], lw['wup'], lw['f_cw'],
                   lw['f_cb'], lw['wdn'], fg, tm=tm, tiles_per_batch=tiles_per_batch, final=final)
    n_keep = min(WINDOW, s)
    kv_rows = jnp.transpose(rows_t.reshape(b, 4, NSA_KV_HEADS, HEAD_DIM, s), (0, 4, 1, 2, 3))
    state = (kv_rows, win.reshape(b, s, 2, NSA_KV_HEADS, HEAD_DIM)[:, s - n_keep:],
             c1, n1, m1, mqk.reshape(b, s, 256)[:, s - (MLSTM_CONV - 1):], fst[:, 8 - (FFN_CONV - 1):])
    return y, state


def _layer_sample(x2d, mod, lw, fg, cache_t, page_table, wcache_t, mc, mn, mm, mconv, fconv, *, layer, b, s, final,
                  kt=512):
    t = b * s
    past = page_table.shape[1] * PAGE_SIZE
    pos = past + jnp.arange(s, dtype=jnp.int32)
    cos1, sin1 = _rope_tables(pos)
    cos_t, sin_t = jnp.tile(cos1, (b, 1)), jnp.tile(sin1, (b, 1))
    eye_b = jnp.eye(b, dtype=F32)
    wc = jnp.concatenate([jnp.kron(eye_b, lw['c_ws'][g][:s, :s]) for g in range(CMLP_GROUPS)], axis=1).astype(BF16)
    bc = jnp.tile(jnp.repeat(lw['c_bs'].T[:s], HEAD_DIM, axis=1), (b, 1))
    rows, win, qf, _, small, mqk, mv, mo, ocm, cvn = _inproj(
        x2d, mod, lw['g1'], lw['wp'], lw['bp'], lw['wsw'], lw['bsw'], cos_t, sin_t, wc, bc, lw['cg'],
        tm=t, tiles_per_batch=1, tc=t, q_dtype=F32, rows_feature_major=False)
    nbc = past // CMP_BLOCK
    pair = jnp.repeat(jnp.eye(nbc // 2, dtype=BF16), 2, axis=0)
    n_keep = wcache_t.shape[3]
    wnew3 = win.reshape(b, s, 256)
    a_t = jnp.tile(lw['a_tile'].T, (1, PAGE_SIZE // CMP_BLOCK))
    onsa = _nsa_sample(page_table, qf.reshape(b, s, 512), small.reshape(b, s, LANE), rows.reshape(b, s, 512), wnew3,
                       wcache_t, a_t, lw['w_bd'].T, pair, cache_t, layer=layer, kt=kt).reshape(t, 512)
    L = MLSTM_L
    pad3 = lambda a, fill=0.0: jnp.pad(a.reshape(b, s, -1), ((0, 0), (0, L - s), (0, 0)), constant_values=fill)
    small3 = small.reshape(b, s, LANE)
    lane = jnp.arange(LANE)
    small_p = jnp.where((lane >= 24) & (lane < 28), pad3(small3, NEG), pad3(small3))
    smallt = jnp.swapaxes(small_p[:, :, 24:32], 1, 2)
    hist = jnp.pad(mconv, ((0, 0), (8 - (MLSTM_CONV - 1), 0), (0, 0)))
    ext0, m0 = _mlstm_state_in(mc, mn, mm)
    hm, ext, mout = _mlstm(pad3(mqk), pad3(mv), pad3(mo), small_p, smallt, hist, ext0, m0, lw['m_cw'], lw['m_cb'],
                           lw['wq'], lw['wk'], lw['ng'], tl=L)
    c1, n1, m1 = _mlstm_state_out(ext, mout)
    hm = hm[:, :s].reshape(t, 256)
    h1 = jnp.zeros((b, s, 2 * D_FF), F32).at[:, 0].set(fconv[:, 1]).reshape(t, 2 * D_FF)
    h2 = jnp.zeros((b, s, 2 * D_FF), F32).at[:, 0].set(fconv[:, 0]).at[:, 1].set(fconv[:, 1]).reshape(t, 2 * D_FF)
    y, up = _post(x2d, onsa, hm, ocm, mod, lw['wout'], lw['g2'], lw['wup'], lw['f_cw'], lw['f_cb'], lw['wdn'], fg,
                  h1, h2, tm=t, tiles_per_batch=1, final=final, seg_len=s)
    new_win_t = jnp.concatenate([wcache_t[layer][:, :, s:], jnp.swapaxes(wnew3, 1, 2)], axis=2)
    new_win = jnp.transpose(new_win_t.reshape(b, 2, NSA_KV_HEADS, HEAD_DIM, n_keep), (0, 4, 1, 2, 3))
    state = (rows.reshape(b, s, 4, NSA_KV_HEADS, HEAD_DIM), new_win, c1, n1, m1,
             mqk.reshape(b, s, 256)[:, s - (MLSTM_CONV - 1):], cvn.reshape(b, s, CMLP_W),
             up.reshape(b, s, 2 * D_FF)[:, s - (FFN_CONV - 1):])
    return y, state


def kernel(x_prompt, x_sample, cache_nsa_kv, page_table, cache_win_kv, state_mlstm_C, state_mlstm_n, state_mlstm_m, state_mlstm_conv, state_ffn_conv, c_prompt, c_sample, ada_w, ada_b, norm1_g, norm2_g, w_in, b_in, cmp_a, cmp_w, m_conv_w, m_conv_b, m_wq, m_wk, m_norm_g, c_norm_g, c_ws, c_bs, w_out, w_up, f_conv_w, f_conv_b, w_down, final_norm_g):
    bp, sp, _ = x_prompt.shape
    bs, ss, _ = x_sample.shape
    depth = ada_w.shape[0]
    w = dict(w_in=w_in, b_in=b_in, norm1_g=norm1_g, norm2_g=norm2_g, cmp_a=cmp_a, cmp_w=cmp_w, m_conv_w=m_conv_w,
             m_conv_b=m_conv_b, m_wq=m_wq, m_wk=m_wk, m_norm_g=m_norm_g, c_norm_g=c_norm_g, c_ws=c_ws, c_bs=c_bs,
             w_out=w_out, w_up=w_up, f_conv_w=f_conv_w, f_conv_b=f_conv_b, w_down=w_down)
    n_c = bp + bs
    n_pad = -(-n_c // 8) * 8
    c_all = jnp.pad(jnp.concatenate([c_prompt, c_sample], axis=0), ((0, n_pad - n_c), (0, 0)))
    mod_all = _ada(c_all, ada_w, ada_b)
    fg = final_norm_g[None, :]
    xp = x_prompt.reshape(bp * sp, D_MODEL)
    xs = x_sample.reshape(bs * ss, D_MODEL)
    n_pool = cache_nsa_kv.shape[1]
    n_keep = cache_win_kv.shape[2]
    cache_t = jnp.transpose(cache_nsa_kv, (0, 1, 3, 4, 5, 2)).reshape(depth, n_pool, 512, PAGE_SIZE)
    wcache_t = jnp.transpose(cache_win_kv, (0, 1, 3, 4, 5, 2)).reshape(depth, bs, 256, n_keep)
    st_p, st_s = [], []
    for l in range(depth):
        lw = _layer_weights(l, w)
        final = l == depth - 1
        mod_p = mod_all[l, 0:bp][:, None, :]
        mod_s = jnp.repeat(mod_all[l, bp:bp + bs], ss, axis=0)[None]
        xp, sp_l = _layer_prompt(xp, mod_p, lw, fg, b=bp, s=sp, final=final)
        xs, ss_l = _layer_sample(xs, mod_s, lw, fg, cache_t, page_table, wcache_t, state_mlstm_C[l],
                                 state_mlstm_n[l], state_mlstm_m[l], state_mlstm_conv[l], state_ffn_conv[l],
                                 layer=l, b=bs, s=ss, final=final)
        st_p.append(sp_l)
        st_s.append(ss_l)
    sp_st = [jnp.stack(t) for t in zip(*st_p)]
    ss_st = [jnp.stack(t) for t in zip(*st_s)]
    y_prompt = xp.reshape(bp, sp, D_MODEL)
    y_sample = xs.reshape(bs, ss, D_MODEL)
    return (y_prompt, y_sample, sp_st[0], ss_st[0], sp_st[1], ss_st[1],
            sp_st[2], sp_st[3], sp_st[4], sp_st[5],
            ss_st[2], ss_st[3], ss_st[4], ss_st[5],
            ss_st[6], sp_st[6], ss_st[7])
```

```python
import functools

import numpy as np
import jax
import jax.numpy as jnp
from jax import lax
from jax.experimental import pallas as pl
from jax.experimental.pallas import tpu as pltpu

F32 = jnp.float32
BF16 = jnp.bfloat16

D_MODEL = 1024
HEAD_DIM = 64
NSA_HEADS = 8
NSA_KV_HEADS = 2
NSA_GROUP = 4
CMP_BLOCK = 32
SEL_BLOCK = 64
SEL_TOPN = 16
WINDOW = 512
ROPE_THETA = 10000.0
MLSTM_HEADS = 4
MLSTM_CONV = 4
CMLP_GROUPS = 4
CMLP_CHUNK = 128
D_FF = 2816
FFN_CONV = 3
PAGE_SIZE = 128
NSA_W = 512
KV_W = 128
MLSTM_W = 256
CMLP_W = 256
IN_SIZES = (NSA_W, 2 * KV_W, 2 * KV_W, 2 * KV_W, 3 * NSA_HEADS, MLSTM_W, MLSTM_W, MLSTM_W, MLSTM_HEADS,
            MLSTM_HEADS, CMLP_W, CMLP_W)
IN_OFF = tuple(int(v) for v in np.cumsum((0,) + IN_SIZES))
IN_W = IN_OFF[-1]
EPS = 1e-6
NEG = -1e30
FORCE = 1e4
LANE = 128
NP_COLS = 2688
NSW_COLS = 896
VMEM_LIMIT = 56 * 1024 * 1024
PAGES_PER_STEP = 32
NSA_TQ = 256
SEL_KT = 256
Q_SCALE = HEAD_DIM ** -0.5 * 1.4426950408889634
MLSTM_L = 128


def _dot(a, b):
    return jnp.dot(a, b, preferred_element_type=F32)


def _dot_nt(a, b):
    return lax.dot_general(a, b, (((1,), (1,)), ((), ())), preferred_element_type=F32)


def _dot01(m01, x):
    hi = x.astype(BF16)
    r1 = x - hi.astype(F32)
    mid = r1.astype(BF16)
    lo = (r1 - mid.astype(F32)).astype(BF16)
    return _dot(m01, hi) + _dot(m01, mid) + _dot(m01, lo)


def _dot01_r(x, m01):
    hi = x.astype(BF16)
    r1 = x - hi.astype(F32)
    mid = r1.astype(BF16)
    lo = (r1 - mid.astype(F32)).astype(BF16)
    return _dot(hi, m01) + _dot(mid, m01) + _dot(lo, m01)


def _sigmoid(x):
    return 1.0 / (1.0 + jnp.exp(-x))


def _silu(x):
    return x * _sigmoid(x)


def _gelu_tanh(x):
    return 0.5 * x * (1.0 + jnp.tanh(0.7978845608028654 * (x + 0.044715 * (x * x * x))))


def _log_sigmoid(x):
    return jnp.minimum(x, 0.0) - jnp.log(1.0 + jnp.exp(-jnp.abs(x)))


def _const_spec(shape):
    nd = len(shape)
    return pl.BlockSpec(shape, lambda *_: (0,) * nd)


def _params(sem):
    return pltpu.CompilerParams(dimension_semantics=sem, vmem_limit_bytes=VMEM_LIMIT)


def _ada_kernel(c_ref, w_ref, b_ref, o_ref):
    c = c_ref[...]
    o_ref[0] = _dot(_silu(c).astype(BF16), w_ref[0]) + b_ref[0]


def _ada(c_all, ada_w, ada_b):
    depth = ada_w.shape[0]
    n = c_all.shape[0]
    return pl.pallas_call(
        _ada_kernel,
        grid=(depth, 6),
        in_specs=[pl.BlockSpec((n, D_MODEL), lambda l, j: (0, 0)),
                  pl.BlockSpec((1, D_MODEL, D_MODEL), lambda l, j: (l, 0, j)),
                  pl.BlockSpec((1, 1, D_MODEL), lambda l, j: (l, 0, j))],
        out_specs=pl.BlockSpec((1, n, D_MODEL), lambda l, j: (l, 0, j)),
        out_shape=jax.ShapeDtypeStruct((depth, n, 6 * D_MODEL), F32),
        compiler_params=_params(("arbitrary", "arbitrary")),
        name="ada",
    )(c_all, ada_w.astype(BF16), ada_b.reshape(depth, 1, 6 * D_MODEL))


def _inproj_kernel(x_ref, mod_ref, g_ref, w_ref, b_ref, wsw_ref, bsw_ref, cos_ref, sin_ref, wc_ref, bc_ref, cg_ref,
                   rows_ref, win_ref, q_ref, kvb_ref, small_ref, mqk_ref, mv_ref, mo_ref, ocm_ref, cvn_ref,
                   *cmp_rows_ref, tc, tiles_per_batch):
    x = x_ref[...]
    tm = x.shape[0]
    xn = x * lax.rsqrt(jnp.mean(x * x, axis=-1, keepdims=True) + EPS) * g_ref[...]
    mod = mod_ref[0]
    h = xn * (1.0 + mod[:, D_MODEL:2 * D_MODEL]) + mod[:, 0:D_MODEL]
    hb = h.astype(BF16)
    z = _dot(hb, w_ref[...]) + b_ref[...]
    zs = _dot(hb, wsw_ref[...]) + bsw_ref[...]
    cos = cos_ref[...]
    sin = sin_ref[...]

    def rope(c0, s0):
        return z[:, c0:c0 + LANE] * cos + zs[:, s0:s0 + LANE] * sin

    for j in range(4):
        q_ref[:, j * LANE:(j + 1) * LANE] = (rope(j * LANE, j * LANE) * Q_SCALE).astype(q_ref.dtype)
    kc = rope(512, 512)
    vc = z[:, 640:768]
    ks = rope(768, 640)
    vs = z[:, 896:1024]
    kw = rope(1024, 768)
    vw = z[:, 1152:1280]
    if cmp_rows_ref:
        rows_ref[0] = jnp.transpose(jnp.concatenate([kc, vc, ks, vs], axis=1))
        cmp_rows_ref[0][:, 0:128] = kc
        cmp_rows_ref[0][:, 128:256] = vc
    else:
        rows_ref[:, 0:128] = kc
        rows_ref[:, 128:256] = vc
        rows_ref[:, 256:384] = ks
        rows_ref[:, 384:512] = vs
    win_ref[:, 0:128] = kw
    win_ref[:, 128:256] = vw
    lane_k = lax.broadcasted_iota(jnp.int32, (tm, LANE), 1)
    row0 = (pl.program_id(0) % tiles_per_batch) * tm
    blk = ((row0 + lax.broadcasted_iota(jnp.int32, (tm, LANE), 0)) // SEL_BLOCK) % HEAD_DIM
    low = lane_k < HEAD_DIM
    kvb_ref[:, 0:128] = jnp.where(low, ks, jnp.where(lane_k == HEAD_DIM + blk, 1.0, 0.0)).astype(BF16)
    kvb_ref[:, 128:256] = jnp.where(low, jnp.where(lane_k == blk, 1.0, 0.0), ks).astype(BF16)
    kvb_ref[:, 256:384] = jnp.where(low, vs, 1.0).astype(BF16)
    kvb_ref[:, 384:512] = jnp.where(low, 1.0, vs).astype(BF16)
    kvb_ref[:, 512:640] = kw.astype(BF16)
    kvb_ref[:, 640:768] = vw.astype(BF16)
    sm = z[:, 2560:2688]
    lane = lax.broadcasted_iota(jnp.int32, sm.shape, 1)
    small_ref[...] = jnp.where(lane < 24, _sigmoid(sm), jnp.where(lane < 28, sm, _log_sigmoid(sm)))
    mqk_ref[...] = z[:, 1280:1536]
    mv_ref[...] = z[:, 1536:1792].astype(BF16)
    mo_ref[...] = _sigmoid(z[:, 1792:2048])
    cu = _gelu_tanh(z[:, 2048:2304])
    cv = _gelu_tanh(z[:, 2304:2560])
    cvn = cv * lax.rsqrt(jnp.mean(cv * cv, axis=-1, keepdims=True) + EPS) * cg_ref[...]
    cvn_ref[...] = cvn
    lane_c = lax.broadcasted_iota(jnp.int32, (tc, CMLP_W), 1) // HEAD_DIM
    for c in range(tm // tc):
        v = cvn[c * tc:(c + 1) * tc]
        vbd = jnp.concatenate([jnp.where(lane_c == g, v, 0.0) for g in range(CMLP_GROUPS)], axis=0).astype(BF16)
        s = _dot(wc_ref[...], vbd) + bc_ref[...]
        ocm_ref[c * tc:(c + 1) * tc, :] = (cu[c * tc:(c + 1) * tc] * s).astype(BF16)


def _inproj(x2d, mod, g1, wp, bp, wsw, bsw, cos_t, sin_t, wc, bc, cg, *, tm, tiles_per_batch, tc, q_dtype,
            rows_feature_major):
    t_total = x2d.shape[0]
    n_tiles = t_total // tm
    r = mod.shape[1]
    row = lambda w: pl.BlockSpec((tm, w), lambda i: (i, 0))
    if rows_feature_major:
        rows_shape = jax.ShapeDtypeStruct((n_tiles // tiles_per_batch, 512, tiles_per_batch * tm), F32)
        rows_spec = pl.BlockSpec((1, 512, tm), lambda i: (i // tiles_per_batch, 0, i % tiles_per_batch))
    else:
        rows_shape = jax.ShapeDtypeStruct((t_total, 512), F32)
        rows_spec = row(512)
    out_shapes = (
        rows_shape,
        jax.ShapeDtypeStruct((t_total, 256), F32),
        jax.ShapeDtypeStruct((t_total, 512), q_dtype),
        jax.ShapeDtypeStruct((t_total, 768), BF16),
        jax.ShapeDtypeStruct((t_total, 128), F32),
        jax.ShapeDtypeStruct((t_total, 256), F32),
        jax.ShapeDtypeStruct((t_total, 256), BF16),
        jax.ShapeDtypeStruct((t_total, 256), F32),
        jax.ShapeDtypeStruct((t_total, 256), BF16),
        jax.ShapeDtypeStruct((t_total, 256), F32),
    )
    out_specs = [rows_spec, row(256), row(512), row(768), row(128), row(256), row(256), row(256), row(256), row(256)]
    if rows_feature_major:
        out_shapes += (jax.ShapeDtypeStruct((t_total, 256), F32),)
        out_specs.append(row(256))
    return pl.pallas_call(
        functools.partial(_inproj_kernel, tc=tc, tiles_per_batch=tiles_per_batch),
        grid=(n_tiles,),
        in_specs=[row(D_MODEL),
                  pl.BlockSpec((1, r, 6 * D_MODEL), lambda i: (i // tiles_per_batch, 0, 0)),
                  _const_spec((1, D_MODEL)),
                  _const_spec((D_MODEL, NP_COLS)), _const_spec((1, NP_COLS)),
                  _const_spec((D_MODEL, NSW_COLS)), _const_spec((1, NSW_COLS)),
                  pl.BlockSpec((tm, LANE), lambda i: (i % tiles_per_batch, 0)),
                  pl.BlockSpec((tm, LANE), lambda i: (i % tiles_per_batch, 0)),
                  _const_spec(wc.shape), _const_spec(bc.shape), _const_spec((1, CMLP_W))],
        out_specs=out_specs,
        out_shape=out_shapes,
        compiler_params=_params(("arbitrary",)),
        name="inproj",
    )(x2d, mod, g1, wp, bp, wsw, bsw, cos_t, sin_t, wc, bc, cg)


def _compress_kernel(rows_ref, a_ref, w_ref, o_ref):
    x = rows_ref[0]
    n = x.shape[0] // CMP_BLOCK
    pooled = jnp.sum(x.reshape(n, CMP_BLOCK, 2 * KV_W) * a_ref[...][None], axis=1)
    o_ref[0] = _dot(pooled.astype(BF16), w_ref[...])


def _compress(rows3, a_tile, w_bd, *, rows_per_step):
    b, s, _ = rows3.shape
    nb = s // CMP_BLOCK
    steps = s // rows_per_step
    return pl.pallas_call(
        _compress_kernel,
        grid=(b, steps),
        in_specs=[pl.BlockSpec((1, rows_per_step, 2 * KV_W), lambda i, j: (i, j, 0)),
                  _const_spec((CMP_BLOCK, 2 * KV_W)), _const_spec((2 * KV_W, 2 * KV_W))],
        out_specs=pl.BlockSpec((1, rows_per_step // CMP_BLOCK, 2 * KV_W), lambda i, j: (i, j, 0)),
        out_shape=jax.ShapeDtypeStruct((b, nb, 2 * KV_W), F32),
        compiler_params=_params(("arbitrary", "arbitrary")),
        name="compress",
    )(rows3, a_tile, w_bd)


def _group_queries(qf, g):
    lane = lax.broadcasted_iota(jnp.int32, (qf.shape[0], LANE), 1)
    keep = (lane >= HEAD_DIM * g) & (lane < HEAD_DIM * (g + 1))
    out = []
    for r in range(NSA_GROUP):
        h = NSA_GROUP * g + r
        slab = qf[:, (h // 2) * LANE:(h // 2 + 1) * LANE]
        if h % 2 != g:
            slab = pltpu.roll(slab, HEAD_DIM, axis=1)
        out.append(jnp.where(keep, slab, 0.0))
    return out


def _place_heads(pieces, g):
    lane = lax.broadcasted_iota(jnp.int32, pieces[0].shape, 1)
    slabs = []
    for j in range(2):
        halves = []
        for half in range(2):
            p = pieces[2 * j + half]
            if half != g:
                p = pltpu.roll(p, HEAD_DIM, axis=1)
            halves.append(p)
        slabs.append(jnp.where(lane < HEAD_DIM, halves[0], halves[1]))
    return slabs


def _softmax_rows(s, allowed):
    sm = jnp.where(allowed, s, NEG)
    m = jnp.max(sm, axis=-1, keepdims=True)
    e = jnp.where(allowed, jnp.exp2(sm - m), 0.0)
    l = jnp.sum(e, axis=-1, keepdims=True)
    return e * (1.0 / jnp.maximum(l, 1e-30))


def _topk_select(vs, jrow, n_rounds):
    big = float(vs[0].shape[0] + 1)

    def body(_, carry):
        out = []
        for v in carry:
            m = jnp.max(v, axis=0, keepdims=True)
            first = jnp.min(jnp.where(v == m, jrow, big), axis=0, keepdims=True)
            out.append(jnp.where(jrow == first, -jnp.inf, v))
        return tuple(out)

    res = lax.fori_loop(0, n_rounds, body, tuple(vs))
    return [jnp.where((r == -jnp.inf) & (v > 0.5 * NEG), 1.0, 0.0) for r, v in zip(res, vs)]


def _expand_tile(sel_b, t, kt):
    nblk = sel_b.shape[1]
    jrow = lax.broadcasted_iota(jnp.int32, (nblk, kt), 0) - t * (kt // SEL_BLOCK)
    cdiv = lax.broadcasted_iota(jnp.int32, (nblk, kt), 1) // SEL_BLOCK
    e = jnp.where(jrow == cdiv, 1.0, 0.0).astype(BF16)
    return _dot(sel_b, e)


def _nsa_prompt_kernel(q_ref, small_ref, kc_ref, vc_ref, kvb_ref, o_ref, m_scr, acc_scr, lhs_scr, s_scr, park_scr,
                       qz_scr, *, kt, nbs):
    i = pl.program_id(1)
    tq = q_ref.shape[0]
    nbc = kc_ref.shape[1]
    qf = q_ref[...].astype(F32)
    q0 = i * tq
    qpos_c = q0 + lax.broadcasted_iota(jnp.int32, (tq, 1), 0)
    qpos_r = q0 + lax.broadcasted_iota(jnp.int32, (1, tq), 1)
    eye = jnp.where(lax.broadcasted_iota(jnp.int32, (tq, tq), 0) == lax.broadcasted_iota(jnp.int32, (tq, tq), 1),
                    1.0, 0.0).astype(BF16)
    qpos_r4 = q0 + lax.broadcasted_iota(jnp.int32, (1, NSA_GROUP * tq), 1) % tq
    kc = kc_ref[0].astype(BF16)
    vc = vc_ref[0].astype(BF16)
    half = nbc // 2
    rho_c = lax.broadcasted_iota(jnp.int32, (nbc, 1), 0)
    end_c = jnp.where(rho_c < half, 2 * rho_c, 2 * (rho_c - half) + 1) * CMP_BLOCK + (CMP_BLOCK - 1)
    rho_r = lax.broadcasted_iota(jnp.int32, (1, nbc), 1)
    end_r = jnp.where(rho_r < half, 2 * rho_r, 2 * (rho_r - half) + 1) * CMP_BLOCK + (CMP_BLOCK - 1)
    w_start = pl.multiple_of(jnp.maximum(q0 - WINDOW, 0), LANE)
    w_len = WINDOW + tq
    wpos = w_start + lax.broadcasted_iota(jnp.int32, (1, w_len), 1)
    bias_w = jnp.where((wpos <= qpos_c) & (wpos > qpos_c - WINDOW), 0.0, NEG)
    bias_c = jnp.where(end_r <= qpos_c, 0.0, NEG)
    any_c = qpos_c >= CMP_BLOCK - 1
    k_w = kvb_ref[0, pl.ds(w_start, w_len), 512:640]
    v_w = kvb_ref[0, pl.ds(w_start, w_len), 640:768]

    def attend(q_b, k, v_ones, bias):
        s = _dot_nt(q_b, k) + bias
        p = jnp.exp2(s - jnp.max(s, axis=-1, keepdims=True))
        o = _dot(p.astype(BF16), v_ones)
        return o * (1.0 / pltpu.roll(o, HEAD_DIM, axis=1))

    lane_g = lax.broadcasted_iota(jnp.int32, (1, LANE), 1)
    one_b = jnp.ones((), BF16)
    jrow_i = lax.broadcasted_iota(jnp.int32, (nbs, tq), 0)
    for g in range(NSA_KV_HEADS):
        qz4 = _group_queries(qf, g)
        qz_b = [x.astype(BF16) for x in qz4]
        qz = jnp.concatenate(qz_b, axis=0)
        keep_g = (lane_g >= HEAD_DIM * g) & (lane_g < HEAD_DIM * (g + 1))
        vc_x = jnp.where(keep_g, vc, one_b)
        o_c = [jnp.where(any_c, attend(qz_b[r], kc, vc_x, bias_c), 0.0) for r in range(NSA_GROUP)]
        s_t = _dot_nt(kc, qz)
        sm = s_t + jnp.where(end_c <= qpos_r4, 0.0, NEG)
        et = jnp.exp2(sm - jnp.max(sm, axis=0, keepdims=True))
        pt = et * (1.0 / jnp.sum(et, axis=0, keepdims=True))
        ph = pt[:, 0:tq] + pt[:, tq:2 * tq] + pt[:, 2 * tq:3 * tq] + pt[:, 3 * tq:4 * tq]
        imp = jnp.where(qpos_r >= CMP_BLOCK - 1, ph[0:half] + ph[half:nbc], 0.0)
        if nbs > half:
            imp = jnp.concatenate([imp, jnp.zeros((nbs - half, tq), F32)], axis=0)
        forced = (jrow_i == qpos_r // SEL_BLOCK) | (jrow_i == 0)
        valid = jrow_i * SEL_BLOCK <= qpos_r
        score = jnp.where(valid, imp + jnp.where(forced, FORCE, 0.0), NEG)
        for r in range(NSA_GROUP):
            park_scr[r] = o_c[r]
            qz_scr[r] = qz_b[r]
        sel_t = _topk_select([score], jrow_i.astype(F32), min(SEL_TOPN, nbs))[0]
        m_scr[...] = jnp.full(m_scr.shape, NEG, F32)
        acc_scr[...] = jnp.zeros(acc_scr.shape, F32)
        sel_q = _dot_nt(eye, sel_t.astype(BF16))
        if nbs < LANE:
            sel_q = jnp.concatenate([sel_q, jnp.zeros((tq, LANE - nbs), F32)], axis=1)
        negsel = (sel_q - 1.0) * 1e30
        base = HEAD_DIM * (1 - g)
        lane_q = lax.broadcasted_iota(jnp.int32, (tq, LANE), 1)
        feat_mask = (lane_q >= base) & (lane_q < base + HEAD_DIM)
        for hf in range(max(1, nbs // HEAD_DIM)):
            feat = (negsel if HEAD_DIM * hf == base else pltpu.roll(negsel, HEAD_DIM, axis=1)).astype(BF16)
            for r in range(NSA_GROUP):
                lhs_scr[hf, r] = jnp.where(feat_mask, feat, qz_b[r])
        tiles_per_half = HEAD_DIM * SEL_BLOCK // kt

        def scores(t, slot):
            k_t = kvb_ref[0, pl.ds(pl.multiple_of(t * kt, kt), kt), g * LANE:(g + 1) * LANE]
            hf = t // tiles_per_half
            for r in range(NSA_GROUP):
                s_scr[slot, r] = _dot_nt(lhs_scr[hf, r], k_t)

        def consume(t, slot, causal):
            k0 = pl.multiple_of(t * kt, kt)
            v_t = kvb_ref[0, pl.ds(k0, kt), (2 + g) * LANE:(3 + g) * LANE]
            if causal:
                kpos = k0 + lax.broadcasted_iota(jnp.int32, (1, kt), 1)
                bias = jnp.where(kpos <= qpos_c, 0.0, NEG)
            for r in range(NSA_GROUP):
                s = s_scr[slot, r]
                if causal:
                    s = s + bias
                m_old = m_scr[r]
                m_new = jnp.maximum(m_old, jnp.max(s, axis=-1, keepdims=True))
                p = jnp.exp2(s - jnp.concatenate([m_new] * (kt // LANE), axis=1))
                acc_scr[r] = jnp.exp2(m_old - m_new) * acc_scr[r] + _dot(p.astype(BF16), v_t)
                m_scr[r] = m_new

        n_full = q0 // kt
        scores(0, 0)

        def pair_body(tp, _):
            scores(2 * tp + 1, 1)
            consume(2 * tp, 0, False)
            scores(2 * tp + 2, 0)
            consume(2 * tp + 1, 1, False)
            return 0

        lax.fori_loop(0, n_full // 2, pair_body, 0)

        @pl.when(n_full % 2 == 1)
        def _odd():
            scores(n_full, 1)
            consume(n_full - 1, 0, False)
            consume(n_full, 1, True)

        @pl.when(n_full % 2 == 0)
        def _even():
            consume(n_full, 0, True)
        acc = acc_scr[...].reshape(NSA_GROUP * tq, LANE)
        o_s = acc * (1.0 / jnp.maximum(pltpu.roll(acc, HEAD_DIM, axis=1), 1e-30))
        v_wx = jnp.where(keep_g, v_w, one_b)
        pieces = []
        for r in range(NSA_GROUP):
            h = NSA_GROUP * g + r
            sl = slice(r * tq, (r + 1) * tq)
            o_w = attend(qz_scr[r], k_w, v_wx, bias_w)
            pieces.append(small_ref[:, 3 * h:3 * h + 1] * park_scr[r] + small_ref[:, 3 * h + 1:3 * h + 2] * o_s[sl]
                          + small_ref[:, 3 * h + 2:3 * h + 3] * o_w)
        for j, slab in enumerate(_place_heads(pieces, g)):
            o_ref[:, (2 * g + j) * LANE:(2 * g + j + 1) * LANE] = slab.astype(o_ref.dtype)


def _nsa_prompt(qb, small, kcp, vcp, kvb3, *, b, s, kt):
    tq = NSA_TQ
    nq = s // tq
    nbc = kcp.shape[1]
    nbs = s // SEL_BLOCK
    assert s >= WINDOW + tq and s % kt == 0 and nbc == 2 * nbs and nbs <= LANE and kt == SEL_KT
    return pl.pallas_call(
        functools.partial(_nsa_prompt_kernel, kt=kt, nbs=nbs),
        grid=(b, nq),
        in_specs=[pl.BlockSpec((tq, 512), lambda bi, i: (bi * nq + i, 0)),
                  pl.BlockSpec((tq, LANE), lambda bi, i: (bi * nq + i, 0)),
                  pl.BlockSpec((1, nbc, LANE), lambda bi, i: (bi, 0, 0)),
                  pl.BlockSpec((1, nbc, LANE), lambda bi, i: (bi, 0, 0)),
                  pl.BlockSpec((1, s, 768), lambda bi, i: (bi, 0, 0))],
        out_specs=pl.BlockSpec((tq, 512), lambda bi, i: (bi * nq + i, 0)),
        out_shape=jax.ShapeDtypeStruct((b * s, 512), BF16),
        scratch_shapes=[pltpu.VMEM((NSA_GROUP, tq, LANE), F32), pltpu.VMEM((NSA_GROUP, tq, LANE), F32),
                        pltpu.VMEM((max(1, nbs // HEAD_DIM), NSA_GROUP, tq, LANE), BF16),
                        pltpu.VMEM((2, NSA_GROUP, tq, kt), F32),
                        pltpu.VMEM((NSA_GROUP, tq, LANE), F32), pltpu.VMEM((NSA_GROUP, tq, LANE), BF16)],
        compiler_params=_params(("arbitrary", "arbitrary")),
        name="nsa_prompt",
    )(qb, small, kcp, vcp, kvb3)


def _nsa_sample_kernel(pt_ref, q_ref, small_ref, rows_ref, wnew_ref, wcache_ref, a_ref, w_ref, pair_ref, *rest,
                       n_pp, kt):
    page_refs = rest[:n_pp]
    o_ref = rest[n_pp]
    ksel, vsel, pooled, s_scr = rest[n_pp + 1:]
    pg = pl.program_id(1)
    npg = pooled.shape[0]
    past = ksel.shape[1]
    nblk = past // SEL_BLOCK
    a_t = a_ref[...]
    per_page = PAGE_SIZE // CMP_BLOCK
    t_i = lax.broadcasted_iota(jnp.int32, (PAGE_SIZE, per_page * n_pp), 0) // CMP_BLOCK
    n_i = lax.broadcasted_iota(jnp.int32, (PAGE_SIZE, per_page * n_pp), 1)
    acc_p = jnp.zeros((2 * KV_W, per_page * n_pp), F32)
    for k in range(n_pp):
        page = page_refs[k][0, 0]
        xa = page[0:2 * KV_W, :] * a_t
        seg = jnp.where(n_i == t_i + per_page * k, 1.0, 0.0).astype(BF16)
        hi = xa.astype(BF16)
        lo = (xa - hi.astype(F32)).astype(BF16)
        acc_p = acc_p + _dot(hi, seg) + _dot(lo, seg)
        c0 = pl.multiple_of((pg * n_pp + k) * PAGE_SIZE, PAGE_SIZE)
        ksel[:, pl.ds(c0, PAGE_SIZE)] = page[256:384, :].astype(BF16)
        vsel[:, pl.ds(c0, PAGE_SIZE)] = page[384:512, :].astype(BF16)
    pooled[pg] = acc_p

    @pl.when(pg == pl.num_programs(1) - 1)
    def _finish():
        nq = q_ref.shape[1]
        cmp_t = [_dot(w_ref[...], pooled[j].astype(BF16)) for j in range(npg)]
        kc = jnp.concatenate([c[0:KV_W] for c in cmp_t], axis=1).astype(BF16)
        vc = jnp.concatenate([c[KV_W:2 * KV_W] for c in cmp_t], axis=1).astype(BF16)
        qf = q_ref[0]
        gates = small_ref[0]
        qg = [_group_queries(qf, g) for g in range(NSA_KV_HEADS)]
        blocks = []
        for r in range(NSA_GROUP):
            blocks += [qg[0][r], qg[1][r]]
        qz = jnp.concatenate(blocks, axis=0).astype(BF16)
        rows_n = qz.shape[0]
        rpr = NSA_KV_HEADS * nq
        qidx_c = lax.broadcasted_iota(jnp.int32, (rows_n, 1), 0) % nq
        s_c = _dot(qz, kc)
        e_c = jnp.exp2(s_c - jnp.max(s_c, axis=-1, keepdims=True))
        p_c = e_c * (1.0 / jnp.sum(e_c, axis=-1, keepdims=True))
        o_c = _dot_nt(p_c.astype(BF16), vc)
        p_sum = p_c[0:rpr] + p_c[rpr:2 * rpr] + p_c[2 * rpr:3 * rpr] + p_c[3 * rpr:4 * rpr]
        imp = _dot01_r(p_sum, pair_ref[...])
        jcol = lax.broadcasted_iota(jnp.int32, imp.shape, 1)
        score = imp + jnp.where(jcol == 0, FORCE, 0.0)
        sel32 = _topk_select_lanes(score, jcol.astype(F32), min(SEL_TOPN - 1, nblk))
        sel_b = jnp.concatenate([sel32] * NSA_GROUP, axis=0).astype(BF16)

        def scores(t, slot):
            k_t = ksel[:, pl.ds(pl.multiple_of(t * kt, kt), kt)]
            s_scr[slot] = _dot(qz, k_t) + jnp.where(_expand_tile(sel_b, t, kt) > 0.5, 0.0, NEG)

        def consume(t, slot, carry):
            m_old, l_old, acc = carry
            v_t = vsel[:, pl.ds(pl.multiple_of(t * kt, kt), kt)]
            s = s_scr[slot]
            m_new = jnp.maximum(m_old, jnp.max(s, axis=-1, keepdims=True))
            p = jnp.exp2(s - m_new)
            alpha = jnp.exp2(m_old - m_new)
            l_new = alpha * l_old + jnp.sum(p, axis=-1, keepdims=True)
            return m_new, l_new, alpha * acc + _dot_nt(p.astype(BF16), v_t)

        def pair_body(tp, carry):
            scores(2 * tp + 1, 1)
            carry = consume(2 * tp, 0, carry)
            scores(2 * tp + 2, 0)
            return consume(2 * tp + 1, 1, carry)

        n_t = past // kt
        scores(0, 0)
        init = (jnp.full((rows_n, 1), NEG, F32), jnp.zeros((rows_n, 1), F32), jnp.zeros((rows_n, LANE), F32))
        carry = lax.fori_loop(0, n_t // 2 - 1, pair_body, init)
        scores(n_t - 1, 1)
        carry = consume(n_t - 2, 0, carry)
        m_old, l_old, acc = consume(n_t - 1, 1, carry)
        newr = rows_ref[0]
        zk = jnp.zeros((LANE - nq, LANE), F32)
        k_n = jnp.concatenate([newr[:, 256:384], zk], axis=0).astype(BF16)
        v_n = jnp.concatenate([newr[:, 384:512], zk], axis=0).astype(BF16)
        ccol = lax.broadcasted_iota(jnp.int32, (1, LANE), 1)
        allow_n = (ccol <= qidx_c) & (ccol < nq)
        sm = jnp.where(allow_n, _dot_nt(qz, k_n), NEG)
        m_new = jnp.maximum(m_old, jnp.max(sm, axis=-1, keepdims=True))
        p = jnp.where(allow_n, jnp.exp2(sm - m_new), 0.0)
        alpha = jnp.exp2(m_old - m_new)
        l_new = alpha * l_old + jnp.sum(p, axis=-1, keepdims=True)
        acc = alpha * acc + _dot(p.astype(BF16), v_n)
        o_s = acc * (1.0 / jnp.maximum(l_new, 1e-30))
        wc = wcache_ref[0, 0]
        wn = wnew_ref[0]
        n_keep = wc.shape[1]
        k_wn = jnp.concatenate([wn[:, 0:KV_W], zk], axis=0).astype(BF16)
        v_wn = jnp.concatenate([wn[:, KV_W:2 * KV_W], zk], axis=0).astype(BF16)
        s_w = jnp.concatenate([_dot(qz, wc[0:KV_W].astype(BF16)), _dot_nt(qz, k_wn)], axis=1)
        wcol = lax.broadcasted_iota(jnp.int32, (1, n_keep + LANE), 1)
        allow_w = (((wcol < n_keep) & (wcol + (WINDOW - n_keep) > qidx_c))
                   | ((wcol >= n_keep) & (wcol - n_keep <= qidx_c) & (wcol - n_keep < nq)))
        p_w = _softmax_rows(s_w, allow_w).astype(BF16)
        o_w = _dot_nt(p_w[:, 0:n_keep], wc[KV_W:2 * KV_W].astype(BF16)) + _dot(p_w[:, n_keep:], v_wn)
        slabs = []
        for g in range(NSA_KV_HEADS):
            pieces = []
            for r in range(NSA_GROUP):
                h = NSA_GROUP * g + r
                sl = slice(r * rpr + g * nq, r * rpr + (g + 1) * nq)
                pieces.append(gates[:, 3 * h:3 * h + 1] * o_c[sl] + gates[:, 3 * h + 1:3 * h + 2] * o_s[sl]
                              + gates[:, 3 * h + 2:3 * h + 3] * o_w[sl])
            slabs.extend(_place_heads(pieces, g))
        for j in range(4):
            o_ref[0, :, j * LANE:(j + 1) * LANE] = slabs[j]


def _topk_select_lanes(v, jcol, n_rounds):
    big = float(v.shape[1] + 1)

    def body(_, r):
        m = jnp.max(r, axis=1, keepdims=True)
        first = jnp.min(jnp.where(r == m, jcol, big), axis=1, keepdims=True)
        return jnp.where(jcol == first, -jnp.inf, r)

    res = lax.fori_loop(0, n_rounds, body, v)
    return jnp.where((res == -jnp.inf) & (v > 0.5 * NEG), 1.0, 0.0)


def _nsa_sample(page_table, q3, small3, rows3, wnew3, wcache_t, a_t, w_bd_t, pair, cache_t, *, layer, kt):
    b, n_pages = page_table.shape
    nq = q3.shape[1]
    past = n_pages * PAGE_SIZE
    n_pp = PAGES_PER_STEP
    assert n_pages % n_pp == 0 and past % kt == 0 and nq % 8 == 0 and nq < CMP_BLOCK
    assert n_pp * PAGE_SIZE // CMP_BLOCK == LANE and (past // kt) % 2 == 0
    npg = n_pages // n_pp
    per = lambda w: pl.BlockSpec((1, nq, w), lambda bi, pg, pt: (bi, 0, 0))
    page_specs = [pl.BlockSpec((1, 1, 512, PAGE_SIZE), lambda bi, pg, pt, k=k: (layer, pt[bi, pg * n_pp + k], 0, 0))
                  for k in range(n_pp)]
    const = lambda shape: pl.BlockSpec(shape, lambda bi, pg, pt: (0,) * len(shape))
    grid_spec = pltpu.PrefetchScalarGridSpec(
        num_scalar_prefetch=1,
        grid=(b, npg),
        in_specs=[per(512), per(LANE), per(512), per(256),
                  pl.BlockSpec((1, 1, 256, wcache_t.shape[3]), lambda bi, pg, pt: (layer, bi, 0, 0)),
                  const(a_t.shape), const(w_bd_t.shape), const(pair.shape)] + page_specs,
        out_specs=per(512),
        scratch_shapes=[pltpu.VMEM((LANE, past), BF16), pltpu.VMEM((LANE, past), BF16),
                        pltpu.VMEM((npg, 2 * KV_W, LANE), F32),
                        pltpu.VMEM((2, NSA_HEADS * nq, kt), F32)],
    )
    return pl.pallas_call(
        functools.partial(_nsa_sample_kernel, n_pp=n_pp, kt=kt),
        grid_spec=grid_spec,
        out_shape=jax.ShapeDtypeStruct((b, nq, 512), F32),
        compiler_params=_params(("arbitrary", "arbitrary")),
        name="nsa_sample",
    )(page_table, q3, small3, rows3, wnew3, wcache_t, a_t, w_bd_t, pair, *([cache_t] * n_pp))


def _mlstm_kernel(mqk_ref, mv_ref, mo_ref, small_ref, smallt_ref, hist_ref, ext0_ref, m0_ref, cw_ref, cb_ref,
                  wq_ref, wk_ref, ng_ref, hm_ref, ext_ref, m_ref, ext_scr, m_scr, carry_scr, *, L):
    c = pl.program_id(1)

    @pl.when(c == 0)
    def _init():
        ext_scr[...] = ext0_ref[0]
        m_scr[...] = m0_ref[0]
        carry_scr[...] = hist_ref[0]

    x = mqk_ref[0]
    tl = x.shape[0]
    full = jnp.concatenate([carry_scr[...], x], axis=0)
    y = cb_ref[...]
    for j in range(MLSTM_CONV):
        y = y + cw_ref[j:j + 1, :] * full[8 - (MLSTM_CONV - 1) + j:8 - (MLSTM_CONV - 1) + j + tl]
    carry_scr[...] = full[tl:tl + 8]
    qk = _silu(y).astype(BF16)
    q_all = _dot(qk, wq_ref[...])
    k_all = _dot(qk, wk_ref[...]) * (HEAD_DIM ** -0.5)
    lane = lax.broadcasted_iota(jnp.int32, (L, LANE), 1)
    lane_row = lax.broadcasted_iota(jnp.int32, (1, LANE), 1)
    t_i = lax.broadcasted_iota(jnp.int32, (L, L), 0)
    s_i = lax.broadcasted_iota(jnp.int32, (L, L), 1)
    causal = s_i <= t_i
    tril_b = jnp.where(causal, 1.0, 0.0).astype(BF16)
    triu_b = jnp.where(t_i <= s_i, 1.0, 0.0).astype(BF16)
    eye_b = jnp.where(lax.broadcasted_iota(jnp.int32, (LANE, LANE), 0) == lax.broadcasted_iota(jnp.int32, (LANE, LANE), 1),
                      1.0, 0.0).astype(BF16)
    for u in range(tl // L):
        sl = slice(u * L, (u + 1) * L)
        sm = small_ref[0, sl, :]
        smt = smallt_ref[0, :, sl]
        bcols = _dot01(tril_b, sm)
        brows = _dot01_r(smt, triu_b)
        mrow = m_scr[...]
        for j in range(2):
            cs = slice(j * LANE, (j + 1) * LANE)
            q_slab = q_all[sl, cs]
            k_b = k_all[sl, cs].astype(BF16)
            v_slab = mv_ref[0, sl, cs].astype(F32)
            k_t = _dot_nt(eye_b, k_b).astype(BF16)
            outs = []
            for half in range(2):
                h = 2 * j + half
                keep = (lane >= HEAD_DIM * half) & (lane < HEAD_DIM * (half + 1))
                ones_lane = HEAD_DIM if half == 0 else 0
                qm = jnp.where(keep, q_slab, 0.0).astype(BF16)
                v_ext = jnp.where(keep, v_slab, jnp.where(lane == ones_lane, 1.0, 0.0))
                b_col = bcols[:, 28 + h:29 + h]
                li_col = sm[:, 24 + h:25 + h]
                b_row = brows[4 + h:5 + h, :]
                li_row = smt[h:h + 1, :]
                m_h = mrow[:, h:h + 1]
                d = jnp.where(causal, b_col - b_row + li_row, NEG)
                inter = b_col + m_h
                mt = jnp.maximum(inter, jnp.max(d, axis=-1, keepdims=True))
                a = _dot_nt(qm, k_b) * jnp.exp(d - mt)
                wi = jnp.exp(inter - mt)
                cts = ext_scr[h]
                num = wi * _dot(qm, cts.astype(BF16)) + _dot(a.astype(BF16), v_ext.astype(BF16))
                den = jnp.maximum(jnp.abs(num[:, ones_lane:ones_lane + 1]), jnp.exp(-mt))
                outs.append(num / den)
                bl = b_col[L - 1:L, :]
                g_col = bl - b_col + li_col
                m_new = jnp.maximum(bl + m_h, jnp.max(g_col, axis=0, keepdims=True))
                w_col = jnp.exp(g_col - m_new)
                dec = jnp.exp(bl + m_h - m_new)
                ext_scr[h] = dec * cts + _dot(k_t, (w_col * v_ext).astype(BF16))
                mrow = jnp.where(lane_row == h, m_new, mrow)
            hs = jnp.where(lane < HEAD_DIM, outs[0], outs[1])
            sq = hs * hs
            s0 = jnp.sum(jnp.where(lane < HEAD_DIM, sq, 0.0), axis=-1, keepdims=True)
            s1 = jnp.sum(jnp.where(lane < HEAD_DIM, 0.0, sq), axis=-1, keepdims=True)
            rs = jnp.where(lane < HEAD_DIM, lax.rsqrt(s0 * (1.0 / HEAD_DIM) + EPS), lax.rsqrt(s1 * (1.0 / HEAD_DIM) + EPS))
            hm_ref[0, sl, cs] = (mo_ref[0, sl, cs] * (hs * rs * ng_ref[:, cs])).astype(BF16)
        m_scr[...] = mrow

    @pl.when(c == pl.num_programs(1) - 1)
    def _out():
        ext_ref[0] = ext_scr[...]
        m_ref[0] = m_scr[...]


def _mlstm(mqk3, mv3, mo3, small3, smallt3, hist, ext0, m0, cw, cb, wq, wk, ng, *, tl):
    b, s, _ = mqk3.shape
    L = MLSTM_L
    assert s % tl == 0 and tl % L == 0
    blk = lambda w: pl.BlockSpec((1, tl, w), lambda bi, c: (bi, c, 0))
    st = lambda shape: pl.BlockSpec((1,) + shape, lambda bi, c: (bi,) + (0,) * len(shape))
    const = lambda shape: pl.BlockSpec(shape, lambda bi, c: (0,) * len(shape))
    return pl.pallas_call(
        functools.partial(_mlstm_kernel, L=L),
        grid=(b, s // tl),
        in_specs=[blk(256), blk(256), blk(256), blk(LANE),
                  pl.BlockSpec((1, 8, tl), lambda bi, c: (bi, 0, c)),
                  st((8, 256)), st((MLSTM_HEADS, LANE, LANE)), st((1, LANE)),
                  const((MLSTM_CONV, 256)), const((1, 256)), const((256, 256)), const((256, 256)), const((1, 256))],
        out_specs=[blk(256), st((MLSTM_HEADS, LANE, LANE)), st((1, LANE))],
        out_shape=(jax.ShapeDtypeStruct((b, s, 256), BF16),
                   jax.ShapeDtypeStruct((b, MLSTM_HEADS, LANE, LANE), F32),
                   jax.ShapeDtypeStruct((b, 1, LANE), F32)),
        scratch_shapes=[pltpu.VMEM((MLSTM_HEADS, LANE, LANE), F32), pltpu.VMEM((1, LANE), F32),
                        pltpu.VMEM((8, 256), F32)],
        compiler_params=_params(("arbitrary", "arbitrary")),
        name="mlstm",
    )(mqk3, mv3, mo3, small3, smallt3, hist, ext0, m0, cw, cb, wq, wk, ng)


FF_CHUNK = 1408


def _post_kernel(*refs, final, segmented, tiles_per_batch, seg_len):
    if segmented:
        (x_ref, onsa_ref, hm_ref, ocm_ref, mod_ref, wout_ref, g2_ref, wup_ref, cw_ref, cb_ref, wdn_ref, fg_ref,
         h1_ref, h2_ref, y_ref, fst_ref) = refs
        carry_scr = None
    else:
        (x_ref, onsa_ref, hm_ref, ocm_ref, mod_ref, wout_ref, g2_ref, wup_ref, cw_ref, cb_ref, wdn_ref, fg_ref,
         y_ref, fst_ref, carry_scr) = refs
    x = x_ref[...]
    tm = x.shape[0]
    mix = (_dot(onsa_ref[...].astype(BF16), wout_ref[0:512, :]) + _dot(hm_ref[...], wout_ref[512:768, :])
           + _dot(ocm_ref[...], wout_ref[768:1024, :]))
    mod = mod_ref[0]
    x1 = x + mod[:, 2 * D_MODEL:3 * D_MODEL] * mix
    h2 = x1 * lax.rsqrt(jnp.mean(x1 * x1, axis=-1, keepdims=True) + EPS) * g2_ref[...]
    hb = (h2 * (1.0 + mod[:, 4 * D_MODEL:5 * D_MODEL]) + mod[:, 3 * D_MODEL:4 * D_MODEL]).astype(BF16)
    if not segmented:
        @pl.when(pl.program_id(0) % tiles_per_batch == 0)
        def _reset():
            carry_scr[...] = jnp.zeros(carry_scr.shape, F32)
    else:
        tmod = lax.broadcasted_iota(jnp.int32, (tm, 1), 0) % seg_len
    acc = jnp.zeros((tm, D_MODEL), F32)
    for c0 in range(0, D_FF, FF_CHUNK):
        parts = []
        for part in range(2):
            cs = slice(part * D_FF + c0, part * D_FF + c0 + FF_CHUNK)
            up = _dot(hb, wup_ref[:, cs])
            prev = jnp.zeros((8, FF_CHUNK), F32) if segmented else carry_scr[:, cs]
            full = jnp.concatenate([prev, up], axis=0)
            s1 = full[7:7 + tm]
            s2 = full[6:6 + tm]
            if segmented:
                s1 = jnp.where(tmod >= 1, s1, h1_ref[:, cs])
                s2 = jnp.where(tmod >= 2, s2, h2_ref[:, cs])
                fst_ref[:, cs] = up
            else:
                carry_scr[:, cs] = up[tm - 8:tm]
                fst_ref[0, :, cs] = up[tm - 8:tm]
            parts.append(cb_ref[:, cs] + cw_ref[0:1, cs] * s2 + cw_ref[1:2, cs] * s1 + cw_ref[2:3, cs] * up)
        act = (_silu(parts[0]) * parts[1]).astype(BF16)
        acc = acc + _dot(act, wdn_ref[c0:c0 + FF_CHUNK, :])
    x2 = x1 + mod[:, 5 * D_MODEL:6 * D_MODEL] * acc
    if final:
        x2 = x2 * lax.rsqrt(jnp.mean(x2 * x2, axis=-1, keepdims=True) + EPS) * fg_ref[...]
    y_ref[...] = x2


def _post(x2d, onsa, hm, ocm, mod, wout, g2, wup, cw, cb, wdn, fg, h1=None, h2=None, *, tm, tiles_per_batch, final,
          seg_len=0):
    t_total = x2d.shape[0]
    n_tiles = t_total // tm
    segmented = h1 is not None
    r = mod.shape[1]
    row = lambda w: pl.BlockSpec((tm, w), lambda i: (i, 0))
    in_specs = [row(D_MODEL), row(512), row(256), row(256),
                pl.BlockSpec((1, r, 6 * D_MODEL), lambda i: (i // tiles_per_batch, 0, 0)),
                _const_spec((D_MODEL, D_MODEL)), _const_spec((1, D_MODEL)), _const_spec((D_MODEL, 2 * D_FF)),
                _const_spec((FFN_CONV, 2 * D_FF)), _const_spec((1, 2 * D_FF)), _const_spec((D_FF, D_MODEL)),
                _const_spec((1, D_MODEL))]
    args = [x2d, onsa, hm, ocm, mod, wout, g2, wup, cw, cb, wdn, fg]
    if segmented:
        in_specs += [row(2 * D_FF), row(2 * D_FF)]
        args += [h1, h2]
        out_specs = [row(D_MODEL), row(2 * D_FF)]
        out_shape = (jax.ShapeDtypeStruct((t_total, D_MODEL), F32), jax.ShapeDtypeStruct((t_total, 2 * D_FF), F32))
        scratch = []
    else:
        nb = n_tiles // tiles_per_batch
        out_specs = [row(D_MODEL), pl.BlockSpec((1, 8, 2 * D_FF), lambda i: (i // tiles_per_batch, 0, 0))]
        out_shape = (jax.ShapeDtypeStruct((t_total, D_MODEL), F32), jax.ShapeDtypeStruct((nb, 8, 2 * D_FF), F32))
        scratch = [pltpu.VMEM((8, 2 * D_FF), F32)]
    return pl.pallas_call(
        functools.partial(_post_kernel, final=final, segmented=segmented, tiles_per_batch=tiles_per_batch,
                          seg_len=seg_len),
        grid=(n_tiles,),
        in_specs=in_specs, out_specs=out_specs, out_shape=out_shape, scratch_shapes=scratch,
        compiler_params=_params(("arbitrary",)),
        name="post",
    )(*args)


def _col_perm():
    o = IN_OFF
    cols = list(range(o[0], o[4])) + list(range(o[5], o[8])) + list(range(o[10], o[12]))
    small = list(range(o[4], o[5])) + list(range(o[8], o[10]))
    small += [IN_W] * (LANE - len(small))
    return np.array(cols + small, dtype=np.int32)


def _swap_perm():
    cols = []
    for start, heads in ((0, NSA_HEADS), (512, NSA_KV_HEADS), (768, NSA_KV_HEADS), (1024, NSA_KV_HEADS)):
        for hh in range(heads):
            base = start + HEAD_DIM * hh
            cols += list(range(base + 32, base + 64)) + list(range(base, base + 32))
    return np.array(cols, dtype=np.int32)


def _block_diag(mats):
    n = len(mats)
    rows = []
    for i, m in enumerate(mats):
        rows.append(jnp.concatenate([m if j == i else jnp.zeros((m.shape[0], mats[j].shape[1]), m.dtype)
                                     for j in range(n)], axis=1))
    return jnp.concatenate(rows, axis=0)


def _rope_tables(pos):
    half = HEAD_DIM // 2
    inv = jnp.power(ROPE_THETA, -jnp.arange(half, dtype=F32) / half)
    ang = pos.astype(F32)[:, None] * inv[None, :]
    cos = jnp.cos(ang)
    sin = jnp.sin(ang)
    return jnp.tile(cos, (1, 4)), jnp.concatenate([-sin, sin, -sin, sin], axis=1)


def _layer_weights(l, w):
    win_z = jnp.concatenate([w['w_in'][l], jnp.zeros((D_MODEL, 1), F32)], axis=1)
    bin_z = jnp.concatenate([w['b_in'][l], jnp.zeros((1,), F32)])
    perm, swp = _col_perm(), _swap_perm()
    cmp_a = w['cmp_a'][l]
    cmp_w = w['cmp_w'][l]
    c_ws = jnp.tril(w['c_ws'][l])
    return dict(
        wp=win_z[:, perm].astype(BF16), bp=bin_z[perm][None, :],
        wsw=w['w_in'][l][:, swp].astype(BF16), bsw=w['b_in'][l][swp][None, :],
        g1=w['norm1_g'][l][None, :], g2=w['norm2_g'][l][None, :],
        cg=w['c_norm_g'][l][None, :], c_ws=c_ws, c_bs=w['c_bs'][l],
        a_tile=jnp.concatenate([cmp_a[0].reshape(CMP_BLOCK, KV_W), cmp_a[1].reshape(CMP_BLOCK, KV_W)], axis=1),
        w_bd=_block_diag([cmp_w[0, 0], cmp_w[0, 1], cmp_w[1, 0], cmp_w[1, 1]]).astype(BF16),
        m_cw=w['m_conv_w'][l], m_cb=w['m_conv_b'][l][None, :],
        wq=_block_diag([w['m_wq'][l][h] for h in range(MLSTM_HEADS)]).astype(BF16),
        wk=_block_diag([w['m_wk'][l][h] for h in range(MLSTM_HEADS)]).astype(BF16),
        ng=w['m_norm_g'][l].reshape(1, MLSTM_W),
        wout=w['w_out'][l].astype(BF16), wup=w['w_up'][l].astype(BF16), wdn=w['w_down'][l].astype(BF16),
        f_cw=w['f_conv_w'][l], f_cb=w['f_conv_b'][l][None, :],
    )


def _mlstm_state_in(c0, n0, m0):
    b = c0.shape[0]
    tiles = []
    for h in range(MLSTM_HEADS):
        r0 = HEAD_DIM * (h % 2)
        ol = HEAD_DIM if h % 2 == 0 else 0
        t = jnp.zeros((b, LANE, LANE), F32)
        t = t.at[:, r0:r0 + HEAD_DIM, r0:r0 + HEAD_DIM].set(jnp.swapaxes(c0[:, h], 1, 2))
        t = t.at[:, r0:r0 + HEAD_DIM, ol].set(n0[:, h])
        tiles.append(t)
    m = jnp.zeros((b, 1, LANE), F32).at[:, 0, 0:MLSTM_HEADS].set(m0)
    return jnp.stack(tiles, axis=1), m


def _mlstm_state_out(ext, m):
    cs, ns = [], []
    for h in range(MLSTM_HEADS):
        r0 = HEAD_DIM * (h % 2)
        ol = HEAD_DIM if h % 2 == 0 else 0
        cs.append(jnp.swapaxes(ext[:, h, r0:r0 + HEAD_DIM, r0:r0 + HEAD_DIM], 1, 2))
        ns.append(ext[:, h, r0:r0 + HEAD_DIM, ol])
    return jnp.stack(cs, axis=1), jnp.stack(ns, axis=1), m[:, 0, 0:MLSTM_HEADS]


def _layer_prompt(x2d, mod, lw, fg, *, b, s, final, tm=256):
    assert tm % SEL_KT == 0
    tiles_per_batch = s // tm
    cos_t, sin_t = _rope_tables(jnp.arange(s, dtype=jnp.int32))
    wc = jnp.concatenate([lw['c_ws'][g] for g in range(CMLP_GROUPS)], axis=1).astype(BF16)
    bc = jnp.repeat(lw['c_bs'].T, HEAD_DIM, axis=1)
    rows_t, win, qb, kvb, small, mqk, mv, mo, ocm, _, cmp_rows = _inproj(
        x2d, mod, lw['g1'], lw['wp'], lw['bp'], lw['wsw'], lw['bsw'], cos_t, sin_t, wc, bc, lw['cg'],
        tm=tm, tiles_per_batch=tiles_per_batch, tc=CMLP_CHUNK, q_dtype=BF16, rows_feature_major=True)
    nb = s // CMP_BLOCK
    comp = _compress(cmp_rows.reshape(b, s, 2 * KV_W), lw['a_tile'], lw['w_bd'], rows_per_step=min(s, 1024))
    comp = comp.reshape(b, nb // 2, 2, 2 * KV_W).transpose(0, 2, 1, 3).reshape(b, nb, 2 * KV_W)
    onsa = _nsa_prompt(qb, small, comp[..., 0:KV_W], comp[..., KV_W:], kvb.reshape(b, s, 768), b=b, s=s, kt=SEL_KT)
    zeros = lambda *sh: jnp.zeros(sh, F32)
    ext0, m0 = _mlstm_state_in(zeros(b, MLSTM_HEADS, HEAD_DIM, HEAD_DIM), zeros(b, MLSTM_HEADS, HEAD_DIM),
                               zeros(b, MLSTM_HEADS))
    small3 = small.reshape(b, s, LANE)
    smallt = jnp.swapaxes(small3[:, :, 24:32], 1, 2)
    hm, ext, mout = _mlstm(mqk.reshape(b, s, 256), mv.reshape(b, s, 256), mo.reshape(b, s, 256), small3, smallt,
                           zeros(b, 8, 256), ext0, m0, lw['m_cw'], lw['m_cb'], lw['wq'], lw['wk'], lw['ng'], tl=tm)
    c1, n1, m1 = _mlstm_state_out(ext, mout)
    y, fst = _post(x2d, onsa, hm.reshape(b * s, 256), ocm, mod, lw['wout'], lw['g2'], lw['wup'], lw['f_cw'],
                   lw['f_cb'], lw['wdn'], fg, tm=tm, tiles_per_batch=tiles_per_batch, final=final)
    n_keep = min(WINDOW, s)
    kv_rows = jnp.transpose(rows_t.reshape(b, 4, NSA_KV_HEADS, HEAD_DIM, s), (0, 4, 1, 2, 3))
    new_win = win.reshape(b, s, 256)[:, s - n_keep:].reshape(b, n_keep, 2, NSA_KV_HEADS, HEAD_DIM)
    state = (kv_rows, new_win,
             c1, n1, m1, mqk.reshape(b, s, 256)[:, s - (MLSTM_CONV - 1):], fst[:, 8 - (FFN_CONV - 1):])
    return y, state


def _layer_sample(x2d, mod, lw, fg, cache_t, page_table, wcache_t, mc, mn, mm, mconv, fconv, *, layer, b, s, final,
                  kt=512):
    t = b * s
    past = page_table.shape[1] * PAGE_SIZE
    pos = past + jnp.arange(s, dtype=jnp.int32)
    cos1, sin1 = _rope_tables(pos)
    cos_t, sin_t = jnp.tile(cos1, (b, 1)), jnp.tile(sin1, (b, 1))
    eye_b = jnp.eye(b, dtype=F32)
    wc = jnp.concatenate([jnp.kron(eye_b, lw['c_ws'][g][:s, :s]) for g in range(CMLP_GROUPS)], axis=1).astype(BF16)
    bc = jnp.tile(jnp.repeat(lw['c_bs'].T[:s], HEAD_DIM, axis=1), (b, 1))
    rows, win, qf, _, small, mqk, mv, mo, ocm, cvn = _inproj(
        x2d, mod, lw['g1'], lw['wp'], lw['bp'], lw['wsw'], lw['bsw'], cos_t, sin_t, wc, bc, lw['cg'],
        tm=t, tiles_per_batch=1, tc=t, q_dtype=F32, rows_feature_major=False)
    nbc = past // CMP_BLOCK
    pair = jnp.repeat(jnp.eye(nbc // 2, dtype=BF16), 2, axis=0)
    n_keep = wcache_t.shape[3]
    wnew3 = win.reshape(b, s, 256)
    a_t = jnp.tile(lw['a_tile'].T, (1, PAGE_SIZE // CMP_BLOCK))
    onsa = _nsa_sample(page_table, qf.reshape(b, s, 512), small.reshape(b, s, LANE), rows.reshape(b, s, 512), wnew3,
                       wcache_t, a_t, lw['w_bd'].T, pair, cache_t, layer=layer, kt=kt).reshape(t, 512)
    L = MLSTM_L
    pad3 = lambda a, fill=0.0: jnp.pad(a.reshape(b, s, -1), ((0, 0), (0, L - s), (0, 0)), constant_values=fill)
    small3 = small.reshape(b, s, LANE)
    lane = jnp.arange(LANE)
    small_p = jnp.where((lane >= 24) & (lane < 28), pad3(small3, NEG), pad3(small3))
    smallt = jnp.swapaxes(small_p[:, :, 24:32], 1, 2)
    hist = jnp.pad(mconv, ((0, 0), (8 - (MLSTM_CONV - 1), 0), (0, 0)))
    ext0, m0 = _mlstm_state_in(mc, mn, mm)
    hm, ext, mout = _mlstm(pad3(mqk), pad3(mv), pad3(mo), small_p, smallt, hist, ext0, m0, lw['m_cw'], lw['m_cb'],
                           lw['wq'], lw['wk'], lw['ng'], tl=L)
    c1, n1, m1 = _mlstm_state_out(ext, mout)
    hm = hm[:, :s].reshape(t, 256)
    h1 = jnp.zeros((b, s, 2 * D_FF), F32).at[:, 0].set(fconv[:, 1]).reshape(t, 2 * D_FF)
    h2 = jnp.zeros((b, s, 2 * D_FF), F32).at[:, 0].set(fconv[:, 0]).at[:, 1].set(fconv[:, 1]).reshape(t, 2 * D_FF)
    y, up = _post(x2d, onsa, hm, ocm, mod, lw['wout'], lw['g2'], lw['wup'], lw['f_cw'], lw['f_cb'], lw['wdn'], fg,
                  h1, h2, tm=t, tiles_per_batch=1, final=final, seg_len=s)
    new_win_t = jnp.concatenate([wcache_t[layer][:, :, s:], jnp.swapaxes(wnew3, 1, 2)], axis=2)
    new_win = jnp.transpose(new_win_t.reshape(b, 2, NSA_KV_HEADS, HEAD_DIM, n_keep), (0, 4, 1, 2, 3))
    state = (rows.reshape(b, s, 4, NSA_KV_HEADS, HEAD_DIM), new_win, c1, n1, m1,
             mqk.reshape(b, s, 256)[:, s - (MLSTM_CONV - 1):], cvn.reshape(b, s, CMLP_W),
             up.reshape(b, s, 2 * D_FF)[:, s - (FFN_CONV - 1):])
    return y, state


def kernel(x_prompt, x_sample, cache_nsa_kv, page_table, cache_win_kv, state_mlstm_C, state_mlstm_n, state_mlstm_m, state_mlstm_conv, state_ffn_conv, c_prompt, c_sample, ada_w, ada_b, norm1_g, norm2_g, w_in, b_in, cmp_a, cmp_w, m_conv_w, m_conv_b, m_wq, m_wk, m_norm_g, c_norm_g, c_ws, c_bs, w_out, w_up, f_conv_w, f_conv_b, w_down, final_norm_g):
    bp, sp, _ = x_prompt.shape
    bs, ss, _ = x_sample.shape
    depth = ada_w.shape[0]
    w = dict(w_in=w_in, b_in=b_in, norm1_g=norm1_g, norm2_g=norm2_g, cmp_a=cmp_a, cmp_w=cmp_w, m_conv_w=m_conv_w,
             m_conv_b=m_conv_b, m_wq=m_wq, m_wk=m_wk, m_norm_g=m_norm_g, c_norm_g=c_norm_g, c_ws=c_ws, c_bs=c_bs,
             w_out=w_out, w_up=w_up, f_conv_w=f_conv_w, f_conv_b=f_conv_b, w_down=w_down)
    n_c = bp + bs
    n_pad = -(-n_c // 8) * 8
    c_all = jnp.pad(jnp.concatenate([c_prompt, c_sample], axis=0), ((0, n_pad - n_c), (0, 0)))
    mod_all = _ada(c_all, ada_w, ada_b)
    fg = final_norm_g[None, :]
    xp = x_prompt.reshape(bp * sp, D_MODEL)
    xs = x_sample.reshape(bs * ss, D_MODEL)
    n_pool = cache_nsa_kv.shape[1]
    n_keep = cache_win_kv.shape[2]
    cache_t = jnp.transpose(cache_nsa_kv, (0, 1, 3, 4, 5, 2)).reshape(depth, n_pool, 512, PAGE_SIZE)
    wcache_t = jnp.transpose(cache_win_kv, (0, 1, 3, 4, 5, 2)).reshape(depth, bs, 256, n_keep)
    st_p, st_s = [], []
    for l in range(depth):
        lw = _layer_weights(l, w)
        final = l == depth - 1
        mod_p = mod_all[l, 0:bp][:, None, :]
        mod_s = jnp.repeat(mod_all[l, bp:bp + bs], ss, axis=0)[None]
        xp, sp_l = _layer_prompt(xp, mod_p, lw, fg, b=bp, s=sp, final=final)
        xs, ss_l = _layer_sample(xs, mod_s, lw, fg, cache_t, page_table, wcache_t, state_mlstm_C[l],
                                 state_mlstm_n[l], state_mlstm_m[l], state_mlstm_conv[l], state_ffn_conv[l],
                                 layer=l, b=bs, s=ss, final=final)
        st_p.append(sp_l)
        st_s.append(ss_l)
    sp_st = [jnp.stack(t) for t in zip(*st_p)]
    ss_st = [jnp.stack(t) for t in zip(*st_s)]
    y_prompt = xp.reshape(bp, sp, D_MODEL)
    y_sample = xs.reshape(bs, ss, D_MODEL)
    return (y_prompt, y_sample, sp_st[0], ss_st[0], sp_st[1], ss_st[1],
            sp_st[2], sp_st[3], sp_st[4], sp_st[5],
            ss_st[2], ss_st[3], ss_st[4], ss_st[5],
            ss_st[6], sp_st[6], ss_st[7])
```

```python
import functools

import numpy as np
import jax
import jax.numpy as jnp
from jax import lax
from jax.experimental import pallas as pl
from jax.experimental.pallas import tpu as pltpu

F32 = jnp.float32
BF16 = jnp.bfloat16

D_MODEL = 1024
HEAD_DIM = 64
NSA_HEADS = 8
NSA_KV_HEADS = 2
NSA_GROUP = 4
CMP_BLOCK = 32
SEL_BLOCK = 64
SEL_TOPN = 16
WINDOW = 512
ROPE_THETA = 10000.0
MLSTM_HEADS = 4
MLSTM_CONV = 4
CMLP_GROUPS = 4
CMLP_CHUNK = 128
D_FF = 2816
FFN_CONV = 3
PAGE_SIZE = 128
NSA_W = 512
KV_W = 128
MLSTM_W = 256
CMLP_W = 256
IN_SIZES = (NSA_W, 2 * KV_W, 2 * KV_W, 2 * KV_W, 3 * NSA_HEADS, MLSTM_W, MLSTM_W, MLSTM_W, MLSTM_HEADS,
            MLSTM_HEADS, CMLP_W, CMLP_W)
IN_OFF = tuple(int(v) for v in np.cumsum((0,) + IN_SIZES))
IN_W = IN_OFF[-1]
EPS = 1e-6
NEG = -1e30
FORCE = 1e4
LANE = 128
NP_COLS = 2688
VMEM_LIMIT = 56 * 1024 * 1024
PAGES_PER_STEP = 32
NSA_TQ = 256
SEL_KT = 256
Q_SCALE = HEAD_DIM ** -0.5 * 1.4426950408889634
MLSTM_L = 128


def _dot(a, b):
    return jnp.dot(a, b, preferred_element_type=F32)


def _dot_nt(a, b):
    return lax.dot_general(a, b, (((1,), (1,)), ((), ())), preferred_element_type=F32)


def _dot01(m01, x):
    hi = x.astype(BF16)
    r1 = x - hi.astype(F32)
    mid = r1.astype(BF16)
    lo = (r1 - mid.astype(F32)).astype(BF16)
    return _dot(m01, hi) + _dot(m01, mid) + _dot(m01, lo)


def _dot01_r(x, m01):
    hi = x.astype(BF16)
    r1 = x - hi.astype(F32)
    mid = r1.astype(BF16)
    lo = (r1 - mid.astype(F32)).astype(BF16)
    return _dot(hi, m01) + _dot(mid, m01) + _dot(lo, m01)


def _sigmoid(x):
    return 1.0 / (1.0 + jnp.exp(-x))


def _silu(x):
    return x * _sigmoid(x)


def _gelu_tanh(x):
    return 0.5 * x * (1.0 + jnp.tanh(0.7978845608028654 * (x + 0.044715 * (x * x * x))))


def _log_sigmoid(x):
    return jnp.minimum(x, 0.0) - jnp.log(1.0 + jnp.exp(-jnp.abs(x)))


def _const_spec(shape):
    nd = len(shape)
    return pl.BlockSpec(shape, lambda *_: (0,) * nd)


def _params(sem):
    return pltpu.CompilerParams(dimension_semantics=sem, vmem_limit_bytes=VMEM_LIMIT)


def _ada_kernel(c_ref, w_ref, b_ref, o_ref):
    c = c_ref[...]
    o_ref[0] = _dot(_silu(c).astype(BF16), w_ref[0]) + b_ref[0]


def _ada(c_all, ada_w, ada_b):
    depth = ada_w.shape[0]
    n = c_all.shape[0]
    return pl.pallas_call(
        _ada_kernel,
        grid=(depth, 6),
        in_specs=[pl.BlockSpec((n, D_MODEL), lambda l, j: (0, 0)),
                  pl.BlockSpec((1, D_MODEL, D_MODEL), lambda l, j: (l, 0, j)),
                  pl.BlockSpec((1, 1, D_MODEL), lambda l, j: (l, 0, j))],
        out_specs=pl.BlockSpec((1, n, D_MODEL), lambda l, j: (l, 0, j)),
        out_shape=jax.ShapeDtypeStruct((depth, n, 6 * D_MODEL), F32),
        compiler_params=_params(("arbitrary", "arbitrary")),
        name="ada",
    )(c_all, ada_w.astype(BF16), ada_b.reshape(depth, 1, 6 * D_MODEL))


def _inproj_kernel(x_ref, mod_ref, g_ref, w_ref, b_ref, cos_ref, sin_ref, wc_ref, bc_ref, cg_ref,
                   rows_ref, win_ref, q_ref, kvb_ref, small_ref, mqk_ref, mv_ref, mo_ref, ocm_ref, cvn_ref,
                   *cmp_rows_ref, tc, tiles_per_batch):
    x = x_ref[...]
    tm = x.shape[0]
    xn = x * lax.rsqrt(jnp.mean(x * x, axis=-1, keepdims=True) + EPS) * g_ref[...]
    mod = mod_ref[0]
    h = xn * (1.0 + mod[:, D_MODEL:2 * D_MODEL]) + mod[:, 0:D_MODEL]
    hb = h.astype(BF16)
    z = _dot(hb, w_ref[...]) + b_ref[...]
    cos = cos_ref[...]
    sin = sin_ref[...]
    first_half = lax.broadcasted_iota(jnp.int32, (tm, LANE), 1) % HEAD_DIM < HEAD_DIM // 2

    def rope(c0):
        xs = z[:, c0:c0 + LANE]
        swapped = jnp.where(first_half, pltpu.roll(xs, LANE - HEAD_DIM // 2, axis=1),
                            pltpu.roll(xs, HEAD_DIM // 2, axis=1))
        return xs * cos + swapped * sin

    for j in range(4):
        q_ref[:, j * LANE:(j + 1) * LANE] = (rope(j * LANE) * Q_SCALE).astype(q_ref.dtype)
    kc = rope(512)
    vc = z[:, 640:768]
    ks = rope(768)
    vs = z[:, 896:1024]
    kw = rope(1024)
    vw = z[:, 1152:1280]
    if cmp_rows_ref:
        rows_ref[0] = jnp.transpose(jnp.concatenate([kc, vc, ks, vs], axis=1))
        cmp_rows_ref[0][:, 0:128] = kc
        cmp_rows_ref[0][:, 128:256] = vc
    else:
        rows_ref[:, 0:128] = kc
        rows_ref[:, 128:256] = vc
        rows_ref[:, 256:384] = ks
        rows_ref[:, 384:512] = vs
    win_ref[:, 0:128] = kw
    win_ref[:, 128:256] = vw
    lane_k = lax.broadcasted_iota(jnp.int32, (tm, LANE), 1)
    row0 = (pl.program_id(0) % tiles_per_batch) * tm
    blk = ((row0 + lax.broadcasted_iota(jnp.int32, (tm, LANE), 0)) // SEL_BLOCK) % HEAD_DIM
    low = lane_k < HEAD_DIM
    kvb_ref[:, 0:128] = jnp.where(low, ks, jnp.where(lane_k == HEAD_DIM + blk, 1.0, 0.0)).astype(BF16)
    kvb_ref[:, 128:256] = jnp.where(low, jnp.where(lane_k == blk, 1.0, 0.0), ks).astype(BF16)
    kvb_ref[:, 256:384] = jnp.where(low, vs, 1.0).astype(BF16)
    kvb_ref[:, 384:512] = jnp.where(low, 1.0, vs).astype(BF16)
    kvb_ref[:, 512:640] = kw.astype(BF16)
    kvb_ref[:, 640:768] = vw.astype(BF16)
    sm = z[:, 2560:2688]
    lane = lax.broadcasted_iota(jnp.int32, sm.shape, 1)
    small_ref[...] = jnp.where(lane < 24, _sigmoid(sm), jnp.where(lane < 28, sm, _log_sigmoid(sm)))
    mqk_ref[...] = z[:, 1280:1536]
    mv_ref[...] = z[:, 1536:1792].astype(BF16)
    mo_ref[...] = _sigmoid(z[:, 1792:2048])
    cu = _gelu_tanh(z[:, 2048:2304])
    cv = _gelu_tanh(z[:, 2304:2560])
    cvn = cv * lax.rsqrt(jnp.mean(cv * cv, axis=-1, keepdims=True) + EPS) * cg_ref[...]
    cvn_ref[...] = cvn
    lane_c = lax.broadcasted_iota(jnp.int32, (tc, CMLP_W), 1) // HEAD_DIM
    for c in range(tm // tc):
        v = cvn[c * tc:(c + 1) * tc]
        vbd = jnp.concatenate([jnp.where(lane_c == g, v, 0.0) for g in range(CMLP_GROUPS)], axis=0).astype(BF16)
        s = _dot(wc_ref[...], vbd) + bc_ref[...]
        ocm_ref[c * tc:(c + 1) * tc, :] = (cu[c * tc:(c + 1) * tc] * s).astype(BF16)


def _inproj(x2d, mod, g1, wp, bp, cos_t, sin_t, wc, bc, cg, *, tm, tiles_per_batch, tc, q_dtype,
            rows_feature_major):
    t_total = x2d.shape[0]
    n_tiles = t_total // tm
    r = mod.shape[1]
    row = lambda w: pl.BlockSpec((tm, w), lambda i: (i, 0))
    if rows_feature_major:
        rows_shape = jax.ShapeDtypeStruct((n_tiles // tiles_per_batch, 512, tiles_per_batch * tm), F32)
        rows_spec = pl.BlockSpec((1, 512, tm), lambda i: (i // tiles_per_batch, 0, i % tiles_per_batch))
    else:
        rows_shape = jax.ShapeDtypeStruct((t_total, 512), F32)
        rows_spec = row(512)
    out_shapes = (
        rows_shape,
        jax.ShapeDtypeStruct((t_total, 256), F32),
        jax.ShapeDtypeStruct((t_total, 512), q_dtype),
        jax.ShapeDtypeStruct((t_total, 768), BF16),
        jax.ShapeDtypeStruct((t_total, 128), F32),
        jax.ShapeDtypeStruct((t_total, 256), F32),
        jax.ShapeDtypeStruct((t_total, 256), BF16),
        jax.ShapeDtypeStruct((t_total, 256), F32),
        jax.ShapeDtypeStruct((t_total, 256), BF16),
        jax.ShapeDtypeStruct((t_total, 256), F32),
    )
    out_specs = [rows_spec, row(256), row(512), row(768), row(128), row(256), row(256), row(256), row(256), row(256)]
    if rows_feature_major:
        out_shapes += (jax.ShapeDtypeStruct((t_total, 256), F32),)
        out_specs.append(row(256))
    return pl.pallas_call(
        functools.partial(_inproj_kernel, tc=tc, tiles_per_batch=tiles_per_batch),
        grid=(n_tiles,),
        in_specs=[row(D_MODEL),
                  pl.BlockSpec((1, r, 6 * D_MODEL), lambda i: (i // tiles_per_batch, 0, 0)),
                  _const_spec((1, D_MODEL)),
                  _const_spec((D_MODEL, NP_COLS)), _const_spec((1, NP_COLS)),
                  pl.BlockSpec((tm, LANE), lambda i: (i % tiles_per_batch, 0)),
                  pl.BlockSpec((tm, LANE), lambda i: (i % tiles_per_batch, 0)),
                  _const_spec(wc.shape), _const_spec(bc.shape), _const_spec((1, CMLP_W))],
        out_specs=out_specs,
        out_shape=out_shapes,
        compiler_params=_params(("arbitrary",)),
        name="inproj",
    )(x2d, mod, g1, wp, bp, cos_t, sin_t, wc, bc, cg)


def _compress_kernel(rows_ref, a_ref, w_ref, o_ref):
    x = rows_ref[0]
    n = x.shape[0] // CMP_BLOCK
    pooled = jnp.sum(x.reshape(n, CMP_BLOCK, 2 * KV_W) * a_ref[...][None], axis=1)
    o_ref[0] = _dot(pooled.astype(BF16), w_ref[...])


def _compress(rows3, a_tile, w_bd, *, rows_per_step):
    b, s, _ = rows3.shape
    nb = s // CMP_BLOCK
    steps = s // rows_per_step
    return pl.pallas_call(
        _compress_kernel,
        grid=(b, steps),
        in_specs=[pl.BlockSpec((1, rows_per_step, 2 * KV_W), lambda i, j: (i, j, 0)),
                  _const_spec((CMP_BLOCK, 2 * KV_W)), _const_spec((2 * KV_W, 2 * KV_W))],
        out_specs=pl.BlockSpec((1, rows_per_step // CMP_BLOCK, 2 * KV_W), lambda i, j: (i, j, 0)),
        out_shape=jax.ShapeDtypeStruct((b, nb, 2 * KV_W), F32),
        compiler_params=_params(("arbitrary", "arbitrary")),
        name="compress",
    )(rows3, a_tile, w_bd)


def _group_queries(qf, g):
    lane = lax.broadcasted_iota(jnp.int32, (qf.shape[0], LANE), 1)
    keep = (lane >= HEAD_DIM * g) & (lane < HEAD_DIM * (g + 1))
    out = []
    for r in range(NSA_GROUP):
        h = NSA_GROUP * g + r
        slab = qf[:, (h // 2) * LANE:(h // 2 + 1) * LANE]
        if h % 2 != g:
            slab = pltpu.roll(slab, HEAD_DIM, axis=1)
        out.append(jnp.where(keep, slab, 0.0))
    return out


def _place_heads(pieces, g):
    lane = lax.broadcasted_iota(jnp.int32, pieces[0].shape, 1)
    slabs = []
    for j in range(2):
        halves = []
        for half in range(2):
            p = pieces[2 * j + half]
            if half != g:
                p = pltpu.roll(p, HEAD_DIM, axis=1)
            halves.append(p)
        slabs.append(jnp.where(lane < HEAD_DIM, halves[0], halves[1]))
    return slabs


def _softmax_rows(s, allowed):
    sm = jnp.where(allowed, s, NEG)
    m = jnp.max(sm, axis=-1, keepdims=True)
    e = jnp.where(allowed, jnp.exp2(sm - m), 0.0)
    l = jnp.sum(e, axis=-1, keepdims=True)
    return e * (1.0 / jnp.maximum(l, 1e-30))


def _topk_select(vs, jrow, n_rounds):
    big = float(vs[0].shape[0] + 1)

    def body(_, carry):
        out = []
        for v in carry:
            m = jnp.max(v, axis=0, keepdims=True)
            first = jnp.min(jnp.where(v == m, jrow, big), axis=0, keepdims=True)
            out.append(jnp.where(jrow == first, -jnp.inf, v))
        return tuple(out)

    res = lax.fori_loop(0, n_rounds, body, tuple(vs))
    return [jnp.where((r == -jnp.inf) & (v > 0.5 * NEG), 1.0, 0.0) for r, v in zip(res, vs)]


def _expand_tile(sel_b, t, kt):
    nblk = sel_b.shape[1]
    jrow = lax.broadcasted_iota(jnp.int32, (nblk, kt), 0) - t * (kt // SEL_BLOCK)
    cdiv = lax.broadcasted_iota(jnp.int32, (nblk, kt), 1) // SEL_BLOCK
    e = jnp.where(jrow == cdiv, 1.0, 0.0).astype(BF16)
    return _dot(sel_b, e)


def _nsa_prompt_kernel(q_ref, small_ref, kc_ref, vc_ref, kvb_ref, o_ref, m_scr, acc_scr, lhs_scr, s_scr, park_scr,
                       qz_scr, *, kt, nbs):
    i = pl.program_id(1)
    tq = q_ref.shape[0]
    nbc = kc_ref.shape[1]
    qf = q_ref[...].astype(F32)
    q0 = i * tq
    qpos_c = q0 + lax.broadcasted_iota(jnp.int32, (tq, 1), 0)
    qpos_r = q0 + lax.broadcasted_iota(jnp.int32, (1, tq), 1)
    eye = jnp.where(lax.broadcasted_iota(jnp.int32, (tq, tq), 0) == lax.broadcasted_iota(jnp.int32, (tq, tq), 1),
                    1.0, 0.0).astype(BF16)
    qpos_r4 = q0 + lax.broadcasted_iota(jnp.int32, (1, NSA_GROUP * tq), 1) % tq
    kc = kc_ref[0].astype(BF16)
    vc = vc_ref[0].astype(BF16)
    half = nbc // 2
    rho_c = lax.broadcasted_iota(jnp.int32, (nbc, 1), 0)
    end_c = jnp.where(rho_c < half, 2 * rho_c, 2 * (rho_c - half) + 1) * CMP_BLOCK + (CMP_BLOCK - 1)
    rho_r = lax.broadcasted_iota(jnp.int32, (1, nbc), 1)
    end_r = jnp.where(rho_r < half, 2 * rho_r, 2 * (rho_r - half) + 1) * CMP_BLOCK + (CMP_BLOCK - 1)
    w_start = pl.multiple_of(jnp.maximum(q0 - WINDOW, 0), LANE)
    w_len = WINDOW + tq
    wpos = w_start + lax.broadcasted_iota(jnp.int32, (1, w_len), 1)
    bias_w = jnp.where((wpos <= qpos_c) & (wpos > qpos_c - WINDOW), 0.0, NEG)
    bias_c = jnp.where(end_r <= qpos_c, 0.0, NEG)
    any_c = qpos_c >= CMP_BLOCK - 1
    k_w = kvb_ref[0, pl.ds(w_start, w_len), 512:640]
    v_w = kvb_ref[0, pl.ds(w_start, w_len), 640:768]

    def attend(q_b, k, v_ones, bias):
        s = _dot_nt(q_b, k) + bias
        p = jnp.exp2(s - jnp.max(s, axis=-1, keepdims=True))
        o = _dot(p.astype(BF16), v_ones)
        return o * (1.0 / pltpu.roll(o, HEAD_DIM, axis=1))

    lane_g = lax.broadcasted_iota(jnp.int32, (1, LANE), 1)
    one_b = jnp.ones((), BF16)
    jrow_i = lax.broadcasted_iota(jnp.int32, (nbs, tq), 0)
    for g in range(NSA_KV_HEADS):
        qz4 = _group_queries(qf, g)
        qz_b = [x.astype(BF16) for x in qz4]
        qz = jnp.concatenate(qz_b, axis=0)
        keep_g = (lane_g >= HEAD_DIM * g) & (lane_g < HEAD_DIM * (g + 1))
        vc_x = jnp.where(keep_g, vc, one_b)
        o_c = [jnp.where(any_c, attend(qz_b[r], kc, vc_x, bias_c), 0.0) for r in range(NSA_GROUP)]
        s_t = _dot_nt(kc, qz)
        sm = s_t + jnp.where(end_c <= qpos_r4, 0.0, NEG)
        et = jnp.exp2(sm - jnp.max(sm, axis=0, keepdims=True))
        pt = et * (1.0 / jnp.sum(et, axis=0, keepdims=True))
        ph = pt[:, 0:tq] + pt[:, tq:2 * tq] + pt[:, 2 * tq:3 * tq] + pt[:, 3 * tq:4 * tq]
        imp = jnp.where(qpos_r >= CMP_BLOCK - 1, ph[0:half] + ph[half:nbc], 0.0)
        if nbs > half:
            imp = jnp.concatenate([imp, jnp.zeros((nbs - half, tq), F32)], axis=0)
        forced = (jrow_i == qpos_r // SEL_BLOCK) | (jrow_i == 0)
        valid = jrow_i * SEL_BLOCK <= qpos_r
        score = jnp.where(valid, imp + jnp.where(forced, FORCE, 0.0), NEG)
        for r in range(NSA_GROUP):
            park_scr[r] = o_c[r]
            qz_scr[r] = qz_b[r]
        sel_t = _topk_select([score], jrow_i.astype(F32), min(SEL_TOPN, nbs))[0]
        m_scr[...] = jnp.full(m_scr.shape, NEG, F32)
        acc_scr[...] = jnp.zeros(acc_scr.shape, F32)
        sel_q = _dot_nt(eye, sel_t.astype(BF16))
        if nbs < LANE:
            sel_q = jnp.concatenate([sel_q, jnp.zeros((tq, LANE - nbs), F32)], axis=1)
        negsel = (sel_q - 1.0) * 1e30
        base = HEAD_DIM * (1 - g)
        lane_q = lax.broadcasted_iota(jnp.int32, (tq, LANE), 1)
        feat_mask = (lane_q >= base) & (lane_q < base + HEAD_DIM)
        for hf in range(max(1, nbs // HEAD_DIM)):
            feat = (negsel if HEAD_DIM * hf == base else pltpu.roll(negsel, HEAD_DIM, axis=1)).astype(BF16)
            for r in range(NSA_GROUP):
                lhs_scr[hf, r] = jnp.where(feat_mask, feat, qz_b[r])
        tiles_per_half = HEAD_DIM * SEL_BLOCK // kt

        def scores(t, slot):
            k_t = kvb_ref[0, pl.ds(pl.multiple_of(t * kt, kt), kt), g * LANE:(g + 1) * LANE]
            hf = t // tiles_per_half
            for r in range(NSA_GROUP):
                s_scr[slot, r] = _dot_nt(lhs_scr[hf, r], k_t)

        def consume(t, slot, causal):
            k0 = pl.multiple_of(t * kt, kt)
            v_t = kvb_ref[0, pl.ds(k0, kt), (2 + g) * LANE:(3 + g) * LANE]
            if causal:
                kpos = k0 + lax.broadcasted_iota(jnp.int32, (1, kt), 1)
                bias = jnp.where(kpos <= qpos_c, 0.0, NEG)
            for r in range(NSA_GROUP):
                s = s_scr[slot, r]
                if causal:
                    s = s + bias
                m_old = m_scr[r]
                m_new = jnp.maximum(m_old, jnp.max(s, axis=-1, keepdims=True))
                p = jnp.exp2(s - jnp.concatenate([m_new] * (kt // LANE), axis=1))
                acc_scr[r] = jnp.exp2(m_old - m_new) * acc_scr[r] + _dot(p.astype(BF16), v_t)
                m_scr[r] = m_new

        n_full = q0 // kt
        scores(0, 0)

        def pair_body(tp, _):
            scores(2 * tp + 1, 1)
            consume(2 * tp, 0, False)
            scores(2 * tp + 2, 0)
            consume(2 * tp + 1, 1, False)
            return 0

        lax.fori_loop(0, n_full // 2, pair_body, 0)

        @pl.when(n_full % 2 == 1)
        def _odd():
            scores(n_full, 1)
            consume(n_full - 1, 0, False)
            consume(n_full, 1, True)

        @pl.when(n_full % 2 == 0)
        def _even():
            consume(n_full, 0, True)
        acc = acc_scr[...].reshape(NSA_GROUP * tq, LANE)
        o_s = acc * (1.0 / jnp.maximum(pltpu.roll(acc, HEAD_DIM, axis=1), 1e-30))
        v_wx = jnp.where(keep_g, v_w, one_b)
        pieces = []
        for r in range(NSA_GROUP):
            h = NSA_GROUP * g + r
            sl = slice(r * tq, (r + 1) * tq)
            o_w = attend(qz_scr[r], k_w, v_wx, bias_w)
            pieces.append(small_ref[:, 3 * h:3 * h + 1] * park_scr[r] + small_ref[:, 3 * h + 1:3 * h + 2] * o_s[sl]
                          + small_ref[:, 3 * h + 2:3 * h + 3] * o_w)
        for j, slab in enumerate(_place_heads(pieces, g)):
            o_ref[:, (2 * g + j) * LANE:(2 * g + j + 1) * LANE] = slab.astype(o_ref.dtype)


def _nsa_prompt(qb, small, kcp, vcp, kvb3, *, b, s, kt):
    tq = NSA_TQ
    nq = s // tq
    nbc = kcp.shape[1]
    nbs = s // SEL_BLOCK
    assert s >= WINDOW + tq and s % kt == 0 and nbc == 2 * nbs and nbs <= LANE and kt == SEL_KT
    return pl.pallas_call(
        functools.partial(_nsa_prompt_kernel, kt=kt, nbs=nbs),
        grid=(b, nq),
        in_specs=[pl.BlockSpec((tq, 512), lambda bi, i: (bi * nq + i, 0)),
                  pl.BlockSpec((tq, LANE), lambda bi, i: (bi * nq + i, 0)),
                  pl.BlockSpec((1, nbc, LANE), lambda bi, i: (bi, 0, 0)),
                  pl.BlockSpec((1, nbc, LANE), lambda bi, i: (bi, 0, 0)),
                  pl.BlockSpec((1, s, 768), lambda bi, i: (bi, 0, 0))],
        out_specs=pl.BlockSpec((tq, 512), lambda bi, i: (bi * nq + i, 0)),
        out_shape=jax.ShapeDtypeStruct((b * s, 512), BF16),
        scratch_shapes=[pltpu.VMEM((NSA_GROUP, tq, LANE), F32), pltpu.VMEM((NSA_GROUP, tq, LANE), F32),
                        pltpu.VMEM((max(1, nbs // HEAD_DIM), NSA_GROUP, tq, LANE), BF16),
                        pltpu.VMEM((2, NSA_GROUP, tq, kt), F32),
                        pltpu.VMEM((NSA_GROUP, tq, LANE), F32), pltpu.VMEM((NSA_GROUP, tq, LANE), BF16)],
        compiler_params=_params(("arbitrary", "arbitrary")),
        name="nsa_prompt",
    )(qb, small, kcp, vcp, kvb3)


def _nsa_sample_kernel(pt_ref, q_ref, small_ref, rows_ref, wnew_ref, wcache_ref, a_ref, w_ref, pair_ref, *rest,
                       n_pp, kt):
    page_refs = rest[:n_pp]
    o_ref = rest[n_pp]
    ksel, vsel, pooled, s_scr = rest[n_pp + 1:]
    pg = pl.program_id(1)
    npg = pooled.shape[0]
    past = ksel.shape[1]
    nblk = past // SEL_BLOCK
    a_t = a_ref[...]
    per_page = PAGE_SIZE // CMP_BLOCK
    t_i = lax.broadcasted_iota(jnp.int32, (PAGE_SIZE, per_page * n_pp), 0) // CMP_BLOCK
    n_i = lax.broadcasted_iota(jnp.int32, (PAGE_SIZE, per_page * n_pp), 1)
    acc_p = jnp.zeros((2 * KV_W, per_page * n_pp), F32)
    for k in range(n_pp):
        page = page_refs[k][0, 0]
        xa = page[0:2 * KV_W, :] * a_t
        seg = jnp.where(n_i == t_i + per_page * k, 1.0, 0.0).astype(BF16)
        hi = xa.astype(BF16)
        lo = (xa - hi.astype(F32)).astype(BF16)
        acc_p = acc_p + _dot(hi, seg) + _dot(lo, seg)
        c0 = pl.multiple_of((pg * n_pp + k) * PAGE_SIZE, PAGE_SIZE)
        ksel[:, pl.ds(c0, PAGE_SIZE)] = page[256:384, :].astype(BF16)
        vsel[:, pl.ds(c0, PAGE_SIZE)] = page[384:512, :].astype(BF16)
    pooled[pg] = acc_p

    @pl.when(pg == pl.num_programs(1) - 1)
    def _finish():
        nq = q_ref.shape[1]
        cmp_t = [_dot(w_ref[...], pooled[j].astype(BF16)) for j in range(npg)]
        kc = jnp.concatenate([c[0:KV_W] for c in cmp_t], axis=1).astype(BF16)
        vc = jnp.concatenate([c[KV_W:2 * KV_W] for c in cmp_t], axis=1).astype(BF16)
        qf = q_ref[0]
        gates = small_ref[0]
        qg = [_group_queries(qf, g) for g in range(NSA_KV_HEADS)]
        blocks = []
        for r in range(NSA_GROUP):
            blocks += [qg[0][r], qg[1][r]]
        qz = jnp.concatenate(blocks, axis=0).astype(BF16)
        rows_n = qz.shape[0]
        rpr = NSA_KV_HEADS * nq
        qidx_c = lax.broadcasted_iota(jnp.int32, (rows_n, 1), 0) % nq
        s_c = _dot(qz, kc)
        e_c = jnp.exp2(s_c - jnp.max(s_c, axis=-1, keepdims=True))
        p_c = e_c * (1.0 / jnp.sum(e_c, axis=-1, keepdims=True))
        o_c = _dot_nt(p_c.astype(BF16), vc)
        p_sum = p_c[0:rpr] + p_c[rpr:2 * rpr] + p_c[2 * rpr:3 * rpr] + p_c[3 * rpr:4 * rpr]
        imp = _dot01_r(p_sum, pair_ref[...])
        jcol = lax.broadcasted_iota(jnp.int32, imp.shape, 1)
        score = imp + jnp.where(jcol == 0, FORCE, 0.0)
        sel32 = _topk_select_lanes(score, jcol.astype(F32), min(SEL_TOPN - 1, nblk))
        sel_b = jnp.concatenate([sel32] * NSA_GROUP, axis=0).astype(BF16)

        def scores(t, slot):
            k_t = ksel[:, pl.ds(pl.multiple_of(t * kt, kt), kt)]
            s_scr[slot] = _dot(qz, k_t) + jnp.where(_expand_tile(sel_b, t, kt) > 0.5, 0.0, NEG)

        def consume(t, slot, carry):
            m_old, l_old, acc = carry
            v_t = vsel[:, pl.ds(pl.multiple_of(t * kt, kt), kt)]
            s = s_scr[slot]
            m_new = jnp.maximum(m_old, jnp.max(s, axis=-1, keepdims=True))
            p = jnp.exp2(s - m_new)
            alpha = jnp.exp2(m_old - m_new)
            l_new = alpha * l_old + jnp.sum(p, axis=-1, keepdims=True)
            return m_new, l_new, alpha * acc + _dot_nt(p.astype(BF16), v_t)

        def pair_body(tp, carry):
            scores(2 * tp + 1, 1)
            carry = consume(2 * tp, 0, carry)
            scores(2 * tp + 2, 0)
            return consume(2 * tp + 1, 1, carry)

        n_t = past // kt
        scores(0, 0)
        init = (jnp.full((rows_n, 1), NEG, F32), jnp.zeros((rows_n, 1), F32), jnp.zeros((rows_n, LANE), F32))
        carry = lax.fori_loop(0, n_t // 2 - 1, pair_body, init)
        scores(n_t - 1, 1)
        carry = consume(n_t - 2, 0, carry)
        m_old, l_old, acc = consume(n_t - 1, 1, carry)
        newr = rows_ref[0]
        zk = jnp.zeros((LANE - nq, LANE), F32)
        k_n = jnp.concatenate([newr[:, 256:384], zk], axis=0).astype(BF16)
        v_n = jnp.concatenate([newr[:, 384:512], zk], axis=0).astype(BF16)
        ccol = lax.broadcasted_iota(jnp.int32, (1, LANE), 1)
        allow_n = (ccol <= qidx_c) & (ccol < nq)
        sm = jnp.where(allow_n, _dot_nt(qz, k_n), NEG)
        m_new = jnp.maximum(m_old, jnp.max(sm, axis=-1, keepdims=True))
        p = jnp.where(allow_n, jnp.exp2(sm - m_new), 0.0)
        alpha = jnp.exp2(m_old - m_new)
        l_new = alpha * l_old + jnp.sum(p, axis=-1, keepdims=True)
        acc = alpha * acc + _dot(p.astype(BF16), v_n)
        o_s = acc * (1.0 / jnp.maximum(l_new, 1e-30))
        wc = wcache_ref[0, 0]
        wn = wnew_ref[0]
        n_keep = wc.shape[1]
        k_wn = jnp.concatenate([wn[:, 0:KV_W], zk], axis=0).astype(BF16)
        v_wn = jnp.concatenate([wn[:, KV_W:2 * KV_W], zk], axis=0).astype(BF16)
        s_w = jnp.concatenate([_dot(qz, wc[0:KV_W].astype(BF16)), _dot_nt(qz, k_wn)], axis=1)
        wcol = lax.broadcasted_iota(jnp.int32, (1, n_keep + LANE), 1)
        allow_w = (((wcol < n_keep) & (wcol + (WINDOW - n_keep) > qidx_c))
                   | ((wcol >= n_keep) & (wcol - n_keep <= qidx_c) & (wcol - n_keep < nq)))
        p_w = _softmax_rows(s_w, allow_w).astype(BF16)
        o_w = _dot_nt(p_w[:, 0:n_keep], wc[KV_W:2 * KV_W].astype(BF16)) + _dot(p_w[:, n_keep:], v_wn)
        slabs = []
        for g in range(NSA_KV_HEADS):
            pieces = []
            for r in range(NSA_GROUP):
                h = NSA_GROUP * g + r
                sl = slice(r * rpr + g * nq, r * rpr + (g + 1) * nq)
                pieces.append(gates[:, 3 * h:3 * h + 1] * o_c[sl] + gates[:, 3 * h + 1:3 * h + 2] * o_s[sl]
                              + gates[:, 3 * h + 2:3 * h + 3] * o_w[sl])
            slabs.extend(_place_heads(pieces, g))
        for j in range(4):
            o_ref[0, :, j * LANE:(j + 1) * LANE] = slabs[j]


def _topk_select_lanes(v, jcol, n_rounds):
    big = float(v.shape[1] + 1)

    def body(_, r):
        m = jnp.max(r, axis=1, keepdims=True)
        first = jnp.min(jnp.where(r == m, jcol, big), axis=1, keepdims=True)
        return jnp.where(jcol == first, -jnp.inf, r)

    res = lax.fori_loop(0, n_rounds, body, v)
    return jnp.where((res == -jnp.inf) & (v > 0.5 * NEG), 1.0, 0.0)


def _nsa_sample(page_table, q3, small3, rows3, wnew3, wcache_t, a_t, w_bd_t, pair, cache_t, *, layer, kt):
    b, n_pages = page_table.shape
    nq = q3.shape[1]
    past = n_pages * PAGE_SIZE
    n_pp = PAGES_PER_STEP
    assert n_pages % n_pp == 0 and past % kt == 0 and nq % 8 == 0 and nq < CMP_BLOCK
    assert n_pp * PAGE_SIZE // CMP_BLOCK == LANE and (past // kt) % 2 == 0
    npg = n_pages // n_pp
    per = lambda w: pl.BlockSpec((1, nq, w), lambda bi, pg, pt: (bi, 0, 0))
    page_specs = [pl.BlockSpec((1, 1, 512, PAGE_SIZE), lambda bi, pg, pt, k=k: (layer, pt[bi, pg * n_pp + k], 0, 0))
                  for k in range(n_pp)]
    const = lambda shape: pl.BlockSpec(shape, lambda bi, pg, pt: (0,) * len(shape))
    grid_spec = pltpu.PrefetchScalarGridSpec(
        num_scalar_prefetch=1,
        grid=(b, npg),
        in_specs=[per(512), per(LANE), per(512), per(256),
                  pl.BlockSpec((1, 1, 256, wcache_t.shape[3]), lambda bi, pg, pt: (layer, bi, 0, 0)),
                  const(a_t.shape), const(w_bd_t.shape), const(pair.shape)] + page_specs,
        out_specs=per(512),
        scratch_shapes=[pltpu.VMEM((LANE, past), BF16), pltpu.VMEM((LANE, past), BF16),
                        pltpu.VMEM((npg, 2 * KV_W, LANE), F32),
                        pltpu.VMEM((2, NSA_HEADS * nq, kt), F32)],
    )
    return pl.pallas_call(
        functools.partial(_nsa_sample_kernel, n_pp=n_pp, kt=kt),
        grid_spec=grid_spec,
        out_shape=jax.ShapeDtypeStruct((b, nq, 512), F32),
        compiler_params=_params(("arbitrary", "arbitrary")),
        name="nsa_sample",
    )(page_table, q3, small3, rows3, wnew3, wcache_t, a_t, w_bd_t, pair, *([cache_t] * n_pp))


def _mlstm_kernel(mqk_ref, mv_ref, mo_ref, small_ref, smallt_ref, hist_ref, ext0_ref, m0_ref, cw_ref, cb_ref,
                  wq_ref, wk_ref, ng_ref, hm_ref, ext_ref, m_ref, ext_scr, m_scr, carry_scr, *, L):
    c = pl.program_id(1)

    @pl.when(c == 0)
    def _init():
        ext_scr[...] = ext0_ref[0]
        m_scr[...] = m0_ref[0]
        carry_scr[...] = hist_ref[0]

    x = mqk_ref[0]
    tl = x.shape[0]
    full = jnp.concatenate([carry_scr[...], x], axis=0)
    y = cb_ref[...]
    for j in range(MLSTM_CONV):
        y = y + cw_ref[j:j + 1, :] * full[8 - (MLSTM_CONV - 1) + j:8 - (MLSTM_CONV - 1) + j + tl]
    carry_scr[...] = full[tl:tl + 8]
    qk = _silu(y).astype(BF16)
    q_all = _dot(qk, wq_ref[...])
    k_all = _dot(qk, wk_ref[...]) * (HEAD_DIM ** -0.5)
    lane = lax.broadcasted_iota(jnp.int32, (L, LANE), 1)
    lane_row = lax.broadcasted_iota(jnp.int32, (1, LANE), 1)
    t_i = lax.broadcasted_iota(jnp.int32, (L, L), 0)
    s_i = lax.broadcasted_iota(jnp.int32, (L, L), 1)
    causal = s_i <= t_i
    tril_b = jnp.where(causal, 1.0, 0.0).astype(BF16)
    triu_b = jnp.where(t_i <= s_i, 1.0, 0.0).astype(BF16)
    eye_b = jnp.where(lax.broadcasted_iota(jnp.int32, (LANE, LANE), 0) == lax.broadcasted_iota(jnp.int32, (LANE, LANE), 1),
                      1.0, 0.0).astype(BF16)
    for u in range(tl // L):
        sl = slice(u * L, (u + 1) * L)
        sm = small_ref[0, sl, :]
        smt = smallt_ref[0, :, sl]
        bcols = _dot01(tril_b, sm)
        brows = _dot01_r(smt, triu_b)
        mrow = m_scr[...]
        for j in range(2):
            cs = slice(j * LANE, (j + 1) * LANE)
            q_slab = q_all[sl, cs]
            k_b = k_all[sl, cs].astype(BF16)
            v_slab = mv_ref[0, sl, cs].astype(F32)
            k_t = _dot_nt(eye_b, k_b).astype(BF16)
            outs = []
            for half in range(2):
                h = 2 * j + half
                keep = (lane >= HEAD_DIM * half) & (lane < HEAD_DIM * (half + 1))
                ones_lane = HEAD_DIM if half == 0 else 0
                qm = jnp.where(keep, q_slab, 0.0).astype(BF16)
                v_ext = jnp.where(keep, v_slab, jnp.where(lane == ones_lane, 1.0, 0.0))
                b_col = bcols[:, 28 + h:29 + h]
                li_col = sm[:, 24 + h:25 + h]
                b_row = brows[4 + h:5 + h, :]
                li_row = smt[h:h + 1, :]
                m_h = mrow[:, h:h + 1]
                d = jnp.where(causal, b_col - b_row + li_row, NEG)
                inter = b_col + m_h
                mt = jnp.maximum(inter, jnp.max(d, axis=-1, keepdims=True))
                a = _dot_nt(qm, k_b) * jnp.exp(d - mt)
                wi = jnp.exp(inter - mt)
                cts = ext_scr[h]
                num = wi * _dot(qm, cts.astype(BF16)) + _dot(a.astype(BF16), v_ext.astype(BF16))
                den = jnp.maximum(jnp.abs(num[:, ones_lane:ones_lane + 1]), jnp.exp(-mt))
                outs.append(num / den)
                bl = b_col[L - 1:L, :]
                g_col = bl - b_col + li_col
                m_new = jnp.maximum(bl + m_h, jnp.max(g_col, axis=0, keepdims=True))
                w_col = jnp.exp(g_col - m_new)
                dec = jnp.exp(bl + m_h - m_new)
                ext_scr[h] = dec * cts + _dot(k_t, (w_col * v_ext).astype(BF16))
                mrow = jnp.where(lane_row == h, m_new, mrow)
            hs = jnp.where(lane < HEAD_DIM, outs[0], outs[1])
            sq = hs * hs
            s0 = jnp.sum(jnp.where(lane < HEAD_DIM, sq, 0.0), axis=-1, keepdims=True)
            s1 = jnp.sum(jnp.where(lane < HEAD_DIM, 0.0, sq), axis=-1, keepdims=True)
            rs = jnp.where(lane < HEAD_DIM, lax.rsqrt(s0 * (1.0 / HEAD_DIM) + EPS), lax.rsqrt(s1 * (1.0 / HEAD_DIM) + EPS))
            hm_ref[0, sl, cs] = (mo_ref[0, sl, cs] * (hs * rs * ng_ref[:, cs])).astype(BF16)
        m_scr[...] = mrow

    @pl.when(c == pl.num_programs(1) - 1)
    def _out():
        ext_ref[0] = ext_scr[...]
        m_ref[0] = m_scr[...]


def _mlstm(mqk3, mv3, mo3, small3, smallt3, hist, ext0, m0, cw, cb, wq, wk, ng, *, tl):
    b, s, _ = mqk3.shape
    L = MLSTM_L
    assert s % tl == 0 and tl % L == 0
    blk = lambda w: pl.BlockSpec((1, tl, w), lambda bi, c: (bi, c, 0))
    st = lambda shape: pl.BlockSpec((1,) + shape, lambda bi, c: (bi,) + (0,) * len(shape))
    const = lambda shape: pl.BlockSpec(shape, lambda bi, c: (0,) * len(shape))
    return pl.pallas_call(
        functools.partial(_mlstm_kernel, L=L),
        grid=(b, s // tl),
        in_specs=[blk(256), blk(256), blk(256), blk(LANE),
                  pl.BlockSpec((1, 8, tl), lambda bi, c: (bi, 0, c)),
                  st((8, 256)), st((MLSTM_HEADS, LANE, LANE)), st((1, LANE)),
                  const((MLSTM_CONV, 256)), const((1, 256)), const((256, 256)), const((256, 256)), const((1, 256))],
        out_specs=[blk(256), st((MLSTM_HEADS, LANE, LANE)), st((1, LANE))],
        out_shape=(jax.ShapeDtypeStruct((b, s, 256), BF16),
                   jax.ShapeDtypeStruct((b, MLSTM_HEADS, LANE, LANE), F32),
                   jax.ShapeDtypeStruct((b, 1, LANE), F32)),
        scratch_shapes=[pltpu.VMEM((MLSTM_HEADS, LANE, LANE), F32), pltpu.VMEM((1, LANE), F32),
                        pltpu.VMEM((8, 256), F32)],
        compiler_params=_params(("arbitrary", "arbitrary")),
        name="mlstm",
    )(mqk3, mv3, mo3, small3, smallt3, hist, ext0, m0, cw, cb, wq, wk, ng)


FF_CHUNK = 1408


def _post_kernel(*refs, final, segmented, tiles_per_batch, seg_len):
    if segmented:
        (x_ref, onsa_ref, hm_ref, ocm_ref, mod_ref, wout_ref, g2_ref, wup_ref, cw_ref, cb_ref, wdn_ref, fg_ref,
         h1_ref, h2_ref, y_ref, fst_ref) = refs
        carry_scr = None
    else:
        (x_ref, onsa_ref, hm_ref, ocm_ref, mod_ref, wout_ref, g2_ref, wup_ref, cw_ref, cb_ref, wdn_ref, fg_ref,
         y_ref, fst_ref, carry_scr) = refs
    x = x_ref[...]
    tm = x.shape[0]
    mix = (_dot(onsa_ref[...].astype(BF16), wout_ref[0:512, :]) + _dot(hm_ref[...], wout_ref[512:768, :])
           + _dot(ocm_ref[...], wout_ref[768:1024, :]))
    mod = mod_ref[0]
    x1 = x + mod[:, 2 * D_MODEL:3 * D_MODEL] * mix
    h2 = x1 * lax.rsqrt(jnp.mean(x1 * x1, axis=-1, keepdims=True) + EPS) * g2_ref[...]
    hb = (h2 * (1.0 + mod[:, 4 * D_MODEL:5 * D_MODEL]) + mod[:, 3 * D_MODEL:4 * D_MODEL]).astype(BF16)
    if not segmented:
        @pl.when(pl.program_id(0) % tiles_per_batch == 0)
        def _reset():
            carry_scr[...] = jnp.zeros(carry_scr.shape, F32)
    else:
        tmod = lax.broadcasted_iota(jnp.int32, (tm, 1), 0) % seg_len
    acc = jnp.zeros((tm, D_MODEL), F32)
    for c0 in range(0, D_FF, FF_CHUNK):
        parts = []
        for part in range(2):
            cs = slice(part * D_FF + c0, part * D_FF + c0 + FF_CHUNK)
            up = _dot(hb, wup_ref[:, cs])
            prev = jnp.zeros((8, FF_CHUNK), F32) if segmented else carry_scr[:, cs]
            full = jnp.concatenate([prev, up], axis=0)
            s1 = full[7:7 + tm]
            s2 = full[6:6 + tm]
            if segmented:
                s1 = jnp.where(tmod >= 1, s1, h1_ref[:, cs])
                s2 = jnp.where(tmod >= 2, s2, h2_ref[:, cs])
                fst_ref[:, cs] = up
            else:
                carry_scr[:, cs] = up[tm - 8:tm]
                fst_ref[0, :, cs] = up[tm - 8:tm]
            parts.append(cb_ref[:, cs] + cw_ref[0:1, cs] * s2 + cw_ref[1:2, cs] * s1 + cw_ref[2:3, cs] * up)
        act = (_silu(parts[0]) * parts[1]).astype(BF16)
        acc = acc + _dot(act, wdn_ref[c0:c0 + FF_CHUNK, :])
    x2 = x1 + mod[:, 5 * D_MODEL:6 * D_MODEL] * acc
    if final:
        x2 = x2 * lax.rsqrt(jnp.mean(x2 * x2, axis=-1, keepdims=True) + EPS) * fg_ref[...]
    y_ref[...] = x2


def _post(x2d, onsa, hm, ocm, mod, wout, g2, wup, cw, cb, wdn, fg, h1=None, h2=None, *, tm, tiles_per_batch, final,
          seg_len=0):
    t_total = x2d.shape[0]
    n_tiles = t_total // tm
    segmented = h1 is not None
    r = mod.shape[1]
    row = lambda w: pl.BlockSpec((tm, w), lambda i: (i, 0))
    in_specs = [row(D_MODEL), row(512), row(256), row(256),
                pl.BlockSpec((1, r, 6 * D_MODEL), lambda i: (i // tiles_per_batch, 0, 0)),
                _const_spec((D_MODEL, D_MODEL)), _const_spec((1, D_MODEL)), _const_spec((D_MODEL, 2 * D_FF)),
                _const_spec((FFN_CONV, 2 * D_FF)), _const_spec((1, 2 * D_FF)), _const_spec((D_FF, D_MODEL)),
                _const_spec((1, D_MODEL))]
    args = [x2d, onsa, hm, ocm, mod, wout, g2, wup, cw, cb, wdn, fg]
    if segmented:
        in_specs += [row(2 * D_FF), row(2 * D_FF)]
        args += [h1, h2]
        out_specs = [row(D_MODEL), row(2 * D_FF)]
        out_shape = (jax.ShapeDtypeStruct((t_total, D_MODEL), F32), jax.ShapeDtypeStruct((t_total, 2 * D_FF), F32))
        scratch = []
    else:
        nb = n_tiles // tiles_per_batch
        out_specs = [row(D_MODEL), pl.BlockSpec((1, 8, 2 * D_FF), lambda i: (i // tiles_per_batch, 0, 0))]
        out_shape = (jax.ShapeDtypeStruct((t_total, D_MODEL), F32), jax.ShapeDtypeStruct((nb, 8, 2 * D_FF), F32))
        scratch = [pltpu.VMEM((8, 2 * D_FF), F32)]
    return pl.pallas_call(
        functools.partial(_post_kernel, final=final, segmented=segmented, tiles_per_batch=tiles_per_batch,
                          seg_len=seg_len),
        grid=(n_tiles,),
        in_specs=in_specs, out_specs=out_specs, out_shape=out_shape, scratch_shapes=scratch,
        compiler_params=_params(("arbitrary",)),
        name="post",
    )(*args)


def _col_perm():
    o = IN_OFF
    cols = list(range(o[0], o[4])) + list(range(o[5], o[8])) + list(range(o[10], o[12]))
    small = list(range(o[4], o[5])) + list(range(o[8], o[10]))
    small += [IN_W] * (LANE - len(small))
    return np.array(cols + small, dtype=np.int32)


def _block_diag(mats):
    n = len(mats)
    rows = []
    for i, m in enumerate(mats):
        rows.append(jnp.concatenate([m if j == i else jnp.zeros((m.shape[0], mats[j].shape[1]), m.dtype)
                                     for j in range(n)], axis=1))
    return jnp.concatenate(rows, axis=0)


def _rope_tables(pos):
    half = HEAD_DIM // 2
    inv = jnp.power(ROPE_THETA, -jnp.arange(half, dtype=F32) / half)
    ang = pos.astype(F32)[:, None] * inv[None, :]
    cos = jnp.cos(ang)
    sin = jnp.sin(ang)
    return jnp.tile(cos, (1, 4)), jnp.concatenate([-sin, sin, -sin, sin], axis=1)


def _layer_weights(l, w):
    win_z = jnp.concatenate([w['w_in'][l], jnp.zeros((D_MODEL, 1), F32)], axis=1)
    bin_z = jnp.concatenate([w['b_in'][l], jnp.zeros((1,), F32)])
    perm = _col_perm()
    cmp_a = w['cmp_a'][l]
    cmp_w = w['cmp_w'][l]
    c_ws = jnp.tril(w['c_ws'][l])
    return dict(
        wp=win_z[:, perm].astype(BF16), bp=bin_z[perm][None, :],
        g1=w['norm1_g'][l][None, :], g2=w['norm2_g'][l][None, :],
        cg=w['c_norm_g'][l][None, :], c_ws=c_ws, c_bs=w['c_bs'][l],
        a_tile=jnp.concatenate([cmp_a[0].reshape(CMP_BLOCK, KV_W), cmp_a[1].reshape(CMP_BLOCK, KV_W)], axis=1),
        w_bd=_block_diag([cmp_w[0, 0], cmp_w[0, 1], cmp_w[1, 0], cmp_w[1, 1]]).astype(BF16),
        m_cw=w['m_conv_w'][l], m_cb=w['m_conv_b'][l][None, :],
        wq=_block_diag([w['m_wq'][l][h] for h in range(MLSTM_HEADS)]).astype(BF16),
        wk=_block_diag([w['m_wk'][l][h] for h in range(MLSTM_HEADS)]).astype(BF16),
        ng=w['m_norm_g'][l].reshape(1, MLSTM_W),
        wout=w['w_out'][l].astype(BF16), wup=w['w_up'][l].astype(BF16), wdn=w['w_down'][l].astype(BF16),
        f_cw=w['f_conv_w'][l], f_cb=w['f_conv_b'][l][None, :],
    )


def _mlstm_state_in(c0, n0, m0):
    b = c0.shape[0]
    tiles = []
    for h in range(MLSTM_HEADS):
        r0 = HEAD_DIM * (h % 2)
        ol = HEAD_DIM if h % 2 == 0 else 0
        t = jnp.zeros((b, LANE, LANE), F32)
        t = t.at[:, r0:r0 + HEAD_DIM, r0:r0 + HEAD_DIM].set(jnp.swapaxes(c0[:, h], 1, 2))
        t = t.at[:, r0:r0 + HEAD_DIM, ol].set(n0[:, h])
        tiles.append(t)
    m = jnp.zeros((b, 1, LANE), F32).at[:, 0, 0:MLSTM_HEADS].set(m0)
    return jnp.stack(tiles, axis=1), m


def _mlstm_state_out(ext, m):
    cs, ns = [], []
    for h in range(MLSTM_HEADS):
        r0 = HEAD_DIM * (h % 2)
        ol = HEAD_DIM if h % 2 == 0 else 0
        cs.append(jnp.swapaxes(ext[:, h, r0:r0 + HEAD_DIM, r0:r0 + HEAD_DIM], 1, 2))
        ns.append(ext[:, h, r0:r0 + HEAD_DIM, ol])
    return jnp.stack(cs, axis=1), jnp.stack(ns, axis=1), m[:, 0, 0:MLSTM_HEADS]


def _layer_prompt(x2d, mod, lw, fg, *, b, s, final, tm=256):
    assert tm % SEL_KT == 0
    tiles_per_batch = s // tm
    cos_t, sin_t = _rope_tables(jnp.arange(s, dtype=jnp.int32))
    wc = jnp.concatenate([lw['c_ws'][g] for g in range(CMLP_GROUPS)], axis=1).astype(BF16)
    bc = jnp.repeat(lw['c_bs'].T, HEAD_DIM, axis=1)
    rows_t, win, qb, kvb, small, mqk, mv, mo, ocm, _, cmp_rows = _inproj(
        x2d, mod, lw['g1'], lw['wp'], lw['bp'], cos_t, sin_t, wc, bc, lw['cg'],
        tm=tm, tiles_per_batch=tiles_per_batch, tc=CMLP_CHUNK, q_dtype=BF16, rows_feature_major=True)
    nb = s // CMP_BLOCK
    comp = _compress(cmp_rows.reshape(b, s, 2 * KV_W), lw['a_tile'], lw['w_bd'], rows_per_step=min(s, 1024))
    comp = comp.reshape(b, nb // 2, 2, 2 * KV_W).transpose(0, 2, 1, 3).reshape(b, nb, 2 * KV_W)
    onsa = _nsa_prompt(qb, small, comp[..., 0:KV_W], comp[..., KV_W:], kvb.reshape(b, s, 768), b=b, s=s, kt=SEL_KT)
    zeros = lambda *sh: jnp.zeros(sh, F32)
    ext0, m0 = _mlstm_state_in(zeros(b, MLSTM_HEADS, HEAD_DIM, HEAD_DIM), zeros(b, MLSTM_HEADS, HEAD_DIM),
                               zeros(b, MLSTM_HEADS))
    small3 = small.reshape(b, s, LANE)
    smallt = jnp.swapaxes(small3[:, :, 24:32], 1, 2)
    hm, ext, mout = _mlstm(mqk.reshape(b, s, 256), mv.reshape(b, s, 256), mo.reshape(b, s, 256), small3, smallt,
                           zeros(b, 8, 256), ext0, m0, lw['m_cw'], lw['m_cb'], lw['wq'], lw['wk'], lw['ng'], tl=tm)
    c1, n1, m1 = _mlstm_state_out(ext, mout)
    y, fst = _post(x2d, onsa, hm.reshape(b * s, 256), ocm, mod, lw['wout'], lw['g2'], lw['wup'], lw['f_cw'],
                   lw['f_cb'], lw['wdn'], fg, tm=tm, tiles_per_batch=tiles_per_batch, final=final)
    n_keep = min(WINDOW, s)
    kv_rows = jnp.transpose(rows_t.reshape(b, 4, NSA_KV_HEADS, HEAD_DIM, s), (0, 4, 1, 2, 3))
    new_win = win.reshape(b, s, 256)[:, s - n_keep:].reshape(b, n_keep, 2, NSA_KV_HEADS, HEAD_DIM)
    state = (kv_rows, new_win,
             c1, n1, m1, mqk.reshape(b, s, 256)[:, s - (MLSTM_CONV - 1):], fst[:, 8 - (FFN_CONV - 1):])
    return y, state


def _layer_sample(x2d, mod, lw, fg, cache_t, page_table, wcache_t, mc, mn, mm, mconv, fconv, *, layer, b, s, final,
                  kt=512):
    t = b * s
    past = page_table.shape[1] * PAGE_SIZE
    pos = past + jnp.arange(s, dtype=jnp.int32)
    cos1, sin1 = _rope_tables(pos)
    cos_t, sin_t = jnp.tile(cos1, (b, 1)), jnp.tile(sin1, (b, 1))
    eye_b = jnp.eye(b, dtype=F32)
    wc = jnp.concatenate([jnp.kron(eye_b, lw['c_ws'][g][:s, :s]) for g in range(CMLP_GROUPS)], axis=1).astype(BF16)
    bc = jnp.tile(jnp.repeat(lw['c_bs'].T[:s], HEAD_DIM, axis=1), (b, 1))
    rows, win, qf, _, small, mqk, mv, mo, ocm, cvn = _inproj(
        x2d, mod, lw['g1'], lw['wp'], lw['bp'], cos_t, sin_t, wc, bc, lw['cg'],
        tm=t, tiles_per_batch=1, tc=t, q_dtype=F32, rows_feature_major=False)
    nbc = past // CMP_BLOCK
    pair = jnp.repeat(jnp.eye(nbc // 2, dtype=BF16), 2, axis=0)
    n_keep = wcache_t.shape[3]
    wnew3 = win.reshape(b, s, 256)
    a_t = jnp.tile(lw['a_tile'].T, (1, PAGE_SIZE // CMP_BLOCK))
    onsa = _nsa_sample(page_table, qf.reshape(b, s, 512), small.reshape(b, s, LANE), rows.reshape(b, s, 512), wnew3,
                       wcache_t, a_t, lw['w_bd'].T, pair, cache_t, layer=layer, kt=kt).reshape(t, 512)
    L = MLSTM_L
    pad3 = lambda a, fill=0.0: jnp.pad(a.reshape(b, s, -1), ((0, 0), (0, L - s), (0, 0)), constant_values=fill)
    small3 = small.reshape(b, s, LANE)
    lane = jnp.arange(LANE)
    small_p = jnp.where((lane >= 24) & (lane < 28), pad3(small3, NEG), pad3(small3))
    smallt = jnp.swapaxes(small_p[:, :, 24:32], 1, 2)
    hist = jnp.pad(mconv, ((0, 0), (8 - (MLSTM_CONV - 1), 0), (0, 0)))
    ext0, m0 = _mlstm_state_in(mc, mn, mm)
    hm, ext, mout = _mlstm(pad3(mqk), pad3(mv), pad3(mo), small_p, smallt, hist, ext0, m0, lw['m_cw'], lw['m_cb'],
                           lw['wq'], lw['wk'], lw['ng'], tl=L)
    c1, n1, m1 = _mlstm_state_out(ext, mout)
    hm = hm[:, :s].reshape(t, 256)
    h1 = jnp.zeros((b, s, 2 * D_FF), F32).at[:, 0].set(fconv[:, 1]).reshape(t, 2 * D_FF)
    h2 = jnp.zeros((b, s, 2 * D_FF), F32).at[:, 0].set(fconv[:, 0]).at[:, 1].set(fconv[:, 1]).reshape(t, 2 * D_FF)
    y, up = _post(x2d, onsa, hm, ocm, mod, lw['wout'], lw['g2'], lw['wup'], lw['f_cw'], lw['f_cb'], lw['wdn'], fg,
                  h1, h2, tm=t, tiles_per_batch=1, final=final, seg_len=s)
    new_win_t = jnp.concatenate([wcache_t[layer][:, :, s:], jnp.swapaxes(wnew3, 1, 2)], axis=2)
    new_win = jnp.transpose(new_win_t.reshape(b, 2, NSA_KV_HEADS, HEAD_DIM, n_keep), (0, 4, 1, 2, 3))
    state = (rows.reshape(b, s, 4, NSA_KV_HEADS, HEAD_DIM), new_win, c1, n1, m1,
             mqk.reshape(b, s, 256)[:, s - (MLSTM_CONV - 1):], cvn.reshape(b, s, CMLP_W),
             up.reshape(b, s, 2 * D_FF)[:, s - (FFN_CONV - 1):])
    return y, state


def kernel(x_prompt, x_sample, cache_nsa_kv, page_table, cache_win_kv, state_mlstm_C, state_mlstm_n, state_mlstm_m, state_mlstm_conv, state_ffn_conv, c_prompt, c_sample, ada_w, ada_b, norm1_g, norm2_g, w_in, b_in, cmp_a, cmp_w, m_conv_w, m_conv_b, m_wq, m_wk, m_norm_g, c_norm_g, c_ws, c_bs, w_out, w_up, f_conv_w, f_conv_b, w_down, final_norm_g):
    bp, sp, _ = x_prompt.shape
    bs, ss, _ = x_sample.shape
    depth = ada_w.shape[0]
    w = dict(w_in=w_in, b_in=b_in, norm1_g=norm1_g, norm2_g=norm2_g, cmp_a=cmp_a, cmp_w=cmp_w, m_conv_w=m_conv_w,
             m_conv_b=m_conv_b, m_wq=m_wq, m_wk=m_wk, m_norm_g=m_norm_g, c_norm_g=c_norm_g, c_ws=c_ws, c_bs=c_bs,
             w_out=w_out, w_up=w_up, f_conv_w=f_conv_w, f_conv_b=f_conv_b, w_down=w_down)
    n_c = bp + bs
    n_pad = -(-n_c // 8) * 8
    c_all = jnp.pad(jnp.concatenate([c_prompt, c_sample], axis=0), ((0, n_pad - n_c), (0, 0)))
    mod_all = _ada(c_all, ada_w, ada_b)
    fg = final_norm_g[None, :]
    xp = x_prompt.reshape(bp * sp, D_MODEL)
    xs = x_sample.reshape(bs * ss, D_MODEL)
    n_pool = cache_nsa_kv.shape[1]
    n_keep = cache_win_kv.shape[2]
    cache_t = jnp.transpose(cache_nsa_kv, (0, 1, 3, 4, 5, 2)).reshape(depth, n_pool, 512, PAGE_SIZE)
    wcache_t = jnp.transpose(cache_win_kv, (0, 1, 3, 4, 5, 2)).reshape(depth, bs, 256, n_keep)
    st_p, st_s = [], []
    for l in range(depth):
        lw = _layer_weights(l, w)
        final = l == depth - 1
        mod_p = mod_all[l, 0:bp][:, None, :]
        mod_s = jnp.repeat(mod_all[l, bp:bp + bs], ss, axis=0)[None]
        xp, sp_l = _layer_prompt(xp, mod_p, lw, fg, b=bp, s=sp, final=final)
        xs, ss_l = _layer_sample(xs, mod_s, lw, fg, cache_t, page_table, wcache_t, state_mlstm_C[l],
                                 state_mlstm_n[l], state_mlstm_m[l], state_mlstm_conv[l], state_ffn_conv[l],
                                 layer=l, b=bs, s=ss, final=final)
        st_p.append(sp_l)
        st_s.append(ss_l)
    sp_st = [jnp.stack(t) for t in zip(*st_p)]
    ss_st = [jnp.stack(t) for t in zip(*st_s)]
    y_prompt = xp.reshape(bp, sp, D_MODEL)
    y_sample = xs.reshape(bs, ss, D_MODEL)
    return (y_prompt, y_sample, sp_st[0], ss_st[0], sp_st[1], ss_st[1],
            sp_st[2], sp_st[3], sp_st[4], sp_st[5],
            ss_st[2], ss_st[3], ss_st[4], ss_st[5],
            ss_st[6], sp_st[6], ss_st[7])
```

```python
import functools

import numpy as np
import jax
import jax.numpy as jnp
from jax import lax
from jax.experimental import pallas as pl
from jax.experimental.pallas import tpu as pltpu

F32 = jnp.float32
BF16 = jnp.bfloat16

D_MODEL = 1024
HEAD_DIM = 64
NSA_HEADS = 8
NSA_KV_HEADS = 2
NSA_GROUP = 4
CMP_BLOCK = 32
SEL_BLOCK = 64
SEL_TOPN = 16
WINDOW = 512
ROPE_THETA = 10000.0
MLSTM_HEADS = 4
MLSTM_CONV = 4
CMLP_GROUPS = 4
CMLP_CHUNK = 128
D_FF = 2816
FFN_CONV = 3
PAGE_SIZE = 128
NSA_W = 512
KV_W = 128
MLSTM_W = 256
CMLP_W = 256
IN_SIZES = (NSA_W, 2 * KV_W, 2 * KV_W, 2 * KV_W, 3 * NSA_HEADS, MLSTM_W, MLSTM_W, MLSTM_W, MLSTM_HEADS,
            MLSTM_HEADS, CMLP_W, CMLP_W)
IN_OFF = tuple(int(v) for v in np.cumsum((0,) + IN_SIZES))
IN_W = IN_OFF[-1]
EPS = 1e-6
NEG = -1e30
FORCE = 1e4
LANE = 128
NP_COLS = 2688
VMEM_LIMIT = 56 * 1024 * 1024
PAGES_PER_STEP = 32
NSA_TQ = 256
SEL_KT = 256
Q_SCALE = HEAD_DIM ** -0.5 * 1.4426950408889634
MLSTM_L = 128


def _dot(a, b):
    return jnp.dot(a, b, preferred_element_type=F32)


def _dot_nt(a, b):
    return lax.dot_general(a, b, (((1,), (1,)), ((), ())), preferred_element_type=F32)


def _dot01(m01, x):
    hi = x.astype(BF16)
    r1 = x - hi.astype(F32)
    mid = r1.astype(BF16)
    lo = (r1 - mid.astype(F32)).astype(BF16)
    return _dot(m01, hi) + _dot(m01, mid) + _dot(m01, lo)


def _dot01_r(x, m01):
    hi = x.astype(BF16)
    r1 = x - hi.astype(F32)
    mid = r1.astype(BF16)
    lo = (r1 - mid.astype(F32)).astype(BF16)
    return _dot(hi, m01) + _dot(mid, m01) + _dot(lo, m01)


def _sigmoid(x):
    return 1.0 / (1.0 + jnp.exp(-x))


def _silu(x):
    return x * _sigmoid(x)


def _gelu_tanh(x):
    return 0.5 * x * (1.0 + jnp.tanh(0.7978845608028654 * (x + 0.044715 * (x * x * x))))


def _log_sigmoid(x):
    return jnp.minimum(x, 0.0) - jnp.log(1.0 + jnp.exp(-jnp.abs(x)))


def _const_spec(shape):
    nd = len(shape)
    return pl.BlockSpec(shape, lambda *_: (0,) * nd)


def _params(sem):
    return pltpu.CompilerParams(dimension_semantics=sem, vmem_limit_bytes=VMEM_LIMIT)


def _ada_kernel(c_ref, w_ref, b_ref, o_ref):
    c = c_ref[...]
    o_ref[0] = _dot(_silu(c).astype(BF16), w_ref[0]) + b_ref[0]


def _ada(c_all, ada_w, ada_b):
    depth = ada_w.shape[0]
    n = c_all.shape[0]
    return pl.pallas_call(
        _ada_kernel,
        grid=(depth, 6),
        in_specs=[pl.BlockSpec((n, D_MODEL), lambda l, j: (0, 0)),
                  pl.BlockSpec((1, D_MODEL, D_MODEL), lambda l, j: (l, 0, j)),
                  pl.BlockSpec((1, 1, D_MODEL), lambda l, j: (l, 0, j))],
        out_specs=pl.BlockSpec((1, n, D_MODEL), lambda l, j: (l, 0, j)),
        out_shape=jax.ShapeDtypeStruct((depth, n, 6 * D_MODEL), F32),
        compiler_params=_params(("arbitrary", "arbitrary")),
        name="ada",
    )(c_all, ada_w.astype(BF16), ada_b.reshape(depth, 1, 6 * D_MODEL))


def _inproj_kernel(x_ref, mod_ref, g_ref, w_ref, b_ref, cos_ref, sin_ref, wc_ref, bc_ref, cg_ref,
                   rows_ref, win_ref, q_ref, kvb_ref, small_ref, mqk_ref, mv_ref, mo_ref, ocm_ref, cvn_ref,
                   *cmp_rows_ref, tc, tiles_per_batch):
    x = x_ref[...]
    tm = x.shape[0]
    xn = x * lax.rsqrt(jnp.mean(x * x, axis=-1, keepdims=True) + EPS) * g_ref[...]
    mod = mod_ref[0]
    h = xn * (1.0 + mod[:, D_MODEL:2 * D_MODEL]) + mod[:, 0:D_MODEL]
    hb = h.astype(BF16)
    z = _dot(hb, w_ref[...]) + b_ref[...]
    cos = cos_ref[...]
    sin = sin_ref[...]
    first_half = lax.broadcasted_iota(jnp.int32, (tm, LANE), 1) % HEAD_DIM < HEAD_DIM // 2

    def rope(c0):
        xs = z[:, c0:c0 + LANE]
        swapped = jnp.where(first_half, pltpu.roll(xs, LANE - HEAD_DIM // 2, axis=1),
                            pltpu.roll(xs, HEAD_DIM // 2, axis=1))
        return xs * cos + swapped * sin

    for j in range(4):
        q_ref[:, j * LANE:(j + 1) * LANE] = (rope(j * LANE) * Q_SCALE).astype(q_ref.dtype)
    kc = rope(512)
    vc = z[:, 640:768]
    ks = rope(768)
    vs = z[:, 896:1024]
    kw = rope(1024)
    vw = z[:, 1152:1280]
    if cmp_rows_ref:
        rows_ref[0] = jnp.transpose(jnp.concatenate([kc, vc, ks, vs], axis=1))
        cmp_rows_ref[0][:, 0:128] = kc
        cmp_rows_ref[0][:, 128:256] = vc
    else:
        rows_ref[:, 0:128] = kc
        rows_ref[:, 128:256] = vc
        rows_ref[:, 256:384] = ks
        rows_ref[:, 384:512] = vs
    win_ref[:, 0:128] = kw
    win_ref[:, 128:256] = vw
    lane_k = lax.broadcasted_iota(jnp.int32, (tm, LANE), 1)
    row0 = (pl.program_id(0) % tiles_per_batch) * tm
    blk = ((row0 + lax.broadcasted_iota(jnp.int32, (tm, LANE), 0)) // SEL_BLOCK) % HEAD_DIM
    low = lane_k < HEAD_DIM
    kvb_ref[:, 0:128] = jnp.where(low, ks, jnp.where(lane_k == HEAD_DIM + blk, 1.0, 0.0)).astype(BF16)
    kvb_ref[:, 128:256] = jnp.where(low, jnp.where(lane_k == blk, 1.0, 0.0), ks).astype(BF16)
    kvb_ref[:, 256:384] = jnp.where(low, vs, 1.0).astype(BF16)
    kvb_ref[:, 384:512] = jnp.where(low, 1.0, vs).astype(BF16)
    kvb_ref[:, 512:640] = kw.astype(BF16)
    kvb_ref[:, 640:768] = vw.astype(BF16)
    sm = z[:, 2560:2688]
    lane = lax.broadcasted_iota(jnp.int32, sm.shape, 1)
    small_ref[...] = jnp.where(lane < 24, _sigmoid(sm), jnp.where(lane < 28, sm, _log_sigmoid(sm)))
    mqk_ref[...] = z[:, 1280:1536]
    mv_ref[...] = z[:, 1536:1792].astype(BF16)
    mo_ref[...] = _sigmoid(z[:, 1792:2048])
    cu = _gelu_tanh(z[:, 2048:2304])
    cv = _gelu_tanh(z[:, 2304:2560])
    cvn = cv * lax.rsqrt(jnp.mean(cv * cv, axis=-1, keepdims=True) + EPS) * cg_ref[...]
    cvn_ref[...] = cvn
    lane_c = lax.broadcasted_iota(jnp.int32, (tc, CMLP_W), 1) // HEAD_DIM
    for c in range(tm // tc):
        v = cvn[c * tc:(c + 1) * tc]
        vbd = jnp.concatenate([jnp.where(lane_c == g, v, 0.0) for g in range(CMLP_GROUPS)], axis=0).astype(BF16)
        s = _dot(wc_ref[...], vbd) + bc_ref[...]
        ocm_ref[c * tc:(c + 1) * tc, :] = (cu[c * tc:(c + 1) * tc] * s).astype(BF16)


def _inproj(x2d, mod, g1, wp, bp, cos_t, sin_t, wc, bc, cg, *, tm, tiles_per_batch, tc, q_dtype,
            rows_feature_major):
    t_total = x2d.shape[0]
    n_tiles = t_total // tm
    r = mod.shape[1]
    row = lambda w: pl.BlockSpec((tm, w), lambda i: (i, 0))
    if rows_feature_major:
        rows_shape = jax.ShapeDtypeStruct((n_tiles // tiles_per_batch, 512, tiles_per_batch * tm), F32)
        rows_spec = pl.BlockSpec((1, 512, tm), lambda i: (i // tiles_per_batch, 0, i % tiles_per_batch))
    else:
        rows_shape = jax.ShapeDtypeStruct((t_total, 512), F32)
        rows_spec = row(512)
    out_shapes = (
        rows_shape,
        jax.ShapeDtypeStruct((t_total, 256), F32),
        jax.ShapeDtypeStruct((t_total, 512), q_dtype),
        jax.ShapeDtypeStruct((t_total, 768), BF16),
        jax.ShapeDtypeStruct((t_total, 128), F32),
        jax.ShapeDtypeStruct((t_total, 256), F32),
        jax.ShapeDtypeStruct((t_total, 256), BF16),
        jax.ShapeDtypeStruct((t_total, 256), F32),
        jax.ShapeDtypeStruct((t_total, 256), BF16),
        jax.ShapeDtypeStruct((t_total, 256), F32),
    )
    out_specs = [rows_spec, row(256), row(512), row(768), row(128), row(256), row(256), row(256), row(256), row(256)]
    if rows_feature_major:
        out_shapes += (jax.ShapeDtypeStruct((t_total, 256), F32),)
        out_specs.append(row(256))
    return pl.pallas_call(
        functools.partial(_inproj_kernel, tc=tc, tiles_per_batch=tiles_per_batch),
        grid=(n_tiles,),
        in_specs=[row(D_MODEL),
                  pl.BlockSpec((1, r, 6 * D_MODEL), lambda i: (i // tiles_per_batch, 0, 0)),
                  _const_spec((1, D_MODEL)),
                  _const_spec((D_MODEL, NP_COLS)), _const_spec((1, NP_COLS)),
                  pl.BlockSpec((tm, LANE), lambda i: (i % tiles_per_batch, 0)),
                  pl.BlockSpec((tm, LANE), lambda i: (i % tiles_per_batch, 0)),
                  _const_spec(wc.shape), _const_spec(bc.shape), _const_spec((1, CMLP_W))],
        out_specs=out_specs,
        out_shape=out_shapes,
        compiler_params=_params(("arbitrary",)),
        name="inproj",
    )(x2d, mod, g1, wp, bp, cos_t, sin_t, wc, bc, cg)


def _compress_kernel(rows_ref, a_ref, w_ref, o_ref):
    x = rows_ref[0]
    n = x.shape[0] // CMP_BLOCK
    pooled = jnp.sum(x.reshape(n, CMP_BLOCK, 2 * KV_W) * a_ref[...][None], axis=1)
    o_ref[0] = _dot(pooled.astype(BF16), w_ref[...])


def _compress(rows3, a_tile, w_bd, *, rows_per_step):
    b, s, _ = rows3.shape
    nb = s // CMP_BLOCK
    steps = s // rows_per_step
    return pl.pallas_call(
        _compress_kernel,
        grid=(b, steps),
        in_specs=[pl.BlockSpec((1, rows_per_step, 2 * KV_W), lambda i, j: (i, j, 0)),
                  _const_spec((CMP_BLOCK, 2 * KV_W)), _const_spec((2 * KV_W, 2 * KV_W))],
        out_specs=pl.BlockSpec((1, rows_per_step // CMP_BLOCK, 2 * KV_W), lambda i, j: (i, j, 0)),
        out_shape=jax.ShapeDtypeStruct((b, nb, 2 * KV_W), F32),
        compiler_params=_params(("arbitrary", "arbitrary")),
        name="compress",
    )(rows3, a_tile, w_bd)


def _group_queries(qf, g):
    lane = lax.broadcasted_iota(jnp.int32, (qf.shape[0], LANE), 1)
    keep = (lane >= HEAD_DIM * g) & (lane < HEAD_DIM * (g + 1))
    out = []
    for r in range(NSA_GROUP):
        h = NSA_GROUP * g + r
        slab = qf[:, (h // 2) * LANE:(h // 2 + 1) * LANE]
        if h % 2 != g:
            slab = pltpu.roll(slab, HEAD_DIM, axis=1)
        out.append(jnp.where(keep, slab, 0.0))
    return out


def _place_heads(pieces, g):
    lane = lax.broadcasted_iota(jnp.int32, pieces[0].shape, 1)
    slabs = []
    for j in range(2):
        halves = []
        for half in range(2):
            p = pieces[2 * j + half]
            if half != g:
                p = pltpu.roll(p, HEAD_DIM, axis=1)
            halves.append(p)
        slabs.append(jnp.where(lane < HEAD_DIM, halves[0], halves[1]))
    return slabs


def _softmax_rows(s, allowed):
    sm = jnp.where(allowed, s, NEG)
    m = jnp.max(sm, axis=-1, keepdims=True)
    e = jnp.where(allowed, jnp.exp2(sm - m), 0.0)
    l = jnp.sum(e, axis=-1, keepdims=True)
    return e * (1.0 / jnp.maximum(l, 1e-30))


def _topk_select(vs, jrow, n_rounds):
    big = float(vs[0].shape[0] + 1)

    def body(_, carry):
        out = []
        for v in carry:
            m = jnp.max(v, axis=0, keepdims=True)
            first = jnp.min(jnp.where(v == m, jrow, big), axis=0, keepdims=True)
            out.append(jnp.where(jrow == first, -jnp.inf, v))
        return tuple(out)

    res = lax.fori_loop(0, n_rounds, body, tuple(vs))
    return [jnp.where((r == -jnp.inf) & (v > 0.5 * NEG), 1.0, 0.0) for r, v in zip(res, vs)]


def _expand_tile(sel_b, t, kt):
    nblk = sel_b.shape[1]
    jrow = lax.broadcasted_iota(jnp.int32, (nblk, kt), 0) - t * (kt // SEL_BLOCK)
    cdiv = lax.broadcasted_iota(jnp.int32, (nblk, kt), 1) // SEL_BLOCK
    e = jnp.where(jrow == cdiv, 1.0, 0.0).astype(BF16)
    return _dot(sel_b, e)


def _nsa_prompt_kernel(q_ref, small_ref, kc_ref, vc_ref, kvb_ref, o_ref, m_scr, acc_scr, lhs_scr, s_scr, park_scr,
                       qz_scr, *, kt, nbs):
    i = pl.program_id(1)
    tq = q_ref.shape[0]
    nbc = kc_ref.shape[1]
    qf = q_ref[...].astype(F32)
    q0 = i * tq
    qpos_c = q0 + lax.broadcasted_iota(jnp.int32, (tq, 1), 0)
    qpos_r = q0 + lax.broadcasted_iota(jnp.int32, (1, tq), 1)
    eye = jnp.where(lax.broadcasted_iota(jnp.int32, (tq, tq), 0) == lax.broadcasted_iota(jnp.int32, (tq, tq), 1),
                    1.0, 0.0).astype(BF16)
    qpos_r4 = q0 + lax.broadcasted_iota(jnp.int32, (1, NSA_GROUP * tq), 1) % tq
    kc = kc_ref[0].astype(BF16)
    vc = vc_ref[0].astype(BF16)
    half = nbc // 2
    rho_c = lax.broadcasted_iota(jnp.int32, (nbc, 1), 0)
    end_c = jnp.where(rho_c < half, 2 * rho_c, 2 * (rho_c - half) + 1) * CMP_BLOCK + (CMP_BLOCK - 1)
    rho_r = lax.broadcasted_iota(jnp.int32, (1, nbc), 1)
    end_r = jnp.where(rho_r < half, 2 * rho_r, 2 * (rho_r - half) + 1) * CMP_BLOCK + (CMP_BLOCK - 1)
    w_start = pl.multiple_of(jnp.maximum(q0 - WINDOW, 0), LANE)
    w_len = WINDOW + tq
    wpos = w_start + lax.broadcasted_iota(jnp.int32, (1, w_len), 1)
    bias_w = jnp.where((wpos <= qpos_c) & (wpos > qpos_c - WINDOW), 0.0, NEG)
    bias_c = jnp.where(end_r <= qpos_c, 0.0, NEG)
    any_c = qpos_c >= CMP_BLOCK - 1
    k_w = kvb_ref[0, pl.ds(w_start, w_len), 512:640]
    v_w = kvb_ref[0, pl.ds(w_start, w_len), 640:768]

    def attend(q_b, k, v_ones, bias):
        s = _dot_nt(q_b, k) + bias
        p = jnp.exp2(s - jnp.max(s, axis=-1, keepdims=True))
        o = _dot(p.astype(BF16), v_ones)
        return o * (1.0 / pltpu.roll(o, HEAD_DIM, axis=1))

    lane_g = lax.broadcasted_iota(jnp.int32, (1, LANE), 1)
    one_b = jnp.ones((), BF16)
    jrow_i = lax.broadcasted_iota(jnp.int32, (nbs, tq), 0)
    for g in range(NSA_KV_HEADS):
        qz4 = _group_queries(qf, g)
        qz_b = [x.astype(BF16) for x in qz4]
        qz = jnp.concatenate(qz_b, axis=0)
        keep_g = (lane_g >= HEAD_DIM * g) & (lane_g < HEAD_DIM * (g + 1))
        vc_x = jnp.where(keep_g, vc, one_b)
        o_c = [jnp.where(any_c, attend(qz_b[r], kc, vc_x, bias_c), 0.0) for r in range(NSA_GROUP)]
        s_t = _dot_nt(kc, qz)
        sm = s_t + jnp.where(end_c <= qpos_r4, 0.0, NEG)
        et = jnp.exp2(sm - jnp.max(sm, axis=0, keepdims=True))
        pt = et * (1.0 / jnp.sum(et, axis=0, keepdims=True))
        ph = pt[:, 0:tq] + pt[:, tq:2 * tq] + pt[:, 2 * tq:3 * tq] + pt[:, 3 * tq:4 * tq]
        imp = jnp.where(qpos_r >= CMP_BLOCK - 1, ph[0:half] + ph[half:nbc], 0.0)
        if nbs > half:
            imp = jnp.concatenate([imp, jnp.zeros((nbs - half, tq), F32)], axis=0)
        forced = (jrow_i == qpos_r // SEL_BLOCK) | (jrow_i == 0)
        valid = jrow_i * SEL_BLOCK <= qpos_r
        score = jnp.where(valid, imp + jnp.where(forced, FORCE, 0.0), NEG)
        for r in range(NSA_GROUP):
            park_scr[r] = o_c[r]
            qz_scr[r] = qz_b[r]
        sel_t = _topk_select([score], jrow_i.astype(F32), min(SEL_TOPN, nbs))[0]
        m_scr[...] = jnp.full(m_scr.shape, NEG, F32)
        acc_scr[...] = jnp.zeros(acc_scr.shape, F32)
        sel_q = _dot_nt(eye, sel_t.astype(BF16))
        if nbs < LANE:
            sel_q = jnp.concatenate([sel_q, jnp.zeros((tq, LANE - nbs), F32)], axis=1)
        negsel = (sel_q - 1.0) * 1e30
        base = HEAD_DIM * (1 - g)
        lane_q = lax.broadcasted_iota(jnp.int32, (tq, LANE), 1)
        feat_mask = (lane_q >= base) & (lane_q < base + HEAD_DIM)
        for hf in range(max(1, nbs // HEAD_DIM)):
            feat = (negsel if HEAD_DIM * hf == base else pltpu.roll(negsel, HEAD_DIM, axis=1)).astype(BF16)
            for r in range(NSA_GROUP):
                lhs_scr[hf, r] = jnp.where(feat_mask, feat, qz_b[r])
        tiles_per_half = HEAD_DIM * SEL_BLOCK // kt

        def scores(t, slot):
            k_t = kvb_ref[0, pl.ds(pl.multiple_of(t * kt, kt), kt), g * LANE:(g + 1) * LANE]
            hf = t // tiles_per_half
            for r in range(NSA_GROUP):
                s_scr[slot, r] = _dot_nt(lhs_scr[hf, r], k_t)

        def consume(t, slot, causal):
            k0 = pl.multiple_of(t * kt, kt)
            v_t = kvb_ref[0, pl.ds(k0, kt), (2 + g) * LANE:(3 + g) * LANE]
            if causal:
                kpos = k0 + lax.broadcasted_iota(jnp.int32, (1, kt), 1)
                bias = jnp.where(kpos <= qpos_c, 0.0, NEG)
            for r in range(NSA_GROUP):
                s = s_scr[slot, r]
                if causal:
                    s = s + bias
                m_old = m_scr[r]
                m_new = jnp.maximum(m_old, jnp.max(s, axis=-1, keepdims=True))
                p = jnp.exp2(s - jnp.concatenate([m_new] * (kt // LANE), axis=1))
                acc_scr[r] = jnp.exp2(m_old - m_new) * acc_scr[r] + _dot(p.astype(BF16), v_t)
                m_scr[r] = m_new

        n_full = q0 // kt
        scores(0, 0)

        def pair_body(tp, _):
            scores(2 * tp + 1, 1)
            consume(2 * tp, 0, False)
            scores(2 * tp + 2, 0)
            consume(2 * tp + 1, 1, False)
            return 0

        lax.fori_loop(0, n_full // 2, pair_body, 0)

        @pl.when(n_full % 2 == 1)
        def _odd():
            scores(n_full, 1)
            consume(n_full - 1, 0, False)
            consume(n_full, 1, True)

        @pl.when(n_full % 2 == 0)
        def _even():
            consume(n_full, 0, True)
        acc = acc_scr[...].reshape(NSA_GROUP * tq, LANE)
        o_s = acc * (1.0 / jnp.maximum(pltpu.roll(acc, HEAD_DIM, axis=1), 1e-30))
        v_wx = jnp.where(keep_g, v_w, one_b)
        pieces = []
        for r in range(NSA_GROUP):
            h = NSA_GROUP * g + r
            sl = slice(r * tq, (r + 1) * tq)
            o_w = attend(qz_scr[r], k_w, v_wx, bias_w)
            pieces.append(small_ref[:, 3 * h:3 * h + 1] * park_scr[r] + small_ref[:, 3 * h + 1:3 * h + 2] * o_s[sl]
                          + small_ref[:, 3 * h + 2:3 * h + 3] * o_w)
        for j, slab in enumerate(_place_heads(pieces, g)):
            o_ref[:, (2 * g + j) * LANE:(2 * g + j + 1) * LANE] = slab.astype(o_ref.dtype)


def _nsa_prompt(qb, small, kcp, vcp, kvb3, *, b, s, kt):
    tq = NSA_TQ
    nq = s // tq
    nbc = kcp.shape[1]
    nbs = s // SEL_BLOCK
    assert s >= WINDOW + tq and s % kt == 0 and nbc == 2 * nbs and nbs <= LANE and kt == SEL_KT
    return pl.pallas_call(
        functools.partial(_nsa_prompt_kernel, kt=kt, nbs=nbs),
        grid=(b, nq),
        in_specs=[pl.BlockSpec((tq, 512), lambda bi, i: (bi * nq + i, 0)),
                  pl.BlockSpec((tq, LANE), lambda bi, i: (bi * nq + i, 0)),
                  pl.BlockSpec((1, nbc, LANE), lambda bi, i: (bi, 0, 0)),
                  pl.BlockSpec((1, nbc, LANE), lambda bi, i: (bi, 0, 0)),
                  pl.BlockSpec((1, s, 768), lambda bi, i: (bi, 0, 0))],
        out_specs=pl.BlockSpec((tq, 512), lambda bi, i: (bi * nq + i, 0)),
        out_shape=jax.ShapeDtypeStruct((b * s, 512), BF16),
        scratch_shapes=[pltpu.VMEM((NSA_GROUP, tq, LANE), F32), pltpu.VMEM((NSA_GROUP, tq, LANE), F32),
                        pltpu.VMEM((max(1, nbs // HEAD_DIM), NSA_GROUP, tq, LANE), BF16),
                        pltpu.VMEM((2, NSA_GROUP, tq, kt), F32),
                        pltpu.VMEM((NSA_GROUP, tq, LANE), F32), pltpu.VMEM((NSA_GROUP, tq, LANE), BF16)],
        compiler_params=_params(("arbitrary", "arbitrary")),
        name="nsa_prompt",
    )(qb, small, kcp, vcp, kvb3)


def _nsa_sample_kernel(pt_ref, q_ref, small_ref, rows_ref, wnew_ref, wcache_ref, a_ref, w_ref, pair_ref, *rest,
                       n_pp, kt):
    page_refs = rest[:n_pp]
    o_ref = rest[n_pp]
    ksel, vsel, pooled, s_scr = rest[n_pp + 1:]
    pg = pl.program_id(1)
    npg = pooled.shape[0]
    past = ksel.shape[1]
    nblk = past // SEL_BLOCK
    a_t = a_ref[...]
    per_page = PAGE_SIZE // CMP_BLOCK
    t_i = lax.broadcasted_iota(jnp.int32, (PAGE_SIZE, per_page * n_pp), 0) // CMP_BLOCK
    n_i = lax.broadcasted_iota(jnp.int32, (PAGE_SIZE, per_page * n_pp), 1)
    acc_p = jnp.zeros((2 * KV_W, per_page * n_pp), F32)
    for k in range(n_pp):
        page = page_refs[k][0, 0]
        xa = page[0:2 * KV_W, :] * a_t
        seg = jnp.where(n_i == t_i + per_page * k, 1.0, 0.0).astype(BF16)
        hi = xa.astype(BF16)
        lo = (xa - hi.astype(F32)).astype(BF16)
        acc_p = acc_p + _dot(hi, seg) + _dot(lo, seg)
        c0 = pl.multiple_of((pg * n_pp + k) * PAGE_SIZE, PAGE_SIZE)
        ksel[:, pl.ds(c0, PAGE_SIZE)] = page[256:384, :].astype(BF16)
        vsel[:, pl.ds(c0, PAGE_SIZE)] = page[384:512, :].astype(BF16)
    pooled[pg] = acc_p

    @pl.when(pg == pl.num_programs(1) - 1)
    def _finish():
        nq = q_ref.shape[1]
        cmp_t = [_dot(w_ref[...], pooled[j].astype(BF16)) for j in range(npg)]
        kc = jnp.concatenate([c[0:KV_W] for c in cmp_t], axis=1).astype(BF16)
        vc = jnp.concatenate([c[KV_W:2 * KV_W] for c in cmp_t], axis=1).astype(BF16)
        qf = q_ref[0]
        gates = small_ref[0]
        qg = [_group_queries(qf, g) for g in range(NSA_KV_HEADS)]
        blocks = []
        for r in range(NSA_GROUP):
            blocks += [qg[0][r], qg[1][r]]
        qz = jnp.concatenate(blocks, axis=0).astype(BF16)
        rows_n = qz.shape[0]
        rpr = NSA_KV_HEADS * nq
        qidx_c = lax.broadcasted_iota(jnp.int32, (rows_n, 1), 0) % nq
        s_c = _dot(qz, kc)
        e_c = jnp.exp2(s_c - jnp.max(s_c, axis=-1, keepdims=True))
        p_c = e_c * (1.0 / jnp.sum(e_c, axis=-1, keepdims=True))
        o_c = _dot_nt(p_c.astype(BF16), vc)
        p_sum = p_c[0:rpr] + p_c[rpr:2 * rpr] + p_c[2 * rpr:3 * rpr] + p_c[3 * rpr:4 * rpr]
        imp = _dot01_r(p_sum, pair_ref[...])
        jcol = lax.broadcasted_iota(jnp.int32, imp.shape, 1)
        score = imp + jnp.where(jcol == 0, FORCE, 0.0)
        sel32 = _topk_select_lanes(score, jcol.astype(F32), min(SEL_TOPN - 1, nblk))
        sel_b = jnp.concatenate([sel32] * NSA_GROUP, axis=0).astype(BF16)

        def scores(t, slot):
            k_t = ksel[:, pl.ds(pl.multiple_of(t * kt, kt), kt)]
            s_scr[slot] = _dot(qz, k_t) + jnp.where(_expand_tile(sel_b, t, kt) > 0.5, 0.0, NEG)

        def consume(t, slot, carry):
            m_old, l_old, acc = carry
            v_t = vsel[:, pl.ds(pl.multiple_of(t * kt, kt), kt)]
            s = s_scr[slot]
            m_new = jnp.maximum(m_old, jnp.max(s, axis=-1, keepdims=True))
            p = jnp.exp2(s - m_new)
            alpha = jnp.exp2(m_old - m_new)
            l_new = alpha * l_old + jnp.sum(p, axis=-1, keepdims=True)
            return m_new, l_new, alpha * acc + _dot_nt(p.astype(BF16), v_t)

        def pair_body(tp, carry):
            scores(2 * tp + 1, 1)
            carry = consume(2 * tp, 0, carry)
            scores(2 * tp + 2, 0)
            return consume(2 * tp + 1, 1, carry)

        n_t = past // kt
        scores(0, 0)
        init = (jnp.full((rows_n, 1), NEG, F32), jnp.zeros((rows_n, 1), F32), jnp.zeros((rows_n, LANE), F32))
        carry = lax.fori_loop(0, n_t // 2 - 1, pair_body, init)
        scores(n_t - 1, 1)
        carry = consume(n_t - 2, 0, carry)
        m_old, l_old, acc = consume(n_t - 1, 1, carry)
        newr = rows_ref[0]
        zk = jnp.zeros((LANE - nq, LANE), F32)
        k_n = jnp.concatenate([newr[:, 256:384], zk], axis=0).astype(BF16)
        v_n = jnp.concatenate([newr[:, 384:512], zk], axis=0).astype(BF16)
        ccol = lax.broadcasted_iota(jnp.int32, (1, LANE), 1)
        allow_n = (ccol <= qidx_c) & (ccol < nq)
        sm = jnp.where(allow_n, _dot_nt(qz, k_n), NEG)
        m_new = jnp.maximum(m_old, jnp.max(sm, axis=-1, keepdims=True))
        p = jnp.where(allow_n, jnp.exp2(sm - m_new), 0.0)
        alpha = jnp.exp2(m_old - m_new)
        l_new = alpha * l_old + jnp.sum(p, axis=-1, keepdims=True)
        acc = alpha * acc + _dot(p.astype(BF16), v_n)
        o_s = acc * (1.0 / jnp.maximum(l_new, 1e-30))
        wc = wcache_ref[0, 0]
        wn = wnew_ref[0]
        n_keep = wc.shape[1]
        k_wn = jnp.concatenate([wn[:, 0:KV_W], zk], axis=0).astype(BF16)
        v_wn = jnp.concatenate([wn[:, KV_W:2 * KV_W], zk], axis=0).astype(BF16)
        s_w = jnp.concatenate([_dot(qz, wc[0:KV_W].astype(BF16)), _dot_nt(qz, k_wn)], axis=1)
        wcol = lax.broadcasted_iota(jnp.int32, (1, n_keep + LANE), 1)
        allow_w = (((wcol < n_keep) & (wcol + (WINDOW - n_keep) > qidx_c))
                   | ((wcol >= n_keep) & (wcol - n_keep <= qidx_c) & (wcol - n_keep < nq)))
        p_w = _softmax_rows(s_w, allow_w).astype(BF16)
        o_w = _dot_nt(p_w[:, 0:n_keep], wc[KV_W:2 * KV_W].astype(BF16)) + _dot(p_w[:, n_keep:], v_wn)
        slabs = []
        for g in range(NSA_KV_HEADS):
            pieces = []
            for r in range(NSA_GROUP):
                h = NSA_GROUP * g + r
                sl = slice(r * rpr + g * nq, r * rpr + (g + 1) * nq)
                pieces.append(gates[:, 3 * h:3 * h + 1] * o_c[sl] + gates[:, 3 * h + 1:3 * h + 2] * o_s[sl]
                              + gates[:, 3 * h + 2:3 * h + 3] * o_w[sl])
            slabs.extend(_place_heads(pieces, g))
        for j in range(4):
            o_ref[0, :, j * LANE:(j + 1) * LANE] = slabs[j]


def _topk_select_lanes(v, jcol, n_rounds):
    big = float(v.shape[1] + 1)

    def body(_, r):
        m = jnp.max(r, axis=1, keepdims=True)
        first = jnp.min(jnp.where(r == m, jcol, big), axis=1, keepdims=True)
        return jnp.where(jcol == first, -jnp.inf, r)

    res = lax.fori_loop(0, n_rounds, body, v)
    return jnp.where((res == -jnp.inf) & (v > 0.5 * NEG), 1.0, 0.0)


def _nsa_sample(page_table, q3, small3, rows3, wnew3, wcache_t, a_t, w_bd_t, pair, cache_t, *, layer, kt):
    b, n_pages = page_table.shape
    nq = q3.shape[1]
    past = n_pages * PAGE_SIZE
    n_pp = PAGES_PER_STEP
    assert n_pages % n_pp == 0 and past % kt == 0 and nq % 8 == 0 and nq < CMP_BLOCK
    assert n_pp * PAGE_SIZE // CMP_BLOCK == LANE and (past // kt) % 2 == 0
    npg = n_pages // n_pp
    per = lambda w: pl.BlockSpec((1, nq, w), lambda bi, pg, pt: (bi, 0, 0))
    page_specs = [pl.BlockSpec((1, 1, 512, PAGE_SIZE), lambda bi, pg, pt, k=k: (layer, pt[bi, pg * n_pp + k], 0, 0))
                  for k in range(n_pp)]
    const = lambda shape: pl.BlockSpec(shape, lambda bi, pg, pt: (0,) * len(shape))
    grid_spec = pltpu.PrefetchScalarGridSpec(
        num_scalar_prefetch=1,
        grid=(b, npg),
        in_specs=[per(512), per(LANE), per(512), per(256),
                  pl.BlockSpec((1, 1, 256, wcache_t.shape[3]), lambda bi, pg, pt: (layer, bi, 0, 0)),
                  const(a_t.shape), const(w_bd_t.shape), const(pair.shape)] + page_specs,
        out_specs=per(512),
        scratch_shapes=[pltpu.VMEM((LANE, past), BF16), pltpu.VMEM((LANE, past), BF16),
                        pltpu.VMEM((npg, 2 * KV_W, LANE), F32),
                        pltpu.VMEM((2, NSA_HEADS * nq, kt), F32)],
    )
    return pl.pallas_call(
        functools.partial(_nsa_sample_kernel, n_pp=n_pp, kt=kt),
        grid_spec=grid_spec,
        out_shape=jax.ShapeDtypeStruct((b, nq, 512), F32),
        compiler_params=_params(("arbitrary", "arbitrary")),
        name="nsa_sample",
    )(page_table, q3, small3, rows3, wnew3, wcache_t, a_t, w_bd_t, pair, *([cache_t] * n_pp))


def _mlstm_kernel(mqk_ref, mv_ref, mo_ref, small_ref, smallt_ref, hist_ref, ext0_ref, m0_ref, cw_ref, cb_ref,
                  wq_ref, wk_ref, ng_ref, hm_ref, ext_ref, m_ref, ext_scr, m_scr, carry_scr, *, L):
    c = pl.program_id(1)

    @pl.when(c == 0)
    def _init():
        ext_scr[...] = ext0_ref[0]
        m_scr[...] = m0_ref[0]
        carry_scr[...] = hist_ref[0]

    x = mqk_ref[0]
    tl = x.shape[0]
    full = jnp.concatenate([carry_scr[...], x], axis=0)
    y = cb_ref[...]
    for j in range(MLSTM_CONV):
        y = y + cw_ref[j:j + 1, :] * full[8 - (MLSTM_CONV - 1) + j:8 - (MLSTM_CONV - 1) + j + tl]
    carry_scr[...] = full[tl:tl + 8]
    qk = _silu(y).astype(BF16)
    q_all = _dot(qk, wq_ref[...])
    k_all = _dot(qk, wk_ref[...]) * (HEAD_DIM ** -0.5)
    lane = lax.broadcasted_iota(jnp.int32, (L, LANE), 1)
    lane_row = lax.broadcasted_iota(jnp.int32, (1, LANE), 1)
    t_i = lax.broadcasted_iota(jnp.int32, (L, L), 0)
    s_i = lax.broadcasted_iota(jnp.int32, (L, L), 1)
    causal = s_i <= t_i
    tril_b = jnp.where(causal, 1.0, 0.0).astype(BF16)
    triu_b = jnp.where(t_i <= s_i, 1.0, 0.0).astype(BF16)
    eye_b = jnp.where(lax.broadcasted_iota(jnp.int32, (LANE, LANE), 0) == lax.broadcasted_iota(jnp.int32, (LANE, LANE), 1),
                      1.0, 0.0).astype(BF16)
    for u in range(tl // L):
        sl = slice(u * L, (u + 1) * L)
        sm = small_ref[0, sl, :]
        smt = smallt_ref[0, :, sl]
        bcols = _dot01(tril_b, sm)
        brows = _dot01_r(smt, triu_b)
        mrow = m_scr[...]
        for j in range(2):
            cs = slice(j * LANE, (j + 1) * LANE)
            q_slab = q_all[sl, cs]
            k_b = k_all[sl, cs].astype(BF16)
            v_slab = mv_ref[0, sl, cs].astype(F32)
            k_t = _dot_nt(eye_b, k_b).astype(BF16)
            outs = []
            for half in range(2):
                h = 2 * j + half
                keep = (lane >= HEAD_DIM * half) & (lane < HEAD_DIM * (half + 1))
                qm = jnp.where(keep, q_slab, 0.0).astype(BF16)
                v_ext = jnp.where(keep, v_slab, 1.0)
                b_col = bcols[:, 28 + h:29 + h]
                li_col = sm[:, 24 + h:25 + h]
                b_row = brows[4 + h:5 + h, :]
                li_row = smt[h:h + 1, :]
                m_h = mrow[:, h:h + 1]
                b_rep = jnp.broadcast_to(b_col, (L, LANE))
                d = jnp.where(causal, b_rep - b_row + li_row, NEG)
                inter = b_rep + m_h
                mt = jnp.maximum(inter, jnp.max(d, axis=-1, keepdims=True))
                a = _dot_nt(qm, k_b) * jnp.exp(d - mt)
                wi = jnp.exp(inter - mt)
                cts = ext_scr[h]
                num = wi * _dot(qm, cts.astype(BF16)) + _dot(a.astype(BF16), v_ext.astype(BF16))
                den = jnp.maximum(jnp.abs(pltpu.roll(num, HEAD_DIM, axis=1)), jnp.exp(-mt))
                outs.append(num / den)
                bl = b_col[L - 1:L, :]
                g_col = bl - b_col + li_col
                m_new = jnp.maximum(bl + m_h, jnp.max(g_col, axis=0, keepdims=True))
                w_col = jnp.exp(g_col - m_new)
                dec = jnp.exp(bl + m_h - m_new)
                ext_scr[h] = dec * cts + _dot(k_t, (w_col * v_ext).astype(BF16))
                mrow = jnp.where(lane_row == h, m_new, mrow)
            hs = jnp.where(lane < HEAD_DIM, outs[0], outs[1])
            sq = hs * hs
            s0 = jnp.sum(jnp.where(lane < HEAD_DIM, sq, 0.0), axis=-1, keepdims=True)
            s1 = jnp.sum(jnp.where(lane < HEAD_DIM, 0.0, sq), axis=-1, keepdims=True)
            rs = jnp.where(lane < HEAD_DIM, lax.rsqrt(s0 * (1.0 / HEAD_DIM) + EPS), lax.rsqrt(s1 * (1.0 / HEAD_DIM) + EPS))
            hm_ref[0, sl, cs] = (mo_ref[0, sl, cs] * (hs * rs * ng_ref[:, cs])).astype(BF16)
        m_scr[...] = mrow

    @pl.when(c == pl.num_programs(1) - 1)
    def _out():
        ext_ref[0] = ext_scr[...]
        m_ref[0] = m_scr[...]


def _mlstm(mqk3, mv3, mo3, small3, smallt3, hist, ext0, m0, cw, cb, wq, wk, ng, *, tl):
    b, s, _ = mqk3.shape
    L = MLSTM_L
    assert s % tl == 0 and tl % L == 0 and L == LANE
    blk = lambda w: pl.BlockSpec((1, tl, w), lambda bi, c: (bi, c, 0))
    st = lambda shape: pl.BlockSpec((1,) + shape, lambda bi, c: (bi,) + (0,) * len(shape))
    const = lambda shape: pl.BlockSpec(shape, lambda bi, c: (0,) * len(shape))
    return pl.pallas_call(
        functools.partial(_mlstm_kernel, L=L),
        grid=(b, s // tl),
        in_specs=[blk(256), blk(256), blk(256), blk(LANE),
                  pl.BlockSpec((1, 8, tl), lambda bi, c: (bi, 0, c)),
                  st((8, 256)), st((MLSTM_HEADS, LANE, LANE)), st((1, LANE)),
                  const((MLSTM_CONV, 256)), const((1, 256)), const((256, 256)), const((256, 256)), const((1, 256))],
        out_specs=[blk(256), st((MLSTM_HEADS, LANE, LANE)), st((1, LANE))],
        out_shape=(jax.ShapeDtypeStruct((b, s, 256), BF16),
                   jax.ShapeDtypeStruct((b, MLSTM_HEADS, LANE, LANE), F32),
                   jax.ShapeDtypeStruct((b, 1, LANE), F32)),
        scratch_shapes=[pltpu.VMEM((MLSTM_HEADS, LANE, LANE), F32), pltpu.VMEM((1, LANE), F32),
                        pltpu.VMEM((8, 256), F32)],
        compiler_params=_params(("arbitrary", "arbitrary")),
        name="mlstm",
    )(mqk3, mv3, mo3, small3, smallt3, hist, ext0, m0, cw, cb, wq, wk, ng)


FF_CHUNK = 1408


def _post_kernel(*refs, final, segmented, tiles_per_batch, seg_len):
    if segmented:
        (x_ref, onsa_ref, hm_ref, ocm_ref, mod_ref, wout_ref, g2_ref, wup_ref, cw_ref, cb_ref, wdn_ref, fg_ref,
         h1_ref, h2_ref, y_ref, fst_ref) = refs
        carry_scr = None
    else:
        (x_ref, onsa_ref, hm_ref, ocm_ref, mod_ref, wout_ref, g2_ref, wup_ref, cw_ref, cb_ref, wdn_ref, fg_ref,
         y_ref, fst_ref, carry_scr) = refs
    x = x_ref[...]
    tm = x.shape[0]
    mix = (_dot(onsa_ref[...].astype(BF16), wout_ref[0:512, :]) + _dot(hm_ref[...], wout_ref[512:768, :])
           + _dot(ocm_ref[...], wout_ref[768:1024, :]))
    mod = mod_ref[0]
    x1 = x + mod[:, 2 * D_MODEL:3 * D_MODEL] * mix
    h2 = x1 * lax.rsqrt(jnp.mean(x1 * x1, axis=-1, keepdims=True) + EPS) * g2_ref[...]
    hb = (h2 * (1.0 + mod[:, 4 * D_MODEL:5 * D_MODEL]) + mod[:, 3 * D_MODEL:4 * D_MODEL]).astype(BF16)
    if not segmented:
        @pl.when(pl.program_id(0) % tiles_per_batch == 0)
        def _reset():
            carry_scr[...] = jnp.zeros(carry_scr.shape, F32)
    else:
        tmod = lax.broadcasted_iota(jnp.int32, (tm, 1), 0) % seg_len
    acc = jnp.zeros((tm, D_MODEL), F32)
    for c0 in range(0, D_FF, FF_CHUNK):
        parts = []
        for part in range(2):
            cs = slice(part * D_FF + c0, part * D_FF + c0 + FF_CHUNK)
            up = _dot(hb, wup_ref[:, cs])
            prev = jnp.zeros((8, FF_CHUNK), F32) if segmented else carry_scr[:, cs]
            full = jnp.concatenate([prev, up], axis=0)
            s1 = full[7:7 + tm]
            s2 = full[6:6 + tm]
            if segmented:
                s1 = jnp.where(tmod >= 1, s1, h1_ref[:, cs])
                s2 = jnp.where(tmod >= 2, s2, h2_ref[:, cs])
                fst_ref[:, cs] = up
            else:
                carry_scr[:, cs] = up[tm - 8:tm]
                fst_ref[0, :, cs] = up[tm - 8:tm]
            parts.append(cb_ref[:, cs] + cw_ref[0:1, cs] * s2 + cw_ref[1:2, cs] * s1 + cw_ref[2:3, cs] * up)
        act = (_silu(parts[0]) * parts[1]).astype(BF16)
        acc = acc + _dot(act, wdn_ref[c0:c0 + FF_CHUNK, :])
    x2 = x1 + mod[:, 5 * D_MODEL:6 * D_MODEL] * acc
    if final:
        x2 = x2 * lax.rsqrt(jnp.mean(x2 * x2, axis=-1, keepdims=True) + EPS) * fg_ref[...]
    y_ref[...] = x2


def _post(x2d, onsa, hm, ocm, mod, wout, g2, wup, cw, cb, wdn, fg, h1=None, h2=None, *, tm, tiles_per_batch, final,
          seg_len=0):
    t_total = x2d.shape[0]
    n_tiles = t_total // tm
    segmented = h1 is not None
    r = mod.shape[1]
    row = lambda w: pl.BlockSpec((tm, w), lambda i: (i, 0))
    in_specs = [row(D_MODEL), row(512), row(256), row(256),
                pl.BlockSpec((1, r, 6 * D_MODEL), lambda i: (i // tiles_per_batch, 0, 0)),
                _const_spec((D_MODEL, D_MODEL)), _const_spec((1, D_MODEL)), _const_spec((D_MODEL, 2 * D_FF)),
                _const_spec((FFN_CONV, 2 * D_FF)), _const_spec((1, 2 * D_FF)), _const_spec((D_FF, D_MODEL)),
                _const_spec((1, D_MODEL))]
    args = [x2d, onsa, hm, ocm, mod, wout, g2, wup, cw, cb, wdn, fg]
    if segmented:
        in_specs += [row(2 * D_FF), row(2 * D_FF)]
        args += [h1, h2]
        out_specs = [row(D_MODEL), row(2 * D_FF)]
        out_shape = (jax.ShapeDtypeStruct((t_total, D_MODEL), F32), jax.ShapeDtypeStruct((t_total, 2 * D_FF), F32))
        scratch = []
    else:
        nb = n_tiles // tiles_per_batch
        out_specs = [row(D_MODEL), pl.BlockSpec((1, 8, 2 * D_FF), lambda i: (i // tiles_per_batch, 0, 0))]
        out_shape = (jax.ShapeDtypeStruct((t_total, D_MODEL), F32), jax.ShapeDtypeStruct((nb, 8, 2 * D_FF), F32))
        scratch = [pltpu.VMEM((8, 2 * D_FF), F32)]
    return pl.pallas_call(
        functools.partial(_post_kernel, final=final, segmented=segmented, tiles_per_batch=tiles_per_batch,
                          seg_len=seg_len),
        grid=(n_tiles,),
        in_specs=in_specs, out_specs=out_specs, out_shape=out_shape, scratch_shapes=scratch,
        compiler_params=_params(("arbitrary",)),
        name="post",
    )(*args)


def _col_perm():
    o = IN_OFF
    cols = list(range(o[0], o[4])) + list(range(o[5], o[8])) + list(range(o[10], o[12]))
    small = list(range(o[4], o[5])) + list(range(o[8], o[10]))
    small += [IN_W] * (LANE - len(small))
    return np.array(cols + small, dtype=np.int32)


def _block_diag(mats):
    n = len(mats)
    rows = []
    for i, m in enumerate(mats):
        rows.append(jnp.concatenate([m if j == i else jnp.zeros((m.shape[0], mats[j].shape[1]), m.dtype)
                                     for j in range(n)], axis=1))
    return jnp.concatenate(rows, axis=0)


def _rope_tables(pos):
    half = HEAD_DIM // 2
    inv = jnp.power(ROPE_THETA, -jnp.arange(half, dtype=F32) / half)
    ang = pos.astype(F32)[:, None] * inv[None, :]
    cos = jnp.cos(ang)
    sin = jnp.sin(ang)
    return jnp.tile(cos, (1, 4)), jnp.concatenate([-sin, sin, -sin, sin], axis=1)


def _layer_weights(l, w):
    win_z = jnp.concatenate([w['w_in'][l], jnp.zeros((D_MODEL, 1), F32)], axis=1)
    bin_z = jnp.concatenate([w['b_in'][l], jnp.zeros((1,), F32)])
    perm = _col_perm()
    cmp_a = w['cmp_a'][l]
    cmp_w = w['cmp_w'][l]
    c_ws = jnp.tril(w['c_ws'][l])
    return dict(
        wp=win_z[:, perm].astype(BF16), bp=bin_z[perm][None, :],
        g1=w['norm1_g'][l][None, :], g2=w['norm2_g'][l][None, :],
        cg=w['c_norm_g'][l][None, :], c_ws=c_ws, c_bs=w['c_bs'][l],
        a_tile=jnp.concatenate([cmp_a[0].reshape(CMP_BLOCK, KV_W), cmp_a[1].reshape(CMP_BLOCK, KV_W)], axis=1),
        w_bd=_block_diag([cmp_w[0, 0], cmp_w[0, 1], cmp_w[1, 0], cmp_w[1, 1]]).astype(BF16),
        m_cw=w['m_conv_w'][l], m_cb=w['m_conv_b'][l][None, :],
        wq=_block_diag([w['m_wq'][l][h] for h in range(MLSTM_HEADS)]).astype(BF16),
        wk=_block_diag([w['m_wk'][l][h] for h in range(MLSTM_HEADS)]).astype(BF16),
        ng=w['m_norm_g'][l].reshape(1, MLSTM_W),
        wout=w['w_out'][l].astype(BF16), wup=w['w_up'][l].astype(BF16), wdn=w['w_down'][l].astype(BF16),
        f_cw=w['f_conv_w'][l], f_cb=w['f_conv_b'][l][None, :],
    )


def _mlstm_state_in(c0, n0, m0):
    b = c0.shape[0]
    tiles = []
    for h in range(MLSTM_HEADS):
        r0 = HEAD_DIM * (h % 2)
        ol = HEAD_DIM if h % 2 == 0 else 0
        t = jnp.zeros((b, LANE, LANE), F32)
        t = t.at[:, r0:r0 + HEAD_DIM, r0:r0 + HEAD_DIM].set(jnp.swapaxes(c0[:, h], 1, 2))
        t = t.at[:, r0:r0 + HEAD_DIM, ol:ol + HEAD_DIM].set(
            jnp.broadcast_to(n0[:, h][:, :, None], (b, HEAD_DIM, HEAD_DIM)))
        tiles.append(t)
    m = jnp.zeros((b, 1, LANE), F32).at[:, 0, 0:MLSTM_HEADS].set(m0)
    return jnp.stack(tiles, axis=1), m


def _mlstm_state_out(ext, m):
    cs, ns = [], []
    for h in range(MLSTM_HEADS):
        r0 = HEAD_DIM * (h % 2)
        ol = HEAD_DIM if h % 2 == 0 else 0
        cs.append(jnp.swapaxes(ext[:, h, r0:r0 + HEAD_DIM, r0:r0 + HEAD_DIM], 1, 2))
        ns.append(ext[:, h, r0:r0 + HEAD_DIM, ol])
    return jnp.stack(cs, axis=1), jnp.stack(ns, axis=1), m[:, 0, 0:MLSTM_HEADS]


def _layer_prompt(x2d, mod, lw, fg, *, b, s, final, tm=256):
    assert tm % SEL_KT == 0
    tiles_per_batch = s // tm
    cos_t, sin_t = _rope_tables(jnp.arange(s, dtype=jnp.int32))
    wc = jnp.concatenate([lw['c_ws'][g] for g in range(CMLP_GROUPS)], axis=1).astype(BF16)
    bc = jnp.repeat(lw['c_bs'].T, HEAD_DIM, axis=1)
    rows_t, win, qb, kvb, small, mqk, mv, mo, ocm, _, cmp_rows = _inproj(
        x2d, mod, lw['g1'], lw['wp'], lw['bp'], cos_t, sin_t, wc, bc, lw['cg'],
        tm=tm, tiles_per_batch=tiles_per_batch, tc=CMLP_CHUNK, q_dtype=BF16, rows_feature_major=True)
    nb = s // CMP_BLOCK
    comp = _compress(cmp_rows.reshape(b, s, 2 * KV_W), lw['a_tile'], lw['w_bd'], rows_per_step=min(s, 1024))
    comp = comp.reshape(b, nb // 2, 2, 2 * KV_W).transpose(0, 2, 1, 3).reshape(b, nb, 2 * KV_W)
    onsa = _nsa_prompt(qb, small, comp[..., 0:KV_W], comp[..., KV_W:], kvb.reshape(b, s, 768), b=b, s=s, kt=SEL_KT)
    zeros = lambda *sh: jnp.zeros(sh, F32)
    ext0, m0 = _mlstm_state_in(zeros(b, MLSTM_HEADS, HEAD_DIM, HEAD_DIM), zeros(b, MLSTM_HEADS, HEAD_DIM),
                               zeros(b, MLSTM_HEADS))
    small3 = small.reshape(b, s, LANE)
    smallt = jnp.swapaxes(small3[:, :, 24:32], 1, 2)
    hm, ext, mout = _mlstm(mqk.reshape(b, s, 256), mv.reshape(b, s, 256), mo.reshape(b, s, 256), small3, smallt,
                           zeros(b, 8, 256), ext0, m0, lw['m_cw'], lw['m_cb'], lw['wq'], lw['wk'], lw['ng'], tl=2 * tm)
    c1, n1, m1 = _mlstm_state_out(ext, mout)
    y, fst = _post(x2d, onsa, hm.reshape(b * s, 256), ocm, mod, lw['wout'], lw['g2'], lw['wup'], lw['f_cw'],
                   lw['f_cb'], lw['wdn'], fg, tm=tm, tiles_per_batch=tiles_per_batch, final=final)
    n_keep = min(WINDOW, s)
    kv_rows = jnp.transpose(rows_t.reshape(b, 4, NSA_KV_HEADS, HEAD_DIM, s), (0, 4, 1, 2, 3))
    new_win = win.reshape(b, s, 256)[:, s - n_keep:].reshape(b, n_keep, 2, NSA_KV_HEADS, HEAD_DIM)
    state = (kv_rows, new_win,
             c1, n1, m1, mqk.reshape(b, s, 256)[:, s - (MLSTM_CONV - 1):], fst[:, 8 - (FFN_CONV - 1):])
    return y, state


def _layer_sample(x2d, mod, lw, fg, cache_t, page_table, wcache_t, mc, mn, mm, mconv, fconv, *, layer, b, s, final,
                  kt=512):
    t = b * s
    past = page_table.shape[1] * PAGE_SIZE
    pos = past + jnp.arange(s, dtype=jnp.int32)
    cos1, sin1 = _rope_tables(pos)
    cos_t, sin_t = jnp.tile(cos1, (b, 1)), jnp.tile(sin1, (b, 1))
    eye_b = jnp.eye(b, dtype=F32)
    wc = jnp.concatenate([jnp.kron(eye_b, lw['c_ws'][g][:s, :s]) for g in range(CMLP_GROUPS)], axis=1).astype(BF16)
    bc = jnp.tile(jnp.repeat(lw['c_bs'].T[:s], HEAD_DIM, axis=1), (b, 1))
    rows, win, qf, _, small, mqk, mv, mo, ocm, cvn = _inproj(
        x2d, mod, lw['g1'], lw['wp'], lw['bp'], cos_t, sin_t, wc, bc, lw['cg'],
        tm=t, tiles_per_batch=1, tc=t, q_dtype=F32, rows_feature_major=False)
    nbc = past // CMP_BLOCK
    pair = jnp.repeat(jnp.eye(nbc // 2, dtype=BF16), 2, axis=0)
    n_keep = wcache_t.shape[3]
    wnew3 = win.reshape(b, s, 256)
    a_t = jnp.tile(lw['a_tile'].T, (1, PAGE_SIZE // CMP_BLOCK))
    onsa = _nsa_sample(page_table, qf.reshape(b, s, 512), small.reshape(b, s, LANE), rows.reshape(b, s, 512), wnew3,
                       wcache_t, a_t, lw['w_bd'].T, pair, cache_t, layer=layer, kt=kt).reshape(t, 512)
    L = MLSTM_L
    pad3 = lambda a, fill=0.0: jnp.pad(a.reshape(b, s, -1), ((0, 0), (0, L - s), (0, 0)), constant_values=fill)
    small3 = small.reshape(b, s, LANE)
    lane = jnp.arange(LANE)
    small_p = jnp.where((lane >= 24) & (lane < 28), pad3(small3, NEG), pad3(small3))
    smallt = jnp.swapaxes(small_p[:, :, 24:32], 1, 2)
    hist = jnp.pad(mconv, ((0, 0), (8 - (MLSTM_CONV - 1), 0), (0, 0)))
    ext0, m0 = _mlstm_state_in(mc, mn, mm)
    hm, ext, mout = _mlstm(pad3(mqk), pad3(mv), pad3(mo), small_p, smallt, hist, ext0, m0, lw['m_cw'], lw['m_cb'],
                           lw['wq'], lw['wk'], lw['ng'], tl=L)
    c1, n1, m1 = _mlstm_state_out(ext, mout)
    hm = hm[:, :s].reshape(t, 256)
    h1 = jnp.zeros((b, s, 2 * D_FF), F32).at[:, 0].set(fconv[:, 1]).reshape(t, 2 * D_FF)
    h2 = jnp.zeros((b, s, 2 * D_FF), F32).at[:, 0].set(fconv[:, 0]).at[:, 1].set(fconv[:, 1]).reshape(t, 2 * D_FF)
    y, up = _post(x2d, onsa, hm, ocm, mod, lw['wout'], lw['g2'], lw['wup'], lw['f_cw'], lw['f_cb'], lw['wdn'], fg,
                  h1, h2, tm=t, tiles_per_batch=1, final=final, seg_len=s)
    new_win_t = jnp.concatenate([wcache_t[layer][:, :, s:], jnp.swapaxes(wnew3, 1, 2)], axis=2)
    new_win = jnp.transpose(new_win_t.reshape(b, 2, NSA_KV_HEADS, HEAD_DIM, n_keep), (0, 4, 1, 2, 3))
    state = (rows.reshape(b, s, 4, NSA_KV_HEADS, HEAD_DIM), new_win, c1, n1, m1,
             mqk.reshape(b, s, 256)[:, s - (MLSTM_CONV - 1):], cvn.reshape(b, s, CMLP_W),
             up.reshape(b, s, 2 * D_FF)[:, s - (FFN_CONV - 1):])
    return y, state


def kernel(x_prompt, x_sample, cache_nsa_kv, page_table, cache_win_kv, state_mlstm_C, state_mlstm_n, state_mlstm_m, state_mlstm_conv, state_ffn_conv, c_prompt, c_sample, ada_w, ada_b, norm1_g, norm2_g, w_in, b_in, cmp_a, cmp_w, m_conv_w, m_conv_b, m_wq, m_wk, m_norm_g, c_norm_g, c_ws, c_bs, w_out, w_up, f_conv_w, f_conv_b, w_down, final_norm_g):
    bp, sp, _ = x_prompt.shape
    bs, ss, _ = x_sample.shape
    depth = ada_w.shape[0]
    w = dict(w_in=w_in, b_in=b_in, norm1_g=norm1_g, norm2_g=norm2_g, cmp_a=cmp_a, cmp_w=cmp_w, m_conv_w=m_conv_w,
             m_conv_b=m_conv_b, m_wq=m_wq, m_wk=m_wk, m_norm_g=m_norm_g, c_norm_g=c_norm_g, c_ws=c_ws, c_bs=c_bs,
             w_out=w_out, w_up=w_up, f_conv_w=f_conv_w, f_conv_b=f_conv_b, w_down=w_down)
    n_c = bp + bs
    n_pad = -(-n_c // 8) * 8
    c_all = jnp.pad(jnp.concatenate([c_prompt, c_sample], axis=0), ((0, n_pad - n_c), (0, 0)))
    mod_all = _ada(c_all, ada_w, ada_b)
    fg = final_norm_g[None, :]
    xp = x_prompt.reshape(bp * sp, D_MODEL)
    xs = x_sample.reshape(bs * ss, D_MODEL)
    n_pool = cache_nsa_kv.shape[1]
    n_keep = cache_win_kv.shape[2]
    cache_t = jnp.transpose(cache_nsa_kv, (0, 1, 3, 4, 5, 2)).reshape(depth, n_pool, 512, PAGE_SIZE)
    wcache_t = jnp.transpose(cache_win_kv, (0, 1, 3, 4, 5, 2)).reshape(depth, bs, 256, n_keep)
    st_p, st_s = [], []
    for l in range(depth):
        lw = _layer_weights(l, w)
        final = l == depth - 1
        mod_p = mod_all[l, 0:bp][:, None, :]
        mod_s = jnp.repeat(mod_all[l, bp:bp + bs], ss, axis=0)[None]
        xp, sp_l = _layer_prompt(xp, mod_p, lw, fg, b=bp, s=sp, final=final)
        xs, ss_l = _layer_sample(xs, mod_s, lw, fg, cache_t, page_table, wcache_t, state_mlstm_C[l],
                                 state_mlstm_n[l], state_mlstm_m[l], state_mlstm_conv[l], state_ffn_conv[l],
                                 layer=l, b=bs, s=ss, final=final)
        st_p.append(sp_l)
        st_s.append(ss_l)
    sp_st = [jnp.stack(t) for t in zip(*st_p)]
    ss_st = [jnp.stack(t) for t in zip(*st_s)]
    y_prompt = xp.reshape(bp, sp, D_MODEL)
    y_sample = xs.reshape(bs, ss, D_MODEL)
    return (y_prompt, y_sample, sp_st[0], ss_st[0], sp_st[1], ss_st[1],
            sp_st[2], sp_st[3], sp_st[4], sp_st[5],
            ss_st[2], ss_st[3], ss_st[4], ss_st[5],
            ss_st[6], sp_st[6], ss_st[7])
```
